```python
import math
import jax
import jax.numpy as jnp
from jax import lax
import numpy as np


D_MODEL = 2048
BATCH = 4
SEQ = 8192
DEPTH = 4

GRID_W = 64
N_BRANCH = 4
BRANCH_W = D_MODEL // N_BRANCH
A_HEADS = 4
A_HD = BRANCH_W // A_HEADS
A_PATTERNS = ((128, 1), (512, 4), (2048, 16))
B_HEADS = 4
B_DK = BRANCH_W // (2 * B_HEADS)
Q_BLOCK = 128
C_HEADS = 8
C_KV_HEADS = 2
C_HD = BRANCH_W // C_HEADS
C_RADIUS = 128
D_HEADS = 4
D_HD = BRANCH_W // D_HEADS
D_WIN_R = 8
D_WIN_C = 16
N_EXPERTS = 16
N_GROUPS = 4
EXPERTS_PER_GROUP = N_EXPERTS // N_GROUPS
TOP_GROUPS = 1
TOP_K = 2
D_FF_EXPERT = D_MODEL // 2
RMS_EPS = 1e-6
NEG_INF = -1e30
IN_SIZES = (A_HEADS * A_HD, A_HEADS * A_HD, A_HEADS * A_HD,
            B_HEADS * 2 * B_DK, B_HEADS * 2 * B_DK, B_HEADS * 2 * B_DK,
            C_HEADS * C_HD, C_KV_HEADS * C_HD, C_KV_HEADS * C_HD,
            D_HEADS * D_HD, D_HEADS * D_HD, D_HEADS * D_HD,
            N_BRANCH * D_MODEL)
N_IN = sum(IN_SIZES)

kernel_name = 'hybrid_gated_multimixer_moe_encoder'


def rmsnorm(x, g):
    xf = x.astype(jnp.float32)
    y = xf * lax.rsqrt(jnp.mean(xf * xf, axis=-1, keepdims=True) + RMS_EPS)
    return (y * g.astype(jnp.float32)).astype(x.dtype)


def alibi_slopes(n):
    return jnp.asarray(np.array([2.0 ** (-8.0 * (i + 1) / n) for i in range(n)], dtype=np.float32))


def heads(t, n):
    B, S, W = t.shape
    return t.reshape(B, S, n, W // n).transpose(0, 2, 1, 3)


def merge_heads(t):
    B, H, S, E = t.shape
    return t.transpose(0, 2, 1, 3).reshape(B, S, H * E)


def query_blocks(t, blk, nb):
    L = t.shape[-2]
    pad = [(0, 0)] * (t.ndim - 2) + [(0, nb * blk - L), (0, 0)]
    return jnp.pad(t, pad).reshape(t.shape[:-2] + (nb, blk, t.shape[-1]))


def band_key_blocks(t, blk, nb):
    L = t.shape[-2]
    pad = [(0, 0)] * (t.ndim - 2) + [(blk, nb * blk - L + blk), (0, 0)]
    b = jnp.pad(t, pad).reshape(t.shape[:-2] + (nb + 2, blk, t.shape[-1]))
    return jnp.concatenate([b[..., :-2, :, :], b[..., 1:-1, :, :], b[..., 2:, :, :]], axis=-2)


def band_geometry(L, blk, nb, radius):
    qpos = jnp.arange(nb)[:, None] * blk + jnp.arange(blk)[None, :]
    kpos = (jnp.arange(nb)[:, None] - 1) * blk + jnp.arange(3 * blk)[None, :]
    rel = kpos[:, None, :] - qpos[:, :, None]
    valid = (jnp.abs(rel) <= radius) & (kpos[:, None, :] >= 0) & (kpos[:, None, :] < L)
    return rel, valid


def unstride(t, M, dil):
    B, H = t.shape[:2]
    rest = t.shape[5:]
    t = t.reshape((B, H, dil, -1) + rest)[:, :, :, :M]
    t = jnp.moveaxis(t, 2, 3)
    return t.reshape((B, H, M * dil) + rest)


def dilated_attention(q, k, v, slopes):
    B, H, S, E = q.shape
    scale = E ** -0.5
    ms, ls, accs = [], [], []
    for window, dil in A_PATTERNS:
        radius = window // (2 * dil)
        blk = radius
        M = S // dil
        nb = -(-M // blk)
        strided = lambda t: t.reshape(B, H, M, dil, E).swapaxes(2, 3)
        qb = query_blocks(strided(q), blk, nb)
        kb = band_key_blocks(strided(k), blk, nb)
        vb = band_key_blocks(strided(v), blk, nb)
        rel, valid = band_geometry(M, blk, nb, radius)
        s = jnp.einsum('bhrnqe,bhrnke->bhrnqk', qb, kb).astype(jnp.float32) * scale
        s = s - slopes[:, None, None, None, None] * (dil * jnp.abs(rel)).astype(jnp.float32)
        s = jnp.where(valid, s, NEG_INF)
        m = jnp.max(s, axis=-1, keepdims=True)
        p = jnp.exp(s - m)
        l = jnp.sum(p, axis=-1, keepdims=True)
        acc = jnp.einsum('bhrnqk,bhrnke->bhrnqe', p, vb.astype(jnp.float32))
        ms.append(unstride(m, M, dil))
        ls.append(unstride(l, M, dil))
        accs.append(unstride(acc, M, dil))
    m_all = jnp.stack(ms)
    w = jnp.exp(m_all - jnp.max(m_all, axis=0, keepdims=True))
    num = jnp.sum(w * jnp.stack(accs), axis=0)
    den = jnp.sum(w * jnp.stack(ls), axis=0)
    return num / den


def diff_attention(q1, q2, k1, k2, v, lam, slopes, sub_g, lam_init):
    B, H, S, E = q1.shape
    scale = E ** -0.5
    nq = S // Q_BLOCK
    kpos = jnp.arange(S)
    qblk = lambda t: t.reshape(B, H, nq, Q_BLOCK, E).transpose(2, 0, 1, 3, 4)
    vf = v.astype(jnp.float32)

    def one_block(args):
        q1b, q2b, qpos = args
        bias = -slopes[:, None, None] * jnp.abs(qpos[:, None] - kpos[None, :]).astype(jnp.float32)
        a1 = jax.nn.softmax(jnp.einsum('bhqe,bhke->bhqk', q1b, k1).astype(jnp.float32) * scale + bias, axis=-1)
        a2 = jax.nn.softmax(jnp.einsum('bhqe,bhke->bhqk', q2b, k2).astype(jnp.float32) * scale + bias, axis=-1)
        return jnp.einsum('bhqk,bhke->bhqe', a1 - lam[:, None, None] * a2, vf)

    qpos = jnp.arange(S).reshape(nq, Q_BLOCK)
    o = lax.map(one_block, (qblk(q1), qblk(q2), qpos))
    o = o.transpose(1, 2, 0, 3, 4).reshape(B, H, S, 2 * E)
    return rmsnorm(o, sub_g) * (1.0 - lam_init)


def window_gqa_sink(q, k, v, slopes, sink):
    B, HQ, S, E = q.shape
    KV = k.shape[1]
    G = HQ // KV
    blk = C_RADIUS
    nb = -(-S // blk)
    scale = E ** -0.5
    qb = query_blocks(q.reshape(B, KV, G, S, E), blk, nb)
    kb = band_key_blocks(k, blk, nb)
    vb = band_key_blocks(v, blk, nb)
    rel, valid = band_geometry(S, blk, nb, C_RADIUS)
    s = jnp.einsum('bkgnqe,bknte->bkgnqt', qb, kb).astype(jnp.float32) * scale
    s = s - slopes.reshape(KV, G)[:, :, None, None, None] * jnp.abs(rel).astype(jnp.float32)
    s = jnp.where(valid, s, NEG_INF)
    sk = sink.astype(jnp.float32).reshape(KV, G)[:, :, None, None, None]
    m = jnp.maximum(jnp.max(s, axis=-1, keepdims=True), sk)
    p = jnp.exp(s - m)
    den = jnp.sum(p, axis=-1, keepdims=True) + jnp.exp(sk - m)
    o = jnp.einsum('bkgnqt,bknte->bkgnqe', p / den, vb.astype(jnp.float32))
    return o.reshape(B, KV, G, nb * blk, E)[:, :, :, :S].reshape(B, HQ, S, E)


def neighborhood_attention(q, k, v, rpb):
    B, H, S, E = q.shape
    rows = S // GRID_W
    wr = min(D_WIN_R, rows)
    wc = D_WIN_C
    scale = E ** -0.5
    grid = lambda t: t.reshape(B, H, rows, GRID_W, E)
    r = jnp.arange(rows)
    cc = jnp.arange(GRID_W)
    rs = jnp.clip(r - wr // 2, 0, rows - wr)
    ridx = rs[:, None] + jnp.arange(wr)[None, :]
    K = wr * GRID_W
    kg = grid(k)[:, :, ridx].reshape(B, H, rows, K, E)
    vg = grid(v)[:, :, ridx].reshape(B, H, rows, K, E)
    s = jnp.einsum('bhrqe,bhrke->bhrqk', grid(q), kg).astype(jnp.float32) * scale
    cs = jnp.clip(cc - wc // 2, 0, GRID_W - wc)
    kc = jnp.tile(jnp.arange(GRID_W), wr)
    col_ok = (kc[None, :] >= cs[:, None]) & (kc[None, :] < cs[:, None] + wc)
    dr_i = jnp.repeat(ridx - r[:, None], GRID_W, axis=1) + (D_WIN_R - 1)
    dc_i = jnp.clip(kc[None, :] - cc[:, None] + (D_WIN_C - 1), 0, 2 * D_WIN_C - 2)
    bias = rpb[:, dr_i[:, None, :], dc_i[None, :, :]].astype(jnp.float32)
    s = jnp.where(col_ok, s + bias, NEG_INF)
    a = jax.nn.softmax(s, axis=-1)
    return jnp.einsum('bhrqk,bhrke->bhrqe', a, vg.astype(jnp.float32)).reshape(B, H, S, E)


def moe(h, router_w, router_b, w1, w3, w2):
    B, S, D = h.shape
    T = B * S
    hf = h.reshape(T, D)
    scores = jax.nn.sigmoid((hf @ router_w).astype(jnp.float32))
    sel = scores + router_b.astype(jnp.float32)
    gscore = lax.top_k(sel.reshape(T, N_GROUPS, EXPERTS_PER_GROUP), TOP_K)[0].sum(-1)
    _, gidx = lax.top_k(gscore, TOP_GROUPS)
    group_keep = (jnp.arange(N_GROUPS)[None, None, :] == gidx[:, :, None]).any(1)
    masked = jnp.where(jnp.repeat(group_keep, EXPERTS_PER_GROUP, axis=1), sel, NEG_INF)
    _, eidx = lax.top_k(masked, TOP_K)
    gw = jnp.take_along_axis(scores, eidx, axis=1)
    gw = gw / jnp.sum(gw, axis=-1, keepdims=True)
    flat_e = eidx.reshape(-1)
    order = jnp.argsort(flat_e)
    tok = order // TOP_K
    xs = hf[tok]
    sizes = jnp.bincount(flat_e, length=N_EXPERTS).astype(jnp.int32)
    g = lax.ragged_dot(xs, w1, sizes)
    u = lax.ragged_dot(xs, w3, sizes)
    y = lax.ragged_dot((jax.nn.silu(g) * u).astype(xs.dtype), w2, sizes)
    y = y * gw.reshape(-1)[order][:, None].astype(y.dtype)
    return jnp.zeros_like(hf).at[tok].add(y.astype(hf.dtype)).reshape(B, S, D)


def setup_inputs(seed: int = 0) -> dict:
    key = jax.random.key(seed)
    ks = jax.random.split(key, 24)
    f32 = jnp.float32
    nrm = lambda k, shape, sc: jax.random.normal(k, shape, f32) * sc
    return {
        'x': nrm(ks[0], (BATCH, SEQ, D_MODEL), 1.0),
        'c': nrm(ks[1], (BATCH, D_MODEL), 1.0),
        'ada_w': nrm(ks[2], (DEPTH, D_MODEL, 6 * D_MODEL), 0.5 * D_MODEL ** -0.5),
        'ada_b': nrm(ks[3], (DEPTH, 6 * D_MODEL), 0.02),
        'norm1_g': 1.0 + nrm(ks[4], (DEPTH, D_MODEL), 0.02),
        'norm2_g': 1.0 + nrm(ks[5], (DEPTH, D_MODEL), 0.02),
        'w_in': nrm(ks[6], (DEPTH, D_MODEL, N_IN), D_MODEL ** -0.5),
        'qk_norm_a': 1.0 + nrm(ks[7], (DEPTH, 2, A_HD), 0.02),
        'qk_norm_b': 1.0 + nrm(ks[8], (DEPTH, 2, B_DK), 0.02),
        'qk_norm_c': 1.0 + nrm(ks[9], (DEPTH, 2, C_HD), 0.02),
        'qk_norm_d': 1.0 + nrm(ks[10], (DEPTH, 2, D_HD), 0.02),
        'diff_lambda': nrm(ks[11], (DEPTH, 4, B_HEADS, B_DK), 0.1),
        'diff_subln_g': 1.0 + nrm(ks[12], (DEPTH, 2 * B_DK), 0.02),
        'sink_logits': nrm(ks[13], (DEPTH, C_HEADS), 0.5),
        'rpb': nrm(ks[14], (DEPTH, D_HEADS, 2 * D_WIN_R - 1, 2 * D_WIN_C - 1), 0.1),
        'w_branch': nrm(ks[15], (DEPTH, N_BRANCH, BRANCH_W, D_MODEL), BRANCH_W ** -0.5),
        'w_out': nrm(ks[16], (DEPTH, D_MODEL, D_MODEL), D_MODEL ** -0.5),
        'router_w': nrm(ks[17], (D_MODEL, N_EXPERTS), D_MODEL ** -0.5),
        'router_b': nrm(ks[18], (N_EXPERTS,), 0.01),
        'w1': nrm(ks[19], (DEPTH, N_EXPERTS, D_MODEL, D_FF_EXPERT), D_MODEL ** -0.5),
        'w3': nrm(ks[20], (DEPTH, N_EXPERTS, D_MODEL, D_FF_EXPERT), D_MODEL ** -0.5),
        'w2': nrm(ks[21], (DEPTH, N_EXPERTS, D_FF_EXPERT, D_MODEL), D_FF_EXPERT ** -0.5),
    }


def reference(x, c, ada_w, ada_b, norm1_g, norm2_g, w_in, qk_norm_a, qk_norm_b, qk_norm_c, qk_norm_d,
              diff_lambda, diff_subln_g, sink_logits, rpb, w_branch, w_out, router_w, router_b, w1, w3, w2):
    dt = x.dtype
    B, S, _ = x.shape
    split_idx = [int(i) for i in np.cumsum(IN_SIZES)[:-1]]
    slopes_a = alibi_slopes(A_HEADS)
    slopes_b = alibi_slopes(B_HEADS)
    slopes_c = alibi_slopes(C_HEADS)
    cond = jax.nn.silu(c)
    for l in range(DEPTH):
        mod = (cond @ ada_w[l] + ada_b[l])[:, None, :]
        sh1, sc1, g1, sh2, sc2, g2 = jnp.split(mod, 6, axis=-1)
        h = rmsnorm(x, norm1_g[l]) * (1 + sc1) + sh1
        z = h @ w_in[l]
        qa, ka, va, qb, kb, vb, qc, kc, vc, qd, kd, vd, gates = jnp.split(z, split_idx, axis=-1)
        o_a = dilated_attention(rmsnorm(heads(qa, A_HEADS), qk_norm_a[l, 0]),
                                rmsnorm(heads(ka, A_HEADS), qk_norm_a[l, 1]),
                                heads(va, A_HEADS), slopes_a)
        qb_h = heads(qb, B_HEADS)
        kb_h = heads(kb, B_HEADS)
        lam_init = 0.8 - 0.6 * math.exp(-0.3 * l)
        lam = (jnp.exp(jnp.sum(diff_lambda[l, 0] * diff_lambda[l, 1], axis=-1))
               - jnp.exp(jnp.sum(diff_lambda[l, 2] * diff_lambda[l, 3], axis=-1)) + lam_init).astype(jnp.float32)
        o_b = diff_attention(rmsnorm(qb_h[..., :B_DK], qk_norm_b[l, 0]), rmsnorm(qb_h[..., B_DK:], qk_norm_b[l, 0]),
                             rmsnorm(kb_h[..., :B_DK], qk_norm_b[l, 1]), rmsnorm(kb_h[..., B_DK:], qk_norm_b[l, 1]),
                             heads(vb, B_HEADS), lam, slopes_b, diff_subln_g[l], lam_init)
        o_c = window_gqa_sink(rmsnorm(heads(qc, C_HEADS), qk_norm_c[l, 0]),
                              rmsnorm(heads(kc, C_KV_HEADS), qk_norm_c[l, 1]),
                              heads(vc, C_KV_HEADS), slopes_c, sink_logits[l])
        o_d = neighborhood_attention(rmsnorm(heads(qd, D_HEADS), qk_norm_d[l, 0]),
                                     rmsnorm(heads(kd, D_HEADS), qk_norm_d[l, 1]),
                                     heads(vd, D_HEADS), rpb[l])
        branches = (o_a, o_b, o_c, o_d)
        gates = gates.reshape(B, S, N_BRANCH, D_MODEL)
        merged = sum(jax.nn.sigmoid(gates[:, :, i]) * (merge_heads(branches[i]).astype(dt) @ w_branch[l, i])
                     for i in range(N_BRANCH))
        x = x + g1 * (merged @ w_out[l])
        h2 = rmsnorm(x, norm2_g[l]) * (1 + sc2) + sh2
        x = x + g2 * moe(h2, router_w, router_b, w1[l], w3[l], w2[l])
    return x
```

```python
import functools

import numpy as np
import jax
import jax.numpy as jnp
from jax import lax
from jax.experimental import pallas as pl
from jax.experimental.pallas import tpu as pltpu

F32 = jnp.float32
BF16 = jnp.bfloat16
LANES = 128
HALF = LANES // 2
RMS_EPS = 1e-6
NEG_INF = -1e30
VMEM_LIMIT = 56 * 1024 * 1024

GRID_W = 64
N_BRANCH = 4
A_HEADS = 4
A_PATTERNS = ((128, 1), (512, 4), (2048, 16))
B_HEADS = 4
C_HEADS = 8
C_KV_HEADS = 2
C_RADIUS = 128
D_HEADS = 4
D_WIN_R = 8
D_WIN_C = 16
N_EXPERTS = 16
N_GROUPS = 4
EXPERTS_PER_GROUP = N_EXPERTS // N_GROUPS

_NT = (((1,), (1,)), ((), ()))


def _params(*sem):
    return pltpu.CompilerParams(dimension_semantics=sem, vmem_limit_bytes=VMEM_LIMIT)


def _sigmoid(v):
    return 1.0 / (1.0 + jnp.exp(-v))


class _Layout:
    def __init__(self, d_model):
        bw = d_model // N_BRANCH
        self.bw = bw
        hb = bw // LANES
        o = {}
        o['qa'], o['ka'], o['va'] = 0, bw, 2 * bw
        o['qb'], o['kb'], o['vb'] = 3 * bw, 4 * bw, 5 * bw
        o['qc'] = 6 * bw
        o['kc'] = 7 * bw
        o['vc'] = 7 * bw + LANES
        o['qd'], o['kd'], o['vd'] = 7 * bw + 2 * LANES, 8 * bw + 2 * LANES, 9 * bw + 2 * LANES
        o['gates'] = 10 * bw + 2 * LANES
        self.orig = o
        gate_blocks = N_BRANCH * d_model // LANES
        self.GATE0 = 0
        b = gate_blocks
        self.QA0, self.KA0, self.VA0 = b, b + hb, b + 2 * hb
        b += 3 * hb
        self.QD0, self.KD0, self.VD0 = b, b + hb, b + 2 * hb
        b += 3 * hb
        self.QB0, self.KB0, self.VB0 = b, b + hb, b + 2 * hb
        b += 3 * hb
        self.QC0 = b
        self.KC0 = b + hb
        self.VC0 = b + hb + 2
        b += hb + 3
        self.n_real = b
        self.chunks_per_tile = 6
        self.n_blocks = -(-b // self.chunks_per_tile) * self.chunks_per_tile
        self.nz = self.n_blocks * LANES
        self.first_norm_tile = gate_blocks // self.chunks_per_tile

    def weight_segments(self):
        o, bw = self.orig, self.bw
        kc0, kc1 = o['kc'], o['kc'] + HALF
        segs = [(o['gates'], N_BRANCH * N_BRANCH * bw),
                (o['qa'], bw), (o['ka'], bw), (o['va'], bw),
                (o['qd'], bw), (o['kd'], bw), (o['vd'], bw),
                (o['qb'], bw), (o['kb'], bw), (o['vb'], bw),
                (o['qc'], bw),
                (kc0, HALF), (kc0, HALF), (kc1, HALF), (kc1, HALF),
                (o['vc'], LANES)]
        pad = self.nz - self.n_real * LANES
        if pad:
            segs.append((None, pad))
        return segs


def _build_wz(w_in, lay):
    parts = []
    for start, width in lay.weight_segments():
        if start is None:
            parts.append(jnp.zeros(w_in.shape[:-1] + (width,), w_in.dtype))
        else:
            parts.append(w_in[..., start:start + width])
    return jnp.concatenate(parts, axis=-1).astype(BF16)


def _build_coef(lay, qk_a, qk_b, qk_c, qk_d):
    hb = lay.bw // LANES
    a_hd = lay.bw // A_HEADS
    b_dk = lay.bw // (2 * B_HEADS)
    c_hd = lay.bw // C_HEADS
    d_hd = lay.bw // D_HEADS
    gain = jnp.ones((lay.nz,), F32)
    invd = np.ones((lay.nz,), np.float32)
    nflag = np.zeros((lay.nz,), np.float32)
    hflag = np.zeros((lay.nz,), np.float32)

    def put(gain, blk0, nblk, g, scale, group, half):
        lo, hi = blk0 * LANES, (blk0 + nblk) * LANES
        reps = (hi - lo) // g.shape[0]
        gain = gain.at[lo:hi].set(jnp.tile(g.astype(F32), reps) * scale)
        invd[lo:hi] = 1.0 / group
        nflag[lo:hi] = 1.0
        hflag[lo:hi] = 1.0 if half else 0.0
        return gain

    gain = put(gain, lay.QA0, hb, qk_a[0], a_hd ** -0.5, a_hd, False)
    gain = put(gain, lay.KA0, hb, qk_a[1], 1.0, a_hd, False)
    gain = put(gain, lay.QD0, hb, qk_d[0], d_hd ** -0.5, d_hd, False)
    gain = put(gain, lay.KD0, hb, qk_d[1], 1.0, d_hd, False)
    gain = put(gain, lay.QB0, hb, qk_b[0], b_dk ** -0.5, b_dk, True)
    gain = put(gain, lay.KB0, hb, qk_b[1], 1.0, b_dk, True)
    gain = put(gain, lay.QC0, hb, qk_c[0], c_hd ** -0.5, c_hd, True)
    gain = put(gain, lay.KC0, 2, qk_c[1], 1.0, c_hd, True)
    return jnp.stack([gain, jnp.asarray(invd), jnp.asarray(nflag), jnp.asarray(hflag)])


def _ada_kernel(c_ref, w_ref, b_ref, o_ref):
    c = c_ref[...]
    cond = (c * _sigmoid(c)).astype(BF16)
    o_ref[0] = jnp.dot(cond, w_ref[0].astype(BF16), preferred_element_type=F32) + b_ref[0]


def _ada_mod(c, ada_w, ada_b, tn=1024):
    depth, d, n = ada_w.shape
    rows = 8
    cp = jnp.zeros((rows, d), F32).at[:c.shape[0]].set(c)
    tn = min(tn, n)
    return pl.pallas_call(
        _ada_kernel,
        out_shape=jax.ShapeDtypeStruct((depth, rows, n), F32),
        grid=(depth, n // tn),
        in_specs=[pl.BlockSpec((rows, d), lambda l, j: (0, 0)),
                  pl.BlockSpec((1, d, tn), lambda l, j: (l, 0, j)),
                  pl.BlockSpec((1, 1, tn), lambda l, j: (l, 0, j))],
        out_specs=pl.BlockSpec((1, rows, tn), lambda l, j: (l, 0, j)),
        compiler_params=_params("parallel", "parallel"),
        name="ada_mod",
    )(cp, ada_w, ada_b.reshape(depth, 1, n))


def _mod_norm(x, g, shift, scale):
    y = x * lax.rsqrt(jnp.mean(x * x, axis=-1, keepdims=True) + RMS_EPS) * g
    return y * (1.0 + scale) + shift


def _inproj_kernel(x_ref, mod_ref, g_ref, w_ref, coef_ref, z_ref, h_scr, *, first_norm_tile, n_chunks):
    j = pl.program_id(1)

    @pl.when(j == 0)
    def _():
        h = _mod_norm(x_ref[...], g_ref[...], mod_ref[0, 0:1, :], mod_ref[0, 1:2, :])
        h_scr[...] = h.astype(BF16)

    acc = jnp.dot(h_scr[...], w_ref[0], preferred_element_type=F32)

    @pl.when(j < first_norm_tile)
    def _():
        z_ref[...] = acc.astype(BF16)

    @pl.when(j >= first_norm_tile)
    def _():
        lo_lane = lax.broadcasted_iota(jnp.int32, (1, LANES), 1) < HALF
        for c in range(n_chunks):
            sl = slice(c * LANES, (c + 1) * LANES)
            y = acc[:, sl]
            gain = coef_ref[0, 0:1, sl]
            invd = coef_ref[0, 1:2, sl]
            nflag = coef_ref[0, 2:3, sl]
            hflag = coef_ref[0, 3:4, sl]
            y2 = y * y
            lo = jnp.sum(jnp.where(lo_lane, y2, 0.0), axis=-1, keepdims=True)
            hi = jnp.sum(jnp.where(lo_lane, 0.0, y2), axis=-1, keepdims=True)
            ss = jnp.where(hflag > 0.0, jnp.where(lo_lane, lo, hi), lo + hi)
            r = lax.rsqrt(ss * invd + RMS_EPS) * gain
            z_ref[:, sl] = (y * jnp.where(nflag > 0.0, r, 1.0)).astype(BF16)


def _inproj(x2, mod, g, wz, coef, layer, lay, seq, tm=1024):
    t, d = x2.shape
    tm = min(tm, seq)
    tn = lay.chunks_per_tile * LANES
    per_batch = seq // tm
    kern = functools.partial(_inproj_kernel, first_norm_tile=lay.first_norm_tile,
                             n_chunks=lay.chunks_per_tile)
    return pl.pallas_call(
        kern,
        out_shape=jax.ShapeDtypeStruct((t, lay.nz), BF16),
        grid=(t // tm, lay.nz // tn),
        in_specs=[pl.BlockSpec((tm, d), lambda i, j: (i, 0)),
                  pl.BlockSpec((1, 6, d), lambda i, j: (i // per_batch, 0, 0)),
                  pl.BlockSpec((1, d), lambda i, j: (0, 0)),
                  pl.BlockSpec((1, d, tn), lambda i, j: (layer, 0, j)),
                  pl.BlockSpec((1, 4, tn), lambda i, j: (layer, 0, j))],
        out_specs=pl.BlockSpec((tm, tn), lambda i, j: (i, j)),
        scratch_shapes=[pltpu.VMEM((tm, d), BF16)],
        compiler_params=_params("parallel", "arbitrary"),
        name="inproj",
    )(x2, mod, g, wz, coef)


def _band_kernel(sink_ref, q_ref, k_ref, v_ref, tab_ref, qmask_ref, o_ref, *, n_side, ck, nkc):
    h = pl.program_id(1)
    i = pl.program_id(2)
    q = q_ref[...] * qmask_ref[0]
    sink = sink_ref[h]
    scores = []
    starts = []
    m = jnp.full((q.shape[0], 1), sink, F32)
    for d in range(-n_side, n_side + 1):
        kc = i + d
        valid = jnp.logical_and(kc >= 0, kc < nkc)
        start = pl.multiple_of(jnp.clip(kc, 0, nkc - 1) * ck, ck)
        kb = k_ref[pl.ds(start, ck), :]
        s = lax.dot_general(q, kb, _NT, preferred_element_type=F32)
        s = s + tab_ref[0, d + n_side] + jnp.where(valid, 0.0, NEG_INF)
        m = jnp.maximum(m, jnp.max(s, axis=-1, keepdims=True))
        scores.append(s)
        starts.append(start)
    den = jnp.exp(sink - m)
    acc = jnp.zeros((q.shape[0], LANES), F32)
    for s, start in zip(scores, starts):
        p = jnp.exp(s - m)
        den = den + jnp.sum(p, axis=-1, keepdims=True)
        acc = acc + jnp.dot(p.astype(BF16), v_ref[pl.ds(start, ck), :], preferred_element_type=F32)
    o_ref[...] = (acc / den).astype(BF16)


def _band_attention(z, table, qmask, sink, *, batch, seq, n_heads, qblk, kblk, vblk, n_side):
    t = z.shape[0]
    tq = table.shape[-1]
    nq = seq // tq
    kern = functools.partial(_band_kernel, n_side=n_side, ck=tq, nkc=nq)
    nd = 2 * n_side + 1
    return pl.pallas_call(
        kern,
        out_shape=jax.ShapeDtypeStruct((t, n_heads * LANES), BF16),
        grid=(batch, n_heads, nq),
        in_specs=[pl.BlockSpec(memory_space=pltpu.SMEM),
                  pl.BlockSpec((tq, LANES), lambda b, h, i: (b * nq + i, qblk(h))),
                  pl.BlockSpec((seq, LANES), lambda b, h, i: (b, kblk(h))),
                  pl.BlockSpec((seq, LANES), lambda b, h, i: (b, vblk(h))),
                  pl.BlockSpec((1, nd, tq, tq), lambda b, h, i: (h, 0, 0, 0)),
                  pl.BlockSpec((1, 1, LANES), lambda b, h, i: (h, 0, 0))],
        out_specs=pl.BlockSpec((tq, LANES), lambda b, h, i: (b * nq + i, h)),
        compiler_params=_params("parallel", "parallel", "arbitrary"),
        name="band_attention",
    )(sink, z, z, z, table, qmask)


def _alibi_slopes(n):
    return np.array([2.0 ** (-8.0 * (i + 1) / n) for i in range(n)], dtype=np.float32)


def _dilated_table(tq):
    reach = max(w // 2 for w, _ in A_PATTERNS)
    n_side = -(-reach // tq)
    d = np.arange(-n_side, n_side + 1)[:, None, None] * tq
    off = d + np.arange(tq)[None, None, :] - np.arange(tq)[None, :, None]
    mult = np.zeros(off.shape, np.float64)
    for window, dil in A_PATTERNS:
        mult += ((off % dil) == 0) & (np.abs(off) <= window // 2)
    logm = np.where(mult > 0, np.log(np.maximum(mult, 1.0)), 0.0)
    slopes = _alibi_slopes(A_HEADS).astype(np.float64)
    tab = -slopes[:, None, None, None] * np.abs(off)[None] + logm[None]
    tab = np.where(mult[None] > 0, tab, NEG_INF)
    return jnp.asarray(tab.astype(np.float32)), n_side


def _window_table(tq):
    n_side = -(-C_RADIUS // tq)
    d = np.arange(-n_side, n_side + 1)[:, None, None] * tq
    off = d + np.arange(tq)[None, None, :] - np.arange(tq)[None, :, None]
    slopes = _alibi_slopes(C_HEADS).astype(np.float64)
    tab = -slopes[:, None, None, None] * np.abs(off)[None]
    tab = np.where((np.abs(off) <= C_RADIUS)[None], tab, NEG_INF)
    return jnp.asarray(tab.astype(np.float32)), n_side


_D_QROWS = 8
_D_KROWS = 16


def _nbr_kernel(q_ref, k_ref, v_ref, tab_ref, o_ref, *, rows):
    t = pl.program_id(2)
    krow = jnp.clip(t * _D_QROWS - D_WIN_R // 2, 0, rows - _D_KROWS)
    start = pl.multiple_of(krow * GRID_W, GRID_W * (D_WIN_R // 2))
    nk = _D_KROWS * GRID_W
    s = lax.dot_general(q_ref[...], k_ref[pl.ds(start, nk), :], _NT, preferred_element_type=F32)
    s = s + tab_ref[0, 0]
    m = jnp.max(s, axis=-1, keepdims=True)
    p = jnp.exp(s - m)
    den = jnp.sum(p, axis=-1, keepdims=True)
    acc = jnp.dot(p.astype(BF16), v_ref[pl.ds(start, nk), :], preferred_element_type=F32)
    o_ref[...] = (acc / den).astype(BF16)


def _nbr_geometry(rows):
    nt = rows // _D_QROWS
    wr = min(D_WIN_R, rows)
    out = []
    for t in (0, 1, nt - 1):
        krow0 = int(np.clip(t * _D_QROWS - D_WIN_R // 2, 0, rows - _D_KROWS))
        r = t * _D_QROWS + np.repeat(np.arange(_D_QROWS), GRID_W)
        c = np.tile(np.arange(GRID_W), _D_QROWS)
        kr = krow0 + np.repeat(np.arange(_D_KROWS), GRID_W)
        kc = np.tile(np.arange(GRID_W), _D_KROWS)
        rs = np.clip(r - wr // 2, 0, rows - wr)
        row_ok = (kr[None, :] >= rs[:, None]) & (kr[None, :] < rs[:, None] + wr)
        cs = np.clip(c - D_WIN_C // 2, 0, GRID_W - D_WIN_C)
        col_ok = (kc[None, :] >= cs[:, None]) & (kc[None, :] < cs[:, None] + D_WIN_C)
        dr = np.clip(kr[None, :] - r[:, None] + (D_WIN_R - 1), 0, 2 * D_WIN_R - 2)
        dc = np.clip(kc[None, :] - c[:, None] + (D_WIN_C - 1), 0, 2 * D_WIN_C - 2)
        out.append((dr, dc, row_ok & col_ok))
    dr, dc, ok = (np.stack(a) for a in zip(*out))
    return dr.astype(np.int32), dc.astype(np.int32), ok


def _nbr_attention(z, rpb_l, geom, *, batch, seq, lay):
    t = z.shape[0]
    rows = seq // GRID_W
    nt = rows // _D_QROWS
    tq = _D_QROWS * GRID_W
    dr, dc, ok = geom
    table = jnp.where(ok[None], rpb_l[:, dr, dc].astype(F32), NEG_INF)
    nk = _D_KROWS * GRID_W
    kern = functools.partial(_nbr_kernel, rows=rows)
    kind = lambda i: jnp.where(i == 0, 0, jnp.where(i == nt - 1, 2, 1))
    return pl.pallas_call(
        kern,
        out_shape=jax.ShapeDtypeStruct((t, D_HEADS * LANES), BF16),
        grid=(batch, D_HEADS, nt),
        in_specs=[pl.BlockSpec((tq, LANES), lambda b, h, i: (b * nt + i, lay.QD0 + h)),
                  pl.BlockSpec((seq, LANES), lambda b, h, i: (b, lay.KD0 + h)),
                  pl.BlockSpec((seq, LANES), lambda b, h, i: (b, lay.VD0 + h)),
                  pl.BlockSpec((1, 1, tq, nk), lambda b, h, i: (h, kind(i), 0, 0))],
        out_specs=pl.BlockSpec((tq, LANES), lambda b, h, i: (b * nt + i, h)),
        compiler_params=_params("parallel", "parallel", "arbitrary"),
        name="nbr_attention",
    )(z, z, z, table)


def _diff_kernel(slope_ref, misc_ref, q_ref, k_ref, v_ref, dl_ref, subg_ref, o_ref,
                 m_scr, l_scr, acc_scr, *, tq, tk, nk):
    h = pl.program_id(1)
    i = pl.program_id(2)
    slope = slope_ref[h]
    lam_init = misc_ref[0]
    lo_lane = lax.broadcasted_iota(jnp.int32, (1, LANES), 1) < HALF
    q = q_ref[...]
    zero = jnp.zeros_like(q)
    qs = (jnp.where(lo_lane, q, zero), jnp.where(lo_lane, zero, q))
    m_scr[...] = jnp.full(m_scr.shape, NEG_INF, F32)
    l_scr[...] = jnp.zeros(l_scr.shape, F32)
    acc_scr[...] = jnp.zeros(acc_scr.shape, F32)
    dji = (lax.broadcasted_iota(jnp.int32, (tq, tk), 1) - lax.broadcasted_iota(jnp.int32, (tq, tk), 0))

    def body(c, carry):
        start = pl.multiple_of(c * tk, tk)
        kb = k_ref[pl.ds(start, tk), :]
        vb = v_ref[pl.ds(start, tk), :]
        bias = -slope * jnp.abs(dji + (c * tk - i * tq)).astype(F32)
        for idx in range(2):
            s = lax.dot_general(qs[idx], kb, _NT, preferred_element_type=F32) + bias
            m_old = m_scr[idx]
            m_new = jnp.maximum(m_old, jnp.max(s, axis=-1, keepdims=True))
            alpha = jnp.exp(m_old - m_new)
            p = jnp.exp(s - m_new)
            l_scr[idx] = alpha * l_scr[idx] + jnp.sum(p, axis=-1, keepdims=True)
            acc_scr[idx] = alpha * acc_scr[idx] + jnp.dot(p.astype(BF16), vb, preferred_element_type=F32)
            m_scr[idx] = m_new
        return carry

    lax.fori_loop(0, nk, body, 0)

    hs = pl.ds(h, 1)
    lam = (jnp.exp(jnp.sum(dl_ref[0, hs, :] * dl_ref[1, hs, :], axis=-1, keepdims=True))
           - jnp.exp(jnp.sum(dl_ref[2, hs, :] * dl_ref[3, hs, :], axis=-1, keepdims=True)) + lam_init)
    o = acc_scr[0] / l_scr[0] - lam * (acc_scr[1] / l_scr[1])
    o = o * lax.rsqrt(jnp.mean(o * o, axis=-1, keepdims=True) + RMS_EPS) * subg_ref[...]
    o_ref[...] = (o * (1.0 - lam_init)).astype(BF16)


def _diff_attention(z, dl, subg, slopes, misc, *, batch, seq, lay, tq=256, tk=512):
    t = z.shape[0]
    tq = min(tq, seq)
    tk = min(tk, seq)
    nq = seq // tq
    kern = functools.partial(_diff_kernel, tq=tq, tk=tk, nk=seq // tk)
    return pl.pallas_call(
        kern,
        out_shape=jax.ShapeDtypeStruct((t, B_HEADS * LANES), BF16),
        grid=(batch, B_HEADS, nq),
        in_specs=[pl.BlockSpec(memory_space=pltpu.SMEM),
                  pl.BlockSpec(memory_space=pltpu.SMEM),
                  pl.BlockSpec((tq, LANES), lambda b, h, i: (b * nq + i, lay.QB0 + h)),
                  pl.BlockSpec((seq, LANES), lambda b, h, i: (b, lay.KB0 + h)),
                  pl.BlockSpec((seq, LANES), lambda b, h, i: (b, lay.VB0 + h)),
                  pl.BlockSpec(dl.shape, lambda b, h, i: (0, 0, 0)),
                  pl.BlockSpec((1, LANES), lambda b, h, i: (0, 0))],
        out_specs=pl.BlockSpec((tq, LANES), lambda b, h, i: (b * nq + i, h)),
        scratch_shapes=[pltpu.VMEM((2, tq, 1), F32), pltpu.VMEM((2, tq, 1), F32),
                        pltpu.VMEM((2, tq, LANES), F32)],
        compiler_params=_params("parallel", "parallel", "arbitrary"),
        name="diff_attention",
    )(slopes, misc, z, z, z, dl, subg)


def _merge_kernel(oa_ref, ob_ref, oc_ref, od_ref, gate_ref, wa_ref, wb_ref, wc_ref, wd_ref, o_ref, *, d):
    acc = None
    for idx, (o, w) in enumerate(((oa_ref, wa_ref), (ob_ref, wb_ref), (oc_ref, wc_ref), (od_ref, wd_ref))):
        y = jnp.dot(o[...], w[...], preferred_element_type=F32)
        g = _sigmoid(gate_ref[:, idx * d:(idx + 1) * d].astype(F32))
        acc = g * y if acc is None else acc + g * y
    o_ref[...] = acc.astype(BF16)


def _merge(oa, ob, oc, od, z, wbs, *, d, tm=256):
    t = oa.shape[0]
    tm = min(tm, t)
    row = lambda a: pl.BlockSpec((tm, a.shape[1]), lambda i: (i, 0))
    full = lambda a: pl.BlockSpec(a.shape, lambda i: (0, 0))
    return pl.pallas_call(
        functools.partial(_merge_kernel, d=d),
        out_shape=jax.ShapeDtypeStruct((t, d), BF16),
        grid=(t // tm,),
        in_specs=[row(oa), row(ob), row(oc), row(od),
                  pl.BlockSpec((tm, N_BRANCH * d), lambda i: (i, 0))] + [full(w) for w in wbs],
        out_specs=pl.BlockSpec((tm, d), lambda i: (i, 0)),
        compiler_params=_params("parallel"),
        name="branch_merge",
    )(oa, ob, oc, od, z, *wbs)


def _outproj_kernel(a_ref, w_ref, x_ref, g_ref, o_ref):
    y = jnp.dot(a_ref[...], w_ref[0], preferred_element_type=F32)
    o_ref[...] = x_ref[...] + g_ref[0] * y


def _outproj(a, w, layer, x2, gate, seq, tm=512):
    t, d = x2.shape
    tm = min(tm, seq)
    per_batch = seq // tm
    return pl.pallas_call(
        _outproj_kernel,
        out_shape=jax.ShapeDtypeStruct((t, d), F32),
        grid=(t // tm,),
        in_specs=[pl.BlockSpec((tm, a.shape[1]), lambda i: (i, 0)),
                  pl.BlockSpec((1,) + w.shape[1:], lambda i: (layer, 0, 0)),
                  pl.BlockSpec((tm, d), lambda i: (i, 0)),
                  pl.BlockSpec((1, 1, d), lambda i: (i // per_batch, 0, 0))],
        out_specs=pl.BlockSpec((tm, d), lambda i: (i, 0)),
        compiler_params=_params("parallel"),
        name="out_proj",
    )(a, w, x2, gate)


def _router_kernel(x_ref, mod_ref, g_ref, rwt_ref, rb_ref, h_ref, e_ref, gw_ref):
    h = _mod_norm(x_ref[...], g_ref[...], mod_ref[0, 3:4, :], mod_ref[0, 4:5, :]).astype(BF16)
    h_ref[...] = h
    logits = lax.dot_general(rwt_ref[...], h, _NT, preferred_element_type=F32)
    score = _sigmoid(logits)
    sel = score + rb_ref[...]
    sel_r = [sel[e:e + 1, :] for e in range(N_EXPERTS)]
    sc_r = [score[e:e + 1, :] for e in range(N_EXPERTS)]
    n = EXPERTS_PER_GROUP

    def top2_sum(v):
        best = None
        for a in range(n):
            for b in range(a + 1, n):
                s = v[a] + v[b]
                best = s if best is None else jnp.maximum(best, s)
        return best

    gbest = top2_sum(sel_r[0:n])
    gidx = jnp.zeros(gbest.shape, jnp.int32)
    cv = list(sel_r[0:n])
    cs = list(sc_r[0:n])
    for g in range(1, N_GROUPS):
        gs = top2_sum(sel_r[g * n:(g + 1) * n])
        upd = gs > gbest
        gbest = jnp.where(upd, gs, gbest)
        gidx = jnp.where(upd, g, gidx)
        for j in range(n):
            cv[j] = jnp.where(upd, sel_r[g * n + j], cv[j])
            cs[j] = jnp.where(upd, sc_r[g * n + j], cs[j])
    b1, i1, s1 = cv[0], jnp.zeros(gbest.shape, jnp.int32), cs[0]
    for j in range(1, n):
        upd = cv[j] > b1
        b1 = jnp.where(upd, cv[j], b1)
        i1 = jnp.where(upd, j, i1)
        s1 = jnp.where(upd, cs[j], s1)
    low = -3.0e38
    b2 = jnp.full(gbest.shape, low, F32)
    i2 = jnp.zeros(gbest.shape, jnp.int32)
    s2 = jnp.zeros(gbest.shape, F32)
    for j in range(n):
        cand = jnp.where(i1 == j, low, cv[j])
        upd = cand > b2
        b2 = jnp.where(upd, cand, b2)
        i2 = jnp.where(upd, j, i2)
        s2 = jnp.where(upd, cs[j], s2)
    tot = s1 + s2
    e_ref[0:1, :] = gidx * n + i1
    e_ref[1:2, :] = gidx * n + i2
    gw_ref[0:1, :] = s1 / tot
    gw_ref[1:2, :] = s2 / tot


def _router(x2, mod, g, rwt, rb, seq, tm=512):
    t, d = x2.shape
    tm = min(tm, seq)
    per_batch = seq // tm
    return pl.pallas_call(
        _router_kernel,
        out_shape=(jax.ShapeDtypeStruct((t, d), BF16),
                   jax.ShapeDtypeStruct((2, t), jnp.int32),
                   jax.ShapeDtypeStruct((2, t), F32)),
        grid=(t // tm,),
        in_specs=[pl.BlockSpec((tm, d), lambda i: (i, 0)),
                  pl.BlockSpec((1, 6, d), lambda i: (i // per_batch, 0, 0)),
                  pl.BlockSpec((1, d), lambda i: (0, 0)),
                  pl.BlockSpec(rwt.shape, lambda i: (0, 0)),
                  pl.BlockSpec(rb.shape, lambda i: (0, 0))],
        out_specs=(pl.BlockSpec((tm, d), lambda i: (i, 0)),
                   pl.BlockSpec((2, tm), lambda i: (0, i)),
                   pl.BlockSpec((2, tm), lambda i: (0, i))),
        compiler_params=_params("parallel"),
        name="norm_router",
    )(x2, mod, g, rwt, rb)


def _rank_kernel(e_ref, rank_ref, cnt_ref, carry):
    @pl.when(pl.program_id(0) == 0)
    def _():
        carry[...] = jnp.zeros(carry.shape, F32)

    tm = e_ref.shape[1]
    eio = lax.broadcasted_iota(jnp.int32, (N_EXPERTS, tm), 0)
    oh0 = eio == e_ref[0:1, :]
    oh1 = eio == e_ref[1:2, :]
    c = jnp.where(jnp.logical_or(oh0, oh1), 1.0, 0.0)
    before = (lax.broadcasted_iota(jnp.int32, (tm, tm), 0) < lax.broadcasted_iota(jnp.int32, (tm, tm), 1))
    pref = jnp.dot(c.astype(BF16), jnp.where(before, 1.0, 0.0).astype(BF16),
                   preferred_element_type=F32) + carry[...]
    rank_ref[0:1, :] = jnp.sum(jnp.where(oh0, pref, 0.0), axis=0, keepdims=True).astype(jnp.int32)
    rank_ref[1:2, :] = jnp.sum(jnp.where(oh1, pref, 0.0), axis=0, keepdims=True).astype(jnp.int32)
    total = carry[...] + jnp.sum(c, axis=1, keepdims=True)
    carry[...] = total
    cnt_ref[...] = jnp.broadcast_to(total, cnt_ref.shape)


def _rank(eidx, tm=512):
    t = eidx.shape[1]
    tm = min(tm, t)
    return pl.pallas_call(
        _rank_kernel,
        out_shape=(jax.ShapeDtypeStruct((2, t), jnp.int32),
                   jax.ShapeDtypeStruct((N_EXPERTS, LANES), F32)),
        grid=(t // tm,),
        in_specs=[pl.BlockSpec((2, tm), lambda i: (0, i))],
        out_specs=(pl.BlockSpec((2, tm), lambda i: (0, i)),
                   pl.BlockSpec((N_EXPERTS, LANES), lambda i: (0, 0))),
        scratch_shapes=[pltpu.VMEM((N_EXPERTS, 1), F32)],
        compiler_params=_params("arbitrary"),
        name="expert_rank",
    )(eidx)


def _gather_kernel(src_ref, h_hbm, o_ref, sem, *, rows):
    base = pl.program_id(0) * rows

    def issue(r, carry):
        pltpu.make_async_copy(h_hbm.at[src_ref[base + r]], o_ref.at[r], sem).start()
        return carry

    lax.fori_loop(0, rows, issue, 0)

    def drain(r, carry):
        pltpu.make_async_copy(h_hbm.at[0], o_ref.at[r], sem).wait()
        return carry

    lax.fori_loop(0, rows, drain, 0)


def _gather_rows(src, h3, rows=512):
    p = src.shape[0]
    rows = min(rows, p)
    sub = h3.shape[1]
    return pl.pallas_call(
        functools.partial(_gather_kernel, rows=rows),
        out_shape=jax.ShapeDtypeStruct((p, sub, LANES), h3.dtype),
        grid_spec=pltpu.PrefetchScalarGridSpec(
            num_scalar_prefetch=1,
            grid=(p // rows,),
            in_specs=[pl.BlockSpec(memory_space=pl.ANY)],
            out_specs=pl.BlockSpec((rows, sub, LANES), lambda i, src: (i, 0, 0)),
            scratch_shapes=[pltpu.SemaphoreType.DMA]),
        compiler_params=_params("arbitrary"),
        name="dispatch_gather",
    )(src, h3)


def _expert_kernel(te_ref, nu_ref, xs_ref, gw_ref, w1_ref, w3_ref, w2_ref, y_ref):
    j = pl.program_id(0)

    @pl.when(j < nu_ref[0])
    def _():
        xs = xs_ref[...]
        g = jnp.dot(xs, w1_ref[0], preferred_element_type=F32)
        u = jnp.dot(xs, w3_ref[0], preferred_element_type=F32)
        mid = (g * _sigmoid(g) * u).astype(BF16)
        y = jnp.dot(mid, w2_ref[0], preferred_element_type=F32)
        y_ref[...] = y * gw_ref[...]

    @pl.when(j >= nu_ref[0])
    def _():
        y_ref[...] = jnp.zeros(y_ref.shape, F32)


def _experts(tile_e, n_used, xs, gws, w1, w3, w2, layer, tm):
    p, d = xs.shape
    f = w1.shape[-1]
    n_exp = w1.shape[1]
    w1r = w1.reshape((-1,) + w1.shape[2:])
    w3r = w3.reshape((-1,) + w3.shape[2:])
    w2r = w2.reshape((-1,) + w2.shape[2:])
    wmap = lambda j, te, nu: (layer * n_exp + te[j], 0, 0)
    return pl.pallas_call(
        _expert_kernel,
        out_shape=jax.ShapeDtypeStruct((p, d), F32),
        grid_spec=pltpu.PrefetchScalarGridSpec(
            num_scalar_prefetch=2,
            grid=(p // tm,),
            in_specs=[pl.BlockSpec((tm, d), lambda j, te, nu: (j, 0)),
                      pl.BlockSpec((tm, 1), lambda j, te, nu: (j, 0)),
                      pl.BlockSpec((1, d, f), wmap),
                      pl.BlockSpec((1, d, f), wmap),
                      pl.BlockSpec((1, f, d), wmap)],
            out_specs=pl.BlockSpec((tm, d), lambda j, te, nu: (j, 0))),
        compiler_params=_params("arbitrary"),
        name="experts",
    )(tile_e, n_used, xs, gws, w1r, w3r, w2r)


def _combine_kernel(pos_ref, x_ref, g_ref, y_hbm, o_ref, buf, sem, *, tm, t):
    base = pl.program_id(0) * tm

    def issue(r, carry):
        pltpu.make_async_copy(y_hbm.at[pos_ref[base + r]], buf.at[0, r], sem).start()
        pltpu.make_async_copy(y_hbm.at[pos_ref[t + base + r]], buf.at[1, r], sem).start()
        return carry

    lax.fori_loop(0, tm, issue, 0)

    def drain(r, carry):
        pltpu.make_async_copy(y_hbm.at[0], buf.at[0, r], sem).wait()
        pltpu.make_async_copy(y_hbm.at[0], buf.at[1, r], sem).wait()
        return carry

    lax.fori_loop(0, tm, drain, 0)
    o_ref[...] = x_ref[...] + g_ref[...] * (buf[0] + buf[1])


def _combine(pos, x3, g3, y3, seq, tm=256):
    t, sub, _ = x3.shape
    tm = min(tm, seq)
    per_batch = seq // tm
    return pl.pallas_call(
        functools.partial(_combine_kernel, tm=tm, t=t),
        out_shape=jax.ShapeDtypeStruct(x3.shape, F32),
        grid_spec=pltpu.PrefetchScalarGridSpec(
            num_scalar_prefetch=1,
            grid=(t // tm,),
            in_specs=[pl.BlockSpec((tm, sub, LANES), lambda i, pos: (i, 0, 0)),
                      pl.BlockSpec((1, sub, LANES), lambda i, pos: (i // per_batch, 0, 0)),
                      pl.BlockSpec(memory_space=pl.ANY)],
            out_specs=pl.BlockSpec((tm, sub, LANES), lambda i, pos: (i, 0, 0)),
            scratch_shapes=[pltpu.VMEM((2, tm, sub, LANES), F32), pltpu.SemaphoreType.DMA]),
        compiler_params=_params("arbitrary"),
        name="moe_combine",
    )(pos, x3, g3, y3)


def _moe(x2, mod, g2n, rwt, rb, w1, w3, w2, layer, *, batch, seq, tm_e=512):
    t, d = x2.shape
    sub = d // LANES
    h, eidx, gw = _router(x2, mod, g2n, rwt, rb, seq)
    rank, cnt = _rank(eidx)
    tm_e = min(tm_e, t)
    counts = cnt[:, 0].astype(jnp.int32)
    gsz = ((counts + tm_e - 1) // tm_e) * tm_e
    ends = jnp.cumsum(gsz)
    offs = ends - gsz
    pos = offs[eidx] + rank
    n_tiles = (2 * t) // tm_e + N_EXPERTS
    p = n_tiles * tm_e
    flat = pos.reshape(-1)
    tok = jnp.broadcast_to(jnp.arange(t, dtype=jnp.int32)[None, :], (2, t)).reshape(-1)
    src = jnp.zeros((p,), jnp.int32).at[flat].set(tok)
    gws = jnp.zeros((p,), F32).at[flat].set(gw.reshape(-1)).reshape(p, 1)
    tile_e = jnp.minimum(jnp.searchsorted(ends, jnp.arange(n_tiles, dtype=jnp.int32) * tm_e, side='right'),
                         N_EXPERTS - 1).astype(jnp.int32)
    n_used = (ends[-1:] // tm_e).astype(jnp.int32)
    xs = _gather_rows(src, h.reshape(t, sub, LANES))
    y = _experts(tile_e, n_used, xs.reshape(p, d), gws, w1, w3, w2, layer, tm_e)
    g3 = mod[:, 5, :].reshape(batch, sub, LANES)
    out = _combine(flat, x2.reshape(t, sub, LANES), g3, y.reshape(p, sub, LANES), seq)
    return out.reshape(t, d)


def kernel(x, c, ada_w, ada_b, norm1_g, norm2_g, w_in, qk_norm_a, qk_norm_b, qk_norm_c, qk_norm_d,
           diff_lambda, diff_subln_g, sink_logits, rpb, w_branch, w_out, router_w, router_b, w1, w3, w2):
    batch, seq, d = x.shape
    depth = ada_w.shape[0]
    t = batch * seq
    lay = _Layout(d)
    bw = lay.bw

    mod_all = _ada_mod(c, ada_w, ada_b)
    wz = _build_wz(w_in, lay)
    coef = jnp.stack([_build_coef(lay, qk_norm_a[l], qk_norm_b[l], qk_norm_c[l], qk_norm_d[l])
                      for l in range(depth)])
    wb16 = w_branch.astype(BF16)
    c_hd = bw // C_HEADS
    grp = C_HEADS // C_KV_HEADS
    wc_rows = []
    for hq in range(C_HEADS):
        blk = jnp.zeros((depth, LANES, d), BF16)
        off = (hq // grp) * c_hd
        wc_rows.append(blk.at[:, off:off + c_hd].set(wb16[:, 2, hq * c_hd:(hq + 1) * c_hd]))
    wc_ext = jnp.concatenate(wc_rows, axis=1)
    wout16 = w_out.astype(BF16)
    w1b, w3b, w2b = w1.astype(BF16), w3.astype(BF16), w2.astype(BF16)
    rwt = router_w.T.astype(BF16)
    rb = router_b.reshape(N_EXPERTS, 1).astype(F32)

    tq = 256
    tab_a, side_a = _dilated_table(tq)
    tab_c, side_c = _window_table(tq)
    ones_mask = jnp.ones((A_HEADS, 1, LANES), BF16)
    lane = np.arange(LANES)
    cmask = np.stack([(lane // HALF) == (hq % 2) for hq in range(C_HEADS)]).astype(np.float32)
    cmask = jnp.asarray(cmask.reshape(C_HEADS, 1, LANES), BF16)
    no_sink = jnp.full((A_HEADS,), NEG_INF, F32)
    slopes_b = jnp.asarray(_alibi_slopes(B_HEADS))
    geom = _nbr_geometry(seq // GRID_W)

    x2 = x.reshape(t, d)
    for l in range(depth):
        mod = mod_all[l, :batch].reshape(batch, 6, d)
        z = _inproj(x2, mod, norm1_g[l].reshape(1, d), wz, coef, l, lay, seq)
        o_a = _band_attention(z, tab_a, ones_mask, no_sink, batch=batch, seq=seq, n_heads=A_HEADS,
                              qblk=lambda h: lay.QA0 + h, kblk=lambda h: lay.KA0 + h,
                              vblk=lambda h: lay.VA0 + h, n_side=side_a)
        lam_init = 0.8 - 0.6 * float(np.exp(-0.3 * l))
        o_b = _diff_attention(z, diff_lambda[l], diff_subln_g[l].reshape(1, LANES), slopes_b,
                              jnp.full((1,), lam_init, F32), batch=batch, seq=seq, lay=lay)
        o_c = _band_attention(z, tab_c, cmask, sink_logits[l].astype(F32), batch=batch, seq=seq,
                              n_heads=C_HEADS, qblk=lambda h: lay.QC0 + h // 2,
                              kblk=lambda h: lay.KC0 + h // grp, vblk=lambda h: lay.VC0,
                              n_side=side_c)
        o_d = _nbr_attention(z, rpb[l], geom, batch=batch, seq=seq, lay=lay)
        merged = _merge(o_a, o_b, o_c, o_d, z, (wb16[l, 0], wb16[l, 1], wc_ext[l], wb16[l, 3]), d=d)
        x2 = _outproj(merged, wout16, l, x2, mod[:, 2:3, :], seq)
        x2 = _moe(x2, mod, norm2_g[l].reshape(1, d), rwt, rb, w1b, w3b, w2b, l, batch=batch, seq=seq)
    return x2.reshape(batch, seq, d)
```

```python
import functools

import numpy as np
import jax
import jax.numpy as jnp
from jax import lax
from jax.experimental import pallas as pl
from jax.experimental.pallas import tpu as pltpu

F32 = jnp.float32
BF16 = jnp.bfloat16
LANES = 128
HALF = LANES // 2
RMS_EPS = 1e-6
NEG_INF = -1e30
VMEM_LIMIT = 56 * 1024 * 1024

GRID_W = 64
N_BRANCH = 4
A_HEADS = 4
A_PATTERNS = ((128, 1), (512, 4), (2048, 16))
B_HEADS = 4
C_HEADS = 8
C_KV_HEADS = 2
C_RADIUS = 128
D_HEADS = 4
D_WIN_R = 8
D_WIN_C = 16
N_EXPERTS = 16
N_GROUPS = 4
EXPERTS_PER_GROUP = N_EXPERTS // N_GROUPS

_NT = (((1,), (1,)), ((), ()))


def _params(*sem):
    return pltpu.CompilerParams(dimension_semantics=sem, vmem_limit_bytes=VMEM_LIMIT)


def _sigmoid(v):
    return 1.0 / (1.0 + jnp.exp(-v))


class _Layout:
    def __init__(self, d_model):
        bw = d_model // N_BRANCH
        self.bw = bw
        hb = bw // LANES
        o = {}
        o['qa'], o['ka'], o['va'] = 0, bw, 2 * bw
        o['qb'], o['kb'], o['vb'] = 3 * bw, 4 * bw, 5 * bw
        o['qc'] = 6 * bw
        o['kc'] = 7 * bw
        o['vc'] = 7 * bw + LANES
        o['qd'], o['kd'], o['vd'] = 7 * bw + 2 * LANES, 8 * bw + 2 * LANES, 9 * bw + 2 * LANES
        o['gates'] = 10 * bw + 2 * LANES
        self.orig = o
        gate_blocks = N_BRANCH * d_model // LANES
        self.GATE0 = 0
        b = gate_blocks
        self.QA0, self.KA0, self.VA0 = b, b + hb, b + 2 * hb
        b += 3 * hb
        self.QD0, self.KD0, self.VD0 = b, b + hb, b + 2 * hb
        b += 3 * hb
        self.QB0, self.KB0, self.VB0 = b, b + hb, b + 2 * hb
        b += 3 * hb
        self.QC0 = b
        self.KC0 = b + hb
        self.VC0 = b + hb + 2
        b += hb + 3
        self.n_real = b
        self.chunks_per_tile = 6
        self.n_blocks = -(-b // self.chunks_per_tile) * self.chunks_per_tile
        self.nz = self.n_blocks * LANES
        self.first_norm_tile = gate_blocks // self.chunks_per_tile

    def weight_segments(self):
        o, bw = self.orig, self.bw
        kc0, kc1 = o['kc'], o['kc'] + HALF
        segs = [(o['gates'], N_BRANCH * N_BRANCH * bw),
                (o['qa'], bw), (o['ka'], bw), (o['va'], bw),
                (o['qd'], bw), (o['kd'], bw), (o['vd'], bw),
                (o['qb'], bw), (o['kb'], bw), (o['vb'], bw),
                (o['qc'], bw),
                (kc0, HALF), (kc0, HALF), (kc1, HALF), (kc1, HALF),
                (o['vc'], LANES)]
        pad = self.nz - self.n_real * LANES
        if pad:
            segs.append((None, pad))
        return segs


def _build_wz(w_in, lay):
    parts = []
    for start, width in lay.weight_segments():
        if start is None:
            parts.append(jnp.zeros(w_in.shape[:-1] + (width,), w_in.dtype))
        else:
            parts.append(w_in[..., start:start + width])
    return jnp.concatenate(parts, axis=-1).astype(BF16)


def _build_coef(lay, qk_a, qk_b, qk_c, qk_d):
    hb = lay.bw // LANES
    a_hd = lay.bw // A_HEADS
    b_dk = lay.bw // (2 * B_HEADS)
    c_hd = lay.bw // C_HEADS
    d_hd = lay.bw // D_HEADS
    gain = jnp.ones((lay.nz,), F32)
    invd = np.ones((lay.nz,), np.float32)
    nflag = np.zeros((lay.nz,), np.float32)
    hflag = np.zeros((lay.nz,), np.float32)

    def put(gain, blk0, nblk, g, scale, group, half):
        lo, hi = blk0 * LANES, (blk0 + nblk) * LANES
        reps = (hi - lo) // g.shape[0]
        gain = gain.at[lo:hi].set(jnp.tile(g.astype(F32), reps) * scale)
        invd[lo:hi] = 1.0 / group
        nflag[lo:hi] = 1.0
        hflag[lo:hi] = 1.0 if half else 0.0
        return gain

    gain = put(gain, lay.QA0, hb, qk_a[0], a_hd ** -0.5, a_hd, False)
    gain = put(gain, lay.KA0, hb, qk_a[1], 1.0, a_hd, False)
    gain = put(gain, lay.QD0, hb, qk_d[0], d_hd ** -0.5, d_hd, False)
    gain = put(gain, lay.KD0, hb, qk_d[1], 1.0, d_hd, False)
    gain = put(gain, lay.QB0, hb, qk_b[0], b_dk ** -0.5 * float(np.log2(np.e)), b_dk, True)
    gain = put(gain, lay.KB0, hb, qk_b[1], 1.0, b_dk, True)
    gain = put(gain, lay.QC0, hb, qk_c[0], c_hd ** -0.5, c_hd, True)
    gain = put(gain, lay.KC0, 2, qk_c[1], 1.0, c_hd, True)
    return jnp.stack([gain, jnp.asarray(invd), jnp.asarray(nflag), jnp.asarray(hflag)])


def _ada_kernel(c_ref, w_ref, b_ref, o_ref):
    c = c_ref[...]
    cond = (c * _sigmoid(c)).astype(BF16)
    o_ref[0] = jnp.dot(cond, w_ref[0].astype(BF16), preferred_element_type=F32) + b_ref[0]


def _ada_mod(c, ada_w, ada_b, tn=1024):
    depth, d, n = ada_w.shape
    rows = 8
    cp = jnp.zeros((rows, d), F32).at[:c.shape[0]].set(c)
    tn = min(tn, n)
    return pl.pallas_call(
        _ada_kernel,
        out_shape=jax.ShapeDtypeStruct((depth, rows, n), F32),
        grid=(depth, n // tn),
        in_specs=[pl.BlockSpec((rows, d), lambda l, j: (0, 0)),
                  pl.BlockSpec((1, d, tn), lambda l, j: (l, 0, j)),
                  pl.BlockSpec((1, 1, tn), lambda l, j: (l, 0, j))],
        out_specs=pl.BlockSpec((1, rows, tn), lambda l, j: (l, 0, j)),
        compiler_params=_params("parallel", "parallel"),
        name="ada_mod",
    )(cp, ada_w, ada_b.reshape(depth, 1, n))


def _mod_norm(x, g, shift, scale):
    y = x * lax.rsqrt(jnp.mean(x * x, axis=-1, keepdims=True) + RMS_EPS) * g
    return y * (1.0 + scale) + shift


def _inproj_kernel(x_ref, mod_ref, g_ref, w_ref, coef_ref, z_ref, h_scr, *, first_norm_tile, n_chunks):
    j = pl.program_id(1)

    @pl.when(j == 0)
    def _():
        h = _mod_norm(x_ref[...], g_ref[...], mod_ref[0, 0:1, :], mod_ref[0, 1:2, :])
        h_scr[...] = h.astype(BF16)

    acc = jnp.dot(h_scr[...], w_ref[0], preferred_element_type=F32)

    @pl.when(j < first_norm_tile)
    def _():
        z_ref[...] = acc.astype(BF16)

    @pl.when(j >= first_norm_tile)
    def _():
        lo_lane = lax.broadcasted_iota(jnp.int32, (1, LANES), 1) < HALF
        for c in range(n_chunks):
            sl = slice(c * LANES, (c + 1) * LANES)
            y = acc[:, sl]
            gain = coef_ref[0, 0:1, sl]
            invd = coef_ref[0, 1:2, sl]
            nflag = coef_ref[0, 2:3, sl]
            hflag = coef_ref[0, 3:4, sl]
            y2 = y * y
            lo = jnp.sum(jnp.where(lo_lane, y2, 0.0), axis=-1, keepdims=True)
            hi = jnp.sum(jnp.where(lo_lane, 0.0, y2), axis=-1, keepdims=True)
            ss = jnp.where(hflag > 0.0, jnp.where(lo_lane, lo, hi), lo + hi)
            r = lax.rsqrt(ss * invd + RMS_EPS) * gain
            z_ref[:, sl] = (y * jnp.where(nflag > 0.0, r, 1.0)).astype(BF16)


def _inproj(x2, mod, g, wz, coef, layer, lay, seq, tm=1024):
    t, d = x2.shape
    tm = min(tm, seq)
    tn = lay.chunks_per_tile * LANES
    per_batch = seq // tm
    kern = functools.partial(_inproj_kernel, first_norm_tile=lay.first_norm_tile,
                             n_chunks=lay.chunks_per_tile)
    return pl.pallas_call(
        kern,
        out_shape=jax.ShapeDtypeStruct((t, lay.nz), BF16),
        grid=(t // tm, lay.nz // tn),
        in_specs=[pl.BlockSpec((tm, d), lambda i, j: (i, 0)),
                  pl.BlockSpec((1, 6, d), lambda i, j: (i // per_batch, 0, 0)),
                  pl.BlockSpec((1, d), lambda i, j: (0, 0)),
                  pl.BlockSpec((1, d, tn), lambda i, j: (layer, 0, j)),
                  pl.BlockSpec((1, 4, tn), lambda i, j: (layer, 0, j))],
        out_specs=pl.BlockSpec((tm, tn), lambda i, j: (i, j)),
        scratch_shapes=[pltpu.VMEM((tm, d), BF16)],
        compiler_params=_params("parallel", "arbitrary"),
        name="inproj",
    )(x2, mod, g, wz, coef)


def _band_kernel(sink_ref, q_ref, k_ref, v_ref, tab_ref, qmask_ref, o_ref, *, n_side, ck, nkc):
    h = pl.program_id(1)
    i = pl.program_id(2)
    q = q_ref[...] * qmask_ref[0]
    sink = sink_ref[h]
    scores = []
    starts = []
    m = jnp.full((q.shape[0], 1), sink, F32)
    for d in range(-n_side, n_side + 1):
        kc = i + d
        valid = jnp.logical_and(kc >= 0, kc < nkc)
        start = pl.multiple_of(jnp.clip(kc, 0, nkc - 1) * ck, ck)
        kb = k_ref[pl.ds(start, ck), :]
        s = lax.dot_general(q, kb, _NT, preferred_element_type=F32)
        s = s + tab_ref[0, d + n_side] + jnp.where(valid, 0.0, NEG_INF)
        m = jnp.maximum(m, jnp.max(s, axis=-1, keepdims=True))
        scores.append(s)
        starts.append(start)
    den = jnp.exp(sink - m)
    acc = jnp.zeros((q.shape[0], LANES), F32)
    for s, start in zip(scores, starts):
        p = jnp.exp(s - m)
        den = den + jnp.sum(p, axis=-1, keepdims=True)
        acc = acc + jnp.dot(p.astype(BF16), v_ref[pl.ds(start, ck), :], preferred_element_type=F32)
    o_ref[...] = (acc / den).astype(BF16)


def _band_attention(z, table, qmask, sink, *, batch, seq, n_heads, qblk, kblk, vblk, n_side):
    t = z.shape[0]
    tq = table.shape[-1]
    nq = seq // tq
    kern = functools.partial(_band_kernel, n_side=n_side, ck=tq, nkc=nq)
    nd = 2 * n_side + 1
    return pl.pallas_call(
        kern,
        out_shape=jax.ShapeDtypeStruct((t, n_heads * LANES), BF16),
        grid=(batch, n_heads, nq),
        in_specs=[pl.BlockSpec(memory_space=pltpu.SMEM),
                  pl.BlockSpec((tq, LANES), lambda b, h, i: (b * nq + i, qblk(h))),
                  pl.BlockSpec((seq, LANES), lambda b, h, i: (b, kblk(h))),
                  pl.BlockSpec((seq, LANES), lambda b, h, i: (b, vblk(h))),
                  pl.BlockSpec((1, nd, tq, tq), lambda b, h, i: (h, 0, 0, 0)),
                  pl.BlockSpec((1, 1, LANES), lambda b, h, i: (h, 0, 0))],
        out_specs=pl.BlockSpec((tq, LANES), lambda b, h, i: (b * nq + i, h)),
        compiler_params=_params("parallel", "parallel", "arbitrary"),
        name="band_attention",
    )(sink, z, z, z, table, qmask)


def _alibi_slopes(n):
    return np.array([2.0 ** (-8.0 * (i + 1) / n) for i in range(n)], dtype=np.float32)


def _dilated_table(tq):
    reach = max(w // 2 for w, _ in A_PATTERNS)
    n_side = -(-reach // tq)
    d = np.arange(-n_side, n_side + 1)[:, None, None] * tq
    off = d + np.arange(tq)[None, None, :] - np.arange(tq)[None, :, None]
    mult = np.zeros(off.shape, np.float64)
    for window, dil in A_PATTERNS:
        mult += ((off % dil) == 0) & (np.abs(off) <= window // 2)
    logm = np.where(mult > 0, np.log(np.maximum(mult, 1.0)), 0.0)
    slopes = _alibi_slopes(A_HEADS).astype(np.float64)
    tab = -slopes[:, None, None, None] * np.abs(off)[None] + logm[None]
    tab = np.where(mult[None] > 0, tab, NEG_INF)
    return jnp.asarray(tab.astype(np.float32)), n_side


def _window_table(tq):
    n_side = -(-C_RADIUS // tq)
    d = np.arange(-n_side, n_side + 1)[:, None, None] * tq
    off = d + np.arange(tq)[None, None, :] - np.arange(tq)[None, :, None]
    slopes = _alibi_slopes(C_HEADS).astype(np.float64)
    tab = -slopes[:, None, None, None] * np.abs(off)[None]
    tab = np.where((np.abs(off) <= C_RADIUS)[None], tab, NEG_INF)
    return jnp.asarray(tab.astype(np.float32)), n_side


_D_QROWS = 8
_D_KROWS = 16


def _nbr_kernel(q_ref, k_ref, v_ref, tab_ref, o_ref, *, rows):
    t = pl.program_id(2)
    krow = jnp.clip(t * _D_QROWS - D_WIN_R // 2, 0, rows - _D_KROWS)
    start = pl.multiple_of(krow * GRID_W, GRID_W * (D_WIN_R // 2))
    nk = _D_KROWS * GRID_W
    s = lax.dot_general(q_ref[...], k_ref[pl.ds(start, nk), :], _NT, preferred_element_type=F32)
    s = s + tab_ref[0, 0]
    m = jnp.max(s, axis=-1, keepdims=True)
    p = jnp.exp(s - m)
    den = jnp.sum(p, axis=-1, keepdims=True)
    acc = jnp.dot(p.astype(BF16), v_ref[pl.ds(start, nk), :], preferred_element_type=F32)
    o_ref[...] = (acc / den).astype(BF16)


def _nbr_tables(rpb, rows):
    depth, heads, n_dr, n_dc = rpb.shape
    nt = rows // _D_QROWS
    wr = min(D_WIN_R, rows)
    pad_l = GRID_W - 1 - (D_WIN_C - 1)
    vpad = jnp.pad(rpb.astype(F32), ((0, 0), (0, 0), (0, 0), (pad_l, 2 * GRID_W - 1 - n_dc - pad_l)))
    toep = jnp.stack([vpad[..., GRID_W - 1 - c:2 * GRID_W - 1 - c] for c in range(GRID_W)], axis=-2)
    c = np.arange(GRID_W)
    cs = np.clip(c - D_WIN_C // 2, 0, GRID_W - D_WIN_C)
    col_ok = (c[None, :] >= cs[:, None]) & (c[None, :] < cs[:, None] + D_WIN_C)
    colmat = jnp.where(col_ok, toep, NEG_INF)
    masked = jnp.full((depth, heads, 1, GRID_W, GRID_W), NEG_INF, F32)
    colmat = jnp.concatenate([colmat, masked], axis=2)
    sel = np.full((3, _D_QROWS, _D_KROWS), n_dr, np.int32)
    for kind, t in enumerate((0, 1, nt - 1)):
        krow0 = int(np.clip(t * _D_QROWS - D_WIN_R // 2, 0, rows - _D_KROWS))
        for ri in range(_D_QROWS):
            r = t * _D_QROWS + ri
            rs = int(np.clip(r - wr // 2, 0, rows - wr))
            for kj in range(_D_KROWS):
                kr = krow0 + kj
                if rs <= kr < rs + wr:
                    sel[kind, ri, kj] = kr - r + (D_WIN_R - 1)
    blocks = jnp.take(colmat, jnp.asarray(sel.reshape(-1)), axis=2)
    blocks = blocks.reshape(depth * heads, 3, _D_QROWS, _D_KROWS, GRID_W, GRID_W)
    blocks = blocks.transpose(0, 1, 2, 4, 3, 5)
    return blocks.reshape(depth * heads, 3, _D_QROWS * GRID_W, _D_KROWS * GRID_W)


def _nbr_attention(z, table, layer, *, batch, seq, lay):
    t = z.shape[0]
    rows = seq // GRID_W
    nt = rows // _D_QROWS
    tq = _D_QROWS * GRID_W
    nk = _D_KROWS * GRID_W
    kern = functools.partial(_nbr_kernel, rows=rows)
    kind = lambda i: jnp.where(i == 0, 0, jnp.where(i == nt - 1, 2, 1))
    return pl.pallas_call(
        kern,
        out_shape=jax.ShapeDtypeStruct((t, D_HEADS * LANES), BF16),
        grid=(batch, D_HEADS, nt),
        in_specs=[pl.BlockSpec((tq, LANES), lambda b, h, i: (b * nt + i, lay.QD0 + h)),
                  pl.BlockSpec((seq, LANES), lambda b, h, i: (b, lay.KD0 + h)),
                  pl.BlockSpec((seq, LANES), lambda b, h, i: (b, lay.VD0 + h)),
                  pl.BlockSpec((1, 1, tq, nk), lambda b, h, i: (layer * D_HEADS + h, kind(i), 0, 0))],
        out_specs=pl.BlockSpec((tq, LANES), lambda b, h, i: (b * nt + i, h)),
        compiler_params=_params("parallel", "parallel", "arbitrary"),
        name="nbr_attention",
    )(z, z, z, table)


_N_BIAS_PIECES = 3
_KPOS_RADIX = 64


def _bf16_round(v):
    u = np.asarray(v, np.float32).view(np.uint32)
    u = (u + np.uint32(0x7FFF) + ((u >> np.uint32(16)) & np.uint32(1))) & np.uint32(0xFFFF0000)
    return u.view(np.float32)


def _diff_feature_tables(tk):
    c32 = (_alibi_slopes(B_HEADS).astype(np.float64) * np.log2(np.e)).astype(np.float32)
    qconst = np.zeros((B_HEADS, 2, LANES), np.float32)
    for h in range(B_HEADS):
        rem = np.float32(c32[h])
        pieces = []
        for _ in range(_N_BIAS_PIECES):
            p = np.float32(_bf16_round(rem))
            pieces.append(p)
            rem = np.float32(rem - p)
        for br, base in ((0, HALF), (1, 0)):
            for n, p in enumerate(pieces):
                qconst[h, br, base + n] = -_KPOS_RADIX * p
                qconst[h, br, base + _N_BIAS_PIECES + n] = -p
    j = np.arange(tk)
    kfeat = np.zeros((2, tk, LANES), np.float32)
    for br, base in ((0, HALF), (1, 0)):
        for n in range(_N_BIAS_PIECES):
            kfeat[br, :, base + n] = j // _KPOS_RADIX
            kfeat[br, :, base + _N_BIAS_PIECES + n] = j % _KPOS_RADIX
            kfeat[br, :, base + 2 * _N_BIAS_PIECES + n] = 1.0
    return jnp.asarray(qconst), jnp.asarray(kfeat, BF16), jnp.asarray(c32)


def _diff_kernel(c_ref, misc_ref, q_ref, k_ref, v_ref, qconst_ref, kfeat_ref, dl_ref, subg_ref, o_ref,
                 kaug_scr, m_scr, l_scr, acc_scr, *, tq, tk, nk):
    h = pl.program_id(1)
    i = pl.program_id(2)
    c = c_ref[h]
    lam_init = misc_ref[0]
    lane = lax.broadcasted_iota(jnp.int32, (1, LANES), 1)
    lo_lane = lane < HALF

    @pl.when(i == 0)
    def _():
        def aug(cc, carry):
            sl = pl.ds(pl.multiple_of(cc * tk, tk), tk)
            kb = k_ref[sl, :]
            kaug_scr[0, sl, :] = jnp.where(lo_lane, kb, kfeat_ref[0])
            kaug_scr[1, sl, :] = jnp.where(lo_lane, kfeat_ref[1], kb)
            return carry
        lax.fori_loop(0, nk, aug, 0)

    q = q_ref[...].astype(F32)
    ci = c * lax.broadcasted_iota(jnp.int32, (tq, 1), 0).astype(F32)
    r1 = ci.astype(BF16).astype(F32)
    r2 = (ci - r1).astype(BF16).astype(F32)
    r3 = (ci - r1 - r2).astype(BF16).astype(F32)
    q_plain, q_right, q_left = [], [], []
    for br, base in ((0, HALF), (1, 0)):
        keep = lo_lane if br == 0 else jnp.logical_not(lo_lane)
        f0 = base + 2 * _N_BIAS_PIECES
        feat = qconst_ref[0, br:br + 1, :] + jnp.where(
            lane == f0, r1, jnp.where(lane == f0 + 1, r2, jnp.where(lane == f0 + 2, r3, 0.0)))
        q_plain.append(jnp.where(keep, q, 0.0).astype(BF16))
        q_right.append(jnp.where(keep, q, feat).astype(BF16))
        q_left.append(jnp.where(keep, q, -feat).astype(BF16))

    m_scr[...] = jnp.full(m_scr.shape, NEG_INF, F32)
    l_scr[...] = jnp.zeros(l_scr.shape, F32)
    acc_scr[...] = jnp.zeros(acc_scr.shape, F32)

    def update(idx, s, shift, vb):
        m_old = m_scr[idx]
        m_new = jnp.maximum(m_old, jnp.max(s, axis=-1, keepdims=True) + shift)
        alpha = jnp.exp2(m_old - m_new)
        p = jnp.exp2(s + (shift - m_new))
        l_scr[idx] = alpha * l_scr[idx] + jnp.sum(p, axis=-1, keepdims=True)
        acc_scr[idx] = alpha * acc_scr[idx] + jnp.dot(p.astype(BF16), vb, preferred_element_type=F32)
        m_scr[idx] = m_new

    def chunk(cc, qops, shift_sign):
        sl = pl.ds(pl.multiple_of(cc * tk, tk), tk)
        vb = v_ref[sl, :]
        shift = shift_sign * c * (cc * tk - i * tq).astype(F32)
        ss = [lax.dot_general(qops[idx], kaug_scr[idx, sl, :], _NT, preferred_element_type=F32)
              for idx in range(2)]
        for idx in range(2):
            update(idx, ss[idx], shift, vb)

    cdiag = (i * tq) // tk

    def left_body(cc, carry):
        chunk(cc, q_left, 1.0)
        return carry

    def right_body(cc, carry):
        chunk(cc, q_right, -1.0)
        return carry

    lax.fori_loop(0, cdiag, left_body, 0)

    sl = pl.ds(pl.multiple_of(cdiag * tk, tk), tk)
    dji = (lax.broadcasted_iota(jnp.int32, (tq, tk), 1) - lax.broadcasted_iota(jnp.int32, (tq, tk), 0))
    bias = -c * jnp.abs(dji + (cdiag * tk - i * tq)).astype(F32)
    for idx in range(2):
        s = lax.dot_general(q_plain[idx], k_ref[sl, :], _NT, preferred_element_type=F32) + bias
        update(idx, s, 0.0, v_ref[sl, :])

    lax.fori_loop(cdiag + 1, nk, right_body, 0)

    hs = pl.ds(h, 1)
    lam = (jnp.exp(jnp.sum(dl_ref[0, hs, :] * dl_ref[1, hs, :], axis=-1, keepdims=True))
           - jnp.exp(jnp.sum(dl_ref[2, hs, :] * dl_ref[3, hs, :], axis=-1, keepdims=True)) + lam_init)
    o = acc_scr[0] / l_scr[0] - lam * (acc_scr[1] / l_scr[1])
    o = o * lax.rsqrt(jnp.mean(o * o, axis=-1, keepdims=True) + RMS_EPS) * subg_ref[...]
    o_ref[...] = (o * (1.0 - lam_init)).astype(BF16)


def _diff_attention(z, dl, subg, tables, misc, *, batch, seq, lay, tq, tk):
    t = z.shape[0]
    nq = seq // tq
    qconst, kfeat, cvals = tables
    kern = functools.partial(_diff_kernel, tq=tq, tk=tk, nk=seq // tk)
    return pl.pallas_call(
        kern,
        out_shape=jax.ShapeDtypeStruct((t, B_HEADS * LANES), BF16),
        grid=(batch, B_HEADS, nq),
        in_specs=[pl.BlockSpec(memory_space=pltpu.SMEM),
                  pl.BlockSpec(memory_space=pltpu.SMEM),
                  pl.BlockSpec((tq, LANES), lambda b, h, i: (b * nq + i, lay.QB0 + h)),
                  pl.BlockSpec((seq, LANES), lambda b, h, i: (b, lay.KB0 + h)),
                  pl.BlockSpec((seq, LANES), lambda b, h, i: (b, lay.VB0 + h)),
                  pl.BlockSpec((1, 2, LANES), lambda b, h, i: (h, 0, 0)),
                  pl.BlockSpec(kfeat.shape, lambda b, h, i: (0, 0, 0)),
                  pl.BlockSpec(dl.shape, lambda b, h, i: (0, 0, 0)),
                  pl.BlockSpec((1, LANES), lambda b, h, i: (0, 0))],
        out_specs=pl.BlockSpec((tq, LANES), lambda b, h, i: (b * nq + i, h)),
        scratch_shapes=[pltpu.VMEM((2, seq, LANES), BF16),
                        pltpu.VMEM((2, tq, 1), F32), pltpu.VMEM((2, tq, 1), F32),
                        pltpu.VMEM((2, tq, LANES), F32)],
        compiler_params=_params("parallel", "arbitrary", "arbitrary"),
        name="diff_attention",
    )(cvals, misc, z, z, z, qconst, kfeat, dl, subg)


def _merge_kernel(oa_ref, ob_ref, oc_ref, od_ref, gate_ref, wa_ref, wb_ref, wc_ref, wd_ref, o_ref, *, d):
    acc = None
    for idx, (o, w) in enumerate(((oa_ref, wa_ref), (ob_ref, wb_ref), (oc_ref, wc_ref), (od_ref, wd_ref))):
        y = jnp.dot(o[...], w[...], preferred_element_type=F32)
        g = _sigmoid(gate_ref[:, idx * d:(idx + 1) * d].astype(F32))
        acc = g * y if acc is None else acc + g * y
    o_ref[...] = acc.astype(BF16)


def _merge(oa, ob, oc, od, z, wbs, *, d, tm=256):
    t = oa.shape[0]
    tm = min(tm, t)
    row = lambda a: pl.BlockSpec((tm, a.shape[1]), lambda i: (i, 0))
    full = lambda a: pl.BlockSpec(a.shape, lambda i: (0, 0))
    return pl.pallas_call(
        functools.partial(_merge_kernel, d=d),
        out_shape=jax.ShapeDtypeStruct((t, d), BF16),
        grid=(t // tm,),
        in_specs=[row(oa), row(ob), row(oc), row(od),
                  pl.BlockSpec((tm, N_BRANCH * d), lambda i: (i, 0))] + [full(w) for w in wbs],
        out_specs=pl.BlockSpec((tm, d), lambda i: (i, 0)),
        compiler_params=_params("parallel"),
        name="branch_merge",
    )(oa, ob, oc, od, z, *wbs)


def _outproj_kernel(a_ref, w_ref, x_ref, g_ref, o_ref):
    y = jnp.dot(a_ref[...], w_ref[0], preferred_element_type=F32)
    o_ref[...] = x_ref[...] + g_ref[0] * y


def _outproj(a, w, layer, x2, gate, seq, tm=512):
    t, d = x2.shape
    tm = min(tm, seq)
    per_batch = seq // tm
    return pl.pallas_call(
        _outproj_kernel,
        out_shape=jax.ShapeDtypeStruct((t, d), F32),
        grid=(t // tm,),
        in_specs=[pl.BlockSpec((tm, a.shape[1]), lambda i: (i, 0)),
                  pl.BlockSpec((1,) + w.shape[1:], lambda i: (layer, 0, 0)),
                  pl.BlockSpec((tm, d), lambda i: (i, 0)),
                  pl.BlockSpec((1, 1, d), lambda i: (i // per_batch, 0, 0))],
        out_specs=pl.BlockSpec((tm, d), lambda i: (i, 0)),
        compiler_params=_params("parallel"),
        name="out_proj",
    )(a, w, x2, gate)


def _router_kernel(x_ref, mod_ref, g_ref, rwt_ref, rb_ref, h_ref, e_ref, gw_ref):
    h = _mod_norm(x_ref[...], g_ref[...], mod_ref[0, 3:4, :], mod_ref[0, 4:5, :]).astype(BF16)
    h_ref[...] = h
    logits = lax.dot_general(rwt_ref[...], h, _NT, preferred_element_type=F32)
    score = _sigmoid(logits)
    sel = score + rb_ref[...]
    sel_r = [sel[e:e + 1, :] for e in range(N_EXPERTS)]
    sc_r = [score[e:e + 1, :] for e in range(N_EXPERTS)]
    n = EXPERTS_PER_GROUP

    def top2_sum(v):
        best = None
        for a in range(n):
            for b in range(a + 1, n):
                s = v[a] + v[b]
                best = s if best is None else jnp.maximum(best, s)
        return best

    gbest = top2_sum(sel_r[0:n])
    gidx = jnp.zeros(gbest.shape, jnp.int32)
    cv = list(sel_r[0:n])
    cs = list(sc_r[0:n])
    for g in range(1, N_GROUPS):
        gs = top2_sum(sel_r[g * n:(g + 1) * n])
        upd = gs > gbest
        gbest = jnp.where(upd, gs, gbest)
        gidx = jnp.where(upd, g, gidx)
        for j in range(n):
            cv[j] = jnp.where(upd, sel_r[g * n + j], cv[j])
            cs[j] = jnp.where(upd, sc_r[g * n + j], cs[j])
    b1, i1, s1 = cv[0], jnp.zeros(gbest.shape, jnp.int32), cs[0]
    for j in range(1, n):
        upd = cv[j] > b1
        b1 = jnp.where(upd, cv[j], b1)
        i1 = jnp.where(upd, j, i1)
        s1 = jnp.where(upd, cs[j], s1)
    low = -3.0e38
    b2 = jnp.full(gbest.shape, low, F32)
    i2 = jnp.zeros(gbest.shape, jnp.int32)
    s2 = jnp.zeros(gbest.shape, F32)
    for j in range(n):
        cand = jnp.where(i1 == j, low, cv[j])
        upd = cand > b2
        b2 = jnp.where(upd, cand, b2)
        i2 = jnp.where(upd, j, i2)
        s2 = jnp.where(upd, cs[j], s2)
    tot = s1 + s2
    e_ref[0:1, :] = gidx * n + i1
    e_ref[1:2, :] = gidx * n + i2
    gw_ref[0:1, :] = s1 / tot
    gw_ref[1:2, :] = s2 / tot


def _router(x2, mod, g, rwt, rb, seq, tm=512):
    t, d = x2.shape
    tm = min(tm, seq)
    per_batch = seq // tm
    return pl.pallas_call(
        _router_kernel,
        out_shape=(jax.ShapeDtypeStruct((t, d), BF16),
                   jax.ShapeDtypeStruct((2, t), jnp.int32),
                   jax.ShapeDtypeStruct((2, t), F32)),
        grid=(t // tm,),
        in_specs=[pl.BlockSpec((tm, d), lambda i: (i, 0)),
                  pl.BlockSpec((1, 6, d), lambda i: (i // per_batch, 0, 0)),
                  pl.BlockSpec((1, d), lambda i: (0, 0)),
                  pl.BlockSpec(rwt.shape, lambda i: (0, 0)),
                  pl.BlockSpec(rb.shape, lambda i: (0, 0))],
        out_specs=(pl.BlockSpec((tm, d), lambda i: (i, 0)),
                   pl.BlockSpec((2, tm), lambda i: (0, i)),
                   pl.BlockSpec((2, tm), lambda i: (0, i))),
        compiler_params=_params("parallel"),
        name="norm_router",
    )(x2, mod, g, rwt, rb)


def _rank_kernel(e_ref, rank_ref, cnt_ref, carry):
    @pl.when(pl.program_id(0) == 0)
    def _():
        carry[...] = jnp.zeros(carry.shape, F32)

    tm = e_ref.shape[1]
    eio = lax.broadcasted_iota(jnp.int32, (N_EXPERTS, tm), 0)
    oh0 = eio == e_ref[0:1, :]
    oh1 = eio == e_ref[1:2, :]
    c = jnp.where(jnp.logical_or(oh0, oh1), 1.0, 0.0)
    before = (lax.broadcasted_iota(jnp.int32, (tm, tm), 0) < lax.broadcasted_iota(jnp.int32, (tm, tm), 1))
    pref = jnp.dot(c.astype(BF16), jnp.where(before, 1.0, 0.0).astype(BF16),
                   preferred_element_type=F32) + carry[...]
    rank_ref[0:1, :] = jnp.sum(jnp.where(oh0, pref, 0.0), axis=0, keepdims=True).astype(jnp.int32)
    rank_ref[1:2, :] = jnp.sum(jnp.where(oh1, pref, 0.0), axis=0, keepdims=True).astype(jnp.int32)
    total = carry[...] + jnp.sum(c, axis=1, keepdims=True)
    carry[...] = total
    cnt_ref[...] = jnp.broadcast_to(total, cnt_ref.shape)


def _rank(eidx, tm=512):
    t = eidx.shape[1]
    tm = min(tm, t)
    return pl.pallas_call(
        _rank_kernel,
        out_shape=(jax.ShapeDtypeStruct((2, t), jnp.int32),
                   jax.ShapeDtypeStruct((N_EXPERTS, LANES), F32)),
        grid=(t // tm,),
        in_specs=[pl.BlockSpec((2, tm), lambda i: (0, i))],
        out_specs=(pl.BlockSpec((2, tm), lambda i: (0, i)),
                   pl.BlockSpec((N_EXPERTS, LANES), lambda i: (0, 0))),
        scratch_shapes=[pltpu.VMEM((N_EXPERTS, 1), F32)],
        compiler_params=_params("arbitrary"),
        name="expert_rank",
    )(eidx)


def _gather_kernel(src_ref, h_hbm, o_ref, sem, *, rows):
    base = pl.program_id(0) * rows

    def issue(r, carry):
        pltpu.make_async_copy(h_hbm.at[src_ref[base + r]], o_ref.at[r], sem).start()
        return carry

    lax.fori_loop(0, rows, issue, 0)

    def drain(r, carry):
        pltpu.make_async_copy(h_hbm.at[0], o_ref.at[r], sem).wait()
        return carry

    lax.fori_loop(0, rows, drain, 0)


def _gather_rows(src, h3, rows=512):
    p = src.shape[0]
    rows = min(rows, p)
    sub = h3.shape[1]
    return pl.pallas_call(
        functools.partial(_gather_kernel, rows=rows),
        out_shape=jax.ShapeDtypeStruct((p, sub, LANES), h3.dtype),
        grid_spec=pltpu.PrefetchScalarGridSpec(
            num_scalar_prefetch=1,
            grid=(p // rows,),
            in_specs=[pl.BlockSpec(memory_space=pl.ANY)],
            out_specs=pl.BlockSpec((rows, sub, LANES), lambda i, src: (i, 0, 0)),
            scratch_shapes=[pltpu.SemaphoreType.DMA]),
        compiler_params=_params("arbitrary"),
        name="dispatch_gather",
    )(src, h3)


def _expert_kernel(te_ref, nu_ref, xs_ref, gw_ref, w1_ref, w3_ref, w2_ref, y_ref):
    j = pl.program_id(0)

    @pl.when(j < nu_ref[0])
    def _():
        xs = xs_ref[...]
        g = jnp.dot(xs, w1_ref[0], preferred_element_type=F32)
        u = jnp.dot(xs, w3_ref[0], preferred_element_type=F32)
        mid = (g * _sigmoid(g) * u).astype(BF16)
        y = jnp.dot(mid, w2_ref[0], preferred_element_type=F32)
        y_ref[...] = y * gw_ref[...]

    @pl.when(j >= nu_ref[0])
    def _():
        y_ref[...] = jnp.zeros(y_ref.shape, F32)


def _experts(tile_e, n_used, xs, gws, w1, w3, w2, layer, tm):
    p, d = xs.shape
    f = w1.shape[-1]
    n_exp = w1.shape[1]
    w1r = w1.reshape((-1,) + w1.shape[2:])
    w3r = w3.reshape((-1,) + w3.shape[2:])
    w2r = w2.reshape((-1,) + w2.shape[2:])
    wmap = lambda j, te, nu: (layer * n_exp + te[j], 0, 0)
    return pl.pallas_call(
        _expert_kernel,
        out_shape=jax.ShapeDtypeStruct((p, d), F32),
        grid_spec=pltpu.PrefetchScalarGridSpec(
            num_scalar_prefetch=2,
            grid=(p // tm,),
            in_specs=[pl.BlockSpec((tm, d), lambda j, te, nu: (j, 0)),
                      pl.BlockSpec((tm, 1), lambda j, te, nu: (j, 0)),
                      pl.BlockSpec((1, d, f), wmap),
                      pl.BlockSpec((1, d, f), wmap),
                      pl.BlockSpec((1, f, d), wmap)],
            out_specs=pl.BlockSpec((tm, d), lambda j, te, nu: (j, 0))),
        compiler_params=_params("arbitrary"),
        name="experts",
    )(tile_e, n_used, xs, gws, w1r, w3r, w2r)


def _combine_kernel(pos_ref, x_ref, g_ref, y_hbm, o_ref, buf, sem, *, tm, t):
    base = pl.program_id(0) * tm

    def issue(r, carry):
        pltpu.make_async_copy(y_hbm.at[pos_ref[base + r]], buf.at[0, r], sem).start()
        pltpu.make_async_copy(y_hbm.at[pos_ref[t + base + r]], buf.at[1, r], sem).start()
        return carry

    lax.fori_loop(0, tm, issue, 0)

    def drain(r, carry):
        pltpu.make_async_copy(y_hbm.at[0], buf.at[0, r], sem).wait()
        pltpu.make_async_copy(y_hbm.at[0], buf.at[1, r], sem).wait()
        return carry

    lax.fori_loop(0, tm, drain, 0)
    o_ref[...] = x_ref[...] + g_ref[...] * (buf[0] + buf[1])


def _combine(pos, x3, g3, y3, seq, tm=256):
    t, sub, _ = x3.shape
    tm = min(tm, seq)
    per_batch = seq // tm
    return pl.pallas_call(
        functools.partial(_combine_kernel, tm=tm, t=t),
        out_shape=jax.ShapeDtypeStruct(x3.shape, F32),
        grid_spec=pltpu.PrefetchScalarGridSpec(
            num_scalar_prefetch=1,
            grid=(t // tm,),
            in_specs=[pl.BlockSpec((tm, sub, LANES), lambda i, pos: (i, 0, 0)),
                      pl.BlockSpec((1, sub, LANES), lambda i, pos: (i // per_batch, 0, 0)),
                      pl.BlockSpec(memory_space=pl.ANY)],
            out_specs=pl.BlockSpec((tm, sub, LANES), lambda i, pos: (i, 0, 0)),
            scratch_shapes=[pltpu.VMEM((2, tm, sub, LANES), F32), pltpu.SemaphoreType.DMA]),
        compiler_params=_params("arbitrary"),
        name="moe_combine",
    )(pos, x3, g3, y3)


def _moe(x2, mod, g2n, rwt, rb, w1, w3, w2, layer, *, batch, seq, tm_e=512):
    t, d = x2.shape
    sub = d // LANES
    h, eidx, gw = _router(x2, mod, g2n, rwt, rb, seq)
    rank, cnt = _rank(eidx)
    tm_e = min(tm_e, t)
    counts = cnt[:, 0].astype(jnp.int32)
    gsz = ((counts + tm_e - 1) // tm_e) * tm_e
    ends = jnp.cumsum(gsz)
    offs = ends - gsz
    pos = rank
    for e in range(N_EXPERTS):
        pos = pos + jnp.where(eidx == e, offs[e], 0)
    n_tiles = (2 * t) // tm_e + N_EXPERTS
    p = n_tiles * tm_e
    flat = pos.reshape(-1)
    tok = jnp.broadcast_to(jnp.arange(t, dtype=jnp.int32)[None, :], (2, t)).reshape(-1)
    src = jnp.zeros((p,), jnp.int32).at[flat].set(tok)
    gws = jnp.zeros((p,), F32).at[flat].set(gw.reshape(-1)).reshape(p, 1)
    tile_e = jnp.minimum(jnp.searchsorted(ends, jnp.arange(n_tiles, dtype=jnp.int32) * tm_e, side='right'),
                         N_EXPERTS - 1).astype(jnp.int32)
    n_used = (ends[-1:] // tm_e).astype(jnp.int32)
    xs = _gather_rows(src, h.reshape(t, sub, LANES))
    y = _experts(tile_e, n_used, xs.reshape(p, d), gws, w1, w3, w2, layer, tm_e)
    g3 = mod[:, 5, :].reshape(batch, sub, LANES)
    out = _combine(flat, x2.reshape(t, sub, LANES), g3, y.reshape(p, sub, LANES), seq)
    return out.reshape(t, d)


def kernel(x, c, ada_w, ada_b, norm1_g, norm2_g, w_in, qk_norm_a, qk_norm_b, qk_norm_c, qk_norm_d,
           diff_lambda, diff_subln_g, sink_logits, rpb, w_branch, w_out, router_w, router_b, w1, w3, w2):
    batch, seq, d = x.shape
    depth = ada_w.shape[0]
    t = batch * seq
    lay = _Layout(d)
    bw = lay.bw

    mod_all = _ada_mod(c, ada_w, ada_b)
    wz = _build_wz(w_in, lay)
    coef = jnp.stack([_build_coef(lay, qk_norm_a[l], qk_norm_b[l], qk_norm_c[l], qk_norm_d[l])
                      for l in range(depth)])
    wb16 = w_branch.astype(BF16)
    c_hd = bw // C_HEADS
    grp = C_HEADS // C_KV_HEADS
    wc_rows = []
    for hq in range(C_HEADS):
        blk = jnp.zeros((depth, LANES, d), BF16)
        off = (hq // grp) * c_hd
        wc_rows.append(blk.at[:, off:off + c_hd].set(wb16[:, 2, hq * c_hd:(hq + 1) * c_hd]))
    wc_ext = jnp.concatenate(wc_rows, axis=1)
    wout16 = w_out.astype(BF16)
    w1b, w3b, w2b = w1.astype(BF16), w3.astype(BF16), w2.astype(BF16)
    rwt = router_w.T.astype(BF16)
    rb = router_b.reshape(N_EXPERTS, 1).astype(F32)

    tq = 256
    tab_a, side_a = _dilated_table(tq)
    tab_c, side_c = _window_table(tq)
    ones_mask = jnp.ones((A_HEADS, 1, LANES), BF16)
    lane = np.arange(LANES)
    cmask = np.stack([(lane // HALF) == (hq % 2) for hq in range(C_HEADS)]).astype(np.float32)
    cmask = jnp.asarray(cmask.reshape(C_HEADS, 1, LANES), BF16)
    no_sink = jnp.full((A_HEADS,), NEG_INF, F32)
    tq_b, tk_b = min(512, seq), min(1024, seq)
    tables_b = _diff_feature_tables(tk_b)
    tab_d = _nbr_tables(rpb, seq // GRID_W)

    x2 = x.reshape(t, d)
    for l in range(depth):
        mod = mod_all[l, :batch].reshape(batch, 6, d)
        z = _inproj(x2, mod, norm1_g[l].reshape(1, d), wz, coef, l, lay, seq)
        o_a = _band_attention(z, tab_a, ones_mask, no_sink, batch=batch, seq=seq, n_heads=A_HEADS,
                              qblk=lambda h: lay.QA0 + h, kblk=lambda h: lay.KA0 + h,
                              vblk=lambda h: lay.VA0 + h, n_side=side_a)
        lam_init = 0.8 - 0.6 * float(np.exp(-0.3 * l))
        o_b = _diff_attention(z, diff_lambda[l], diff_subln_g[l].reshape(1, LANES), tables_b,
                              jnp.full((1,), lam_init, F32), batch=batch, seq=seq, lay=lay,
                              tq=tq_b, tk=tk_b)
        o_c = _band_attention(z, tab_c, cmask, sink_logits[l].astype(F32), batch=batch, seq=seq,
                              n_heads=C_HEADS, qblk=lambda h: lay.QC0 + h // 2,
                              kblk=lambda h: lay.KC0 + h // grp, vblk=lambda h: lay.VC0,
                              n_side=side_c)
        o_d = _nbr_attention(z, tab_d, l, batch=batch, seq=seq, lay=lay)
        merged = _merge(o_a, o_b, o_c, o_d, z, (wb16[l, 0], wb16[l, 1], wc_ext[l], wb16[l, 3]), d=d)
        x2 = _outproj(merged, wout16, l, x2, mod[:, 2:3, :], seq)
        x2 = _moe(x2, mod, norm2_g[l].reshape(1, d), rwt, rb, w1b, w3b, w2b, l, batch=batch, seq=seq)
    return x2.reshape(batch, seq, d)
```

```python
import functools

import numpy as np
import jax
import jax.numpy as jnp
from jax import lax
from jax.experimental import pallas as pl
from jax.experimental.pallas import tpu as pltpu

F32 = jnp.float32
BF16 = jnp.bfloat16
LANES = 128
HALF = LANES // 2
RMS_EPS = 1e-6
NEG_INF = -1e30
VMEM_LIMIT = 56 * 1024 * 1024

GRID_W = 64
N_BRANCH = 4
A_HEADS = 4
A_PATTERNS = ((128, 1), (512, 4), (2048, 16))
B_HEADS = 4
C_HEADS = 8
C_KV_HEADS = 2
C_RADIUS = 128
D_HEADS = 4
D_WIN_R = 8
D_WIN_C = 16
N_EXPERTS = 16
N_GROUPS = 4
EXPERTS_PER_GROUP = N_EXPERTS // N_GROUPS

_NT = (((1,), (1,)), ((), ()))


def _params(*sem):
    return pltpu.CompilerParams(dimension_semantics=sem, vmem_limit_bytes=VMEM_LIMIT)


def _sigmoid(v):
    return 1.0 / (1.0 + jnp.exp(-v))


class _Layout:
    def __init__(self, d_model):
        bw = d_model // N_BRANCH
        self.bw = bw
        hb = bw // LANES
        o = {}
        o['qa'], o['ka'], o['va'] = 0, bw, 2 * bw
        o['qb'], o['kb'], o['vb'] = 3 * bw, 4 * bw, 5 * bw
        o['qc'] = 6 * bw
        o['kc'] = 7 * bw
        o['vc'] = 7 * bw + LANES
        o['qd'], o['kd'], o['vd'] = 7 * bw + 2 * LANES, 8 * bw + 2 * LANES, 9 * bw + 2 * LANES
        o['gates'] = 10 * bw + 2 * LANES
        self.orig = o
        gate_blocks = N_BRANCH * d_model // LANES
        self.GATE0 = 0
        b = gate_blocks
        self.QA0, self.KA0, self.VA0 = b, b + hb, b + 2 * hb
        b += 3 * hb
        self.QD0, self.KD0, self.VD0 = b, b + hb, b + 2 * hb
        b += 3 * hb
        self.QB0, self.KB0, self.VB0 = b, b + hb, b + 2 * hb
        b += 3 * hb
        self.QC0 = b
        self.KC0 = b + hb
        self.VC0 = b + hb + 2
        b += hb + 3
        self.n_real = b
        self.chunks_per_tile = 6
        self.n_blocks = -(-b // self.chunks_per_tile) * self.chunks_per_tile
        self.nz = self.n_blocks * LANES
        self.first_norm_tile = gate_blocks // self.chunks_per_tile
        assert self.QC0 % (C_HEADS // C_KV_HEADS // 2) == 0
        norm = [0] * self.n_blocks
        for blk0, nblk, group in ((self.QA0, 2 * hb, bw // A_HEADS), (self.QD0, 2 * hb, bw // D_HEADS),
                                  (self.QB0, 2 * hb, bw // (2 * B_HEADS)),
                                  (self.QC0, hb + 2, bw // C_HEADS)):
            assert group in (LANES, HALF)
            norm[blk0:blk0 + nblk] = [group] * nblk
        self.norm_group = tuple(norm)

    def weight_segments(self):
        o, bw = self.orig, self.bw
        kc0, kc1 = o['kc'], o['kc'] + HALF
        segs = [(o['gates'], N_BRANCH * N_BRANCH * bw),
                (o['qa'], bw), (o['ka'], bw), (o['va'], bw),
                (o['qd'], bw), (o['kd'], bw), (o['vd'], bw),
                (o['qb'], bw), (o['kb'], bw), (o['vb'], bw),
                (o['qc'], bw),
                (kc0, HALF), (kc0, HALF), (kc1, HALF), (kc1, HALF),
                (o['vc'], LANES)]
        pad = self.nz - self.n_real * LANES
        if pad:
            segs.append((None, pad))
        return segs


def _build_wz(w_in, lay):
    parts = []
    for start, width in lay.weight_segments():
        if start is None:
            parts.append(jnp.zeros(w_in.shape[:-1] + (width,), w_in.dtype))
        else:
            parts.append(w_in[..., start:start + width])
    return jnp.concatenate(parts, axis=-1).astype(BF16)


def _build_coef(lay, qk_a, qk_b, qk_c, qk_d):
    hb = lay.bw // LANES
    a_hd = lay.bw // A_HEADS
    b_dk = lay.bw // (2 * B_HEADS)
    c_hd = lay.bw // C_HEADS
    d_hd = lay.bw // D_HEADS
    gain = jnp.ones((lay.nz,), F32)

    def put(gain, blk0, nblk, g, scale):
        lo, hi = blk0 * LANES, (blk0 + nblk) * LANES
        reps = (hi - lo) // g.shape[0]
        return gain.at[lo:hi].set(jnp.tile(g.astype(F32), reps) * scale)

    gain = put(gain, lay.QA0, hb, qk_a[0], a_hd ** -0.5)
    gain = put(gain, lay.KA0, hb, qk_a[1], 1.0)
    gain = put(gain, lay.QD0, hb, qk_d[0], d_hd ** -0.5)
    gain = put(gain, lay.KD0, hb, qk_d[1], 1.0)
    gain = put(gain, lay.QB0, hb, qk_b[0], b_dk ** -0.5 * float(np.log2(np.e)))
    gain = put(gain, lay.KB0, hb, qk_b[1], 1.0)
    gain = put(gain, lay.QC0, hb, qk_c[0], c_hd ** -0.5)
    gain = put(gain, lay.KC0, 2, qk_c[1], 1.0)
    return gain.reshape(1, lay.nz)


def _ada_kernel(c_ref, w_ref, b_ref, o_ref):
    c = c_ref[...]
    cond = (c * _sigmoid(c)).astype(BF16)
    o_ref[0] = jnp.dot(cond, w_ref[0].astype(BF16), preferred_element_type=F32) + b_ref[0]


def _ada_mod(c, ada_w, ada_b, tn=1024):
    depth, d, n = ada_w.shape
    rows = 8
    cp = jnp.zeros((rows, d), F32).at[:c.shape[0]].set(c)
    tn = min(tn, n)
    return pl.pallas_call(
        _ada_kernel,
        out_shape=jax.ShapeDtypeStruct((depth, rows, n), F32),
        grid=(depth, n // tn),
        in_specs=[pl.BlockSpec((rows, d), lambda l, j: (0, 0)),
                  pl.BlockSpec((1, d, tn), lambda l, j: (l, 0, j)),
                  pl.BlockSpec((1, 1, tn), lambda l, j: (l, 0, j))],
        out_specs=pl.BlockSpec((1, rows, tn), lambda l, j: (l, 0, j)),
        compiler_params=_params("parallel", "parallel"),
        name="ada_mod",
    )(cp, ada_w, ada_b.reshape(depth, 1, n))


def _mod_norm(x, g, shift, scale):
    y = x * lax.rsqrt(jnp.mean(x * x, axis=-1, keepdims=True) + RMS_EPS) * g
    return y * (1.0 + scale) + shift


def _inproj_kernel(x_ref, mod_ref, g_ref, w_ref, gain_ref, z_ref, h_scr, *, first_norm_tile, n_chunks,
                   norm_group):
    j = pl.program_id(1)

    @pl.when(j == 0)
    def _():
        h = _mod_norm(x_ref[...], g_ref[...], mod_ref[0, 0:1, :], mod_ref[0, 1:2, :])
        h_scr[...] = h.astype(BF16)

    acc = jnp.dot(h_scr[...], w_ref[0], preferred_element_type=F32)

    @pl.when(j < first_norm_tile)
    def _():
        z_ref[...] = acc.astype(BF16)

    for tile in range(first_norm_tile, len(norm_group) // n_chunks):
        @pl.when(j == tile)
        def _(tile=tile):
            lo_lane = lax.broadcasted_iota(jnp.int32, (1, LANES), 1) < HALF
            for c in range(n_chunks):
                sl = slice(c * LANES, (c + 1) * LANES)
                group = norm_group[tile * n_chunks + c]
                y = acc[:, sl]
                if group == 0:
                    z_ref[:, sl] = y.astype(BF16)
                    continue
                y2 = y * y
                if group == LANES:
                    ss = jnp.sum(y2, axis=-1, keepdims=True)
                else:
                    lo = jnp.sum(jnp.where(lo_lane, y2, 0.0), axis=-1, keepdims=True)
                    hi = jnp.sum(jnp.where(lo_lane, 0.0, y2), axis=-1, keepdims=True)
                    ss = jnp.where(lo_lane, lo, hi)
                r = lax.rsqrt(ss * (1.0 / group) + RMS_EPS) * gain_ref[0, :, sl]
                z_ref[:, sl] = (y * r).astype(BF16)


def _inproj(x2, mod, g, wz, coef, layer, lay, seq, tm=1024):
    t, d = x2.shape
    tm = min(tm, seq)
    tn = lay.chunks_per_tile * LANES
    per_batch = seq // tm
    kern = functools.partial(_inproj_kernel, first_norm_tile=lay.first_norm_tile,
                             n_chunks=lay.chunks_per_tile, norm_group=lay.norm_group)
    return pl.pallas_call(
        kern,
        out_shape=jax.ShapeDtypeStruct((t, lay.nz), BF16),
        grid=(t // tm, lay.nz // tn),
        in_specs=[pl.BlockSpec((tm, d), lambda i, j: (i, 0)),
                  pl.BlockSpec((1, 6, d), lambda i, j: (i // per_batch, 0, 0)),
                  pl.BlockSpec((1, d), lambda i, j: (0, 0)),
                  pl.BlockSpec((1, d, tn), lambda i, j: (layer, 0, j)),
                  pl.BlockSpec((1, 1, tn), lambda i, j: (layer, 0, j))],
        out_specs=pl.BlockSpec((tm, tn), lambda i, j: (i, j)),
        scratch_shapes=[pltpu.VMEM((tm, d), BF16)],
        compiler_params=_params("parallel", "arbitrary"),
        name="inproj",
    )(x2, mod, g, wz, coef)


def _band_kernel(sink_ref, q_ref, k_ref, v_ref, tab_ref, qmask_ref, o_ref, *, n_side, ck, nkc, hp, q_of,
                 nsub):
    g = pl.program_id(1)
    i = pl.program_id(2)
    for u in range(nsub):
        qi = i * nsub + u
        rows = slice(u * ck, (u + 1) * ck)
        chunks = []
        for d in range(-n_side, n_side + 1):
            kc = qi + d
            valid = jnp.logical_and(kc >= 0, kc < nkc)
            start = pl.multiple_of(jnp.clip(kc, 0, nkc - 1) * ck, ck)
            chunks.append((d + n_side, start, jnp.where(valid, 0.0, NEG_INF)))
        for h in range(hp):
            q = q_ref[rows, q_of(h) * LANES:(q_of(h) + 1) * LANES] * qmask_ref[h]
            sink = sink_ref[g * hp + h]
            m = jnp.full((ck, 1), sink, F32)
            scores = []
            for di, start, penalty in chunks:
                s = lax.dot_general(q, k_ref[pl.ds(start, ck), :], _NT, preferred_element_type=F32)
                s = s + tab_ref[h, di] + penalty
                m = jnp.maximum(m, jnp.max(s, axis=-1, keepdims=True))
                scores.append(s)
            den = jnp.exp(sink - m)
            acc = jnp.zeros((ck, LANES), F32)
            for s, (di, start, penalty) in zip(scores, chunks):
                p = jnp.exp(s - m)
                den = den + jnp.sum(p, axis=-1, keepdims=True)
                acc = acc + jnp.dot(p.astype(BF16), v_ref[pl.ds(start, ck), :],
                                    preferred_element_type=F32)
            o_ref[rows, h * LANES:(h + 1) * LANES] = (acc / den).astype(BF16)


def _band_attention(z, table, qmask, sink, *, batch, seq, n_groups, hp, q_width, q_of, qblk, kblk, vblk,
                    n_side, nsub=2):
    t = z.shape[0]
    ck = table.shape[-1]
    nsub = min(nsub, seq // ck)
    tq = nsub * ck
    nq = seq // tq
    kern = functools.partial(_band_kernel, n_side=n_side, ck=ck, nkc=seq // ck, hp=hp, q_of=q_of, nsub=nsub)
    nd = 2 * n_side + 1
    return pl.pallas_call(
        kern,
        out_shape=jax.ShapeDtypeStruct((t, n_groups * hp * LANES), BF16),
        grid=(batch, n_groups, nq),
        in_specs=[pl.BlockSpec(memory_space=pltpu.SMEM),
                  pl.BlockSpec((tq, q_width), lambda b, g, i: (b * nq + i, qblk(g))),
                  pl.BlockSpec((seq, LANES), lambda b, g, i: (b, kblk(g))),
                  pl.BlockSpec((seq, LANES), lambda b, g, i: (b, vblk(g))),
                  pl.BlockSpec((hp, nd, ck, ck), lambda b, g, i: (g, 0, 0, 0)),
                  pl.BlockSpec((hp, 1, LANES), lambda b, g, i: (g, 0, 0))],
        out_specs=pl.BlockSpec((tq, hp * LANES), lambda b, g, i: (b * nq + i, g)),
        compiler_params=_params("parallel", "parallel", "arbitrary"),
        name="band_attention",
    )(sink, z, z, z, table, qmask)


def _alibi_slopes(n):
    return np.array([2.0 ** (-8.0 * (i + 1) / n) for i in range(n)], dtype=np.float32)


def _dilated_table(tq):
    reach = max(w // 2 for w, _ in A_PATTERNS)
    n_side = -(-reach // tq)
    d = np.arange(-n_side, n_side + 1)[:, None, None] * tq
    off = d + np.arange(tq)[None, None, :] - np.arange(tq)[None, :, None]
    mult = np.zeros(off.shape, np.float64)
    for window, dil in A_PATTERNS:
        mult += ((off % dil) == 0) & (np.abs(off) <= window // 2)
    logm = np.where(mult > 0, np.log(np.maximum(mult, 1.0)), 0.0)
    slopes = _alibi_slopes(A_HEADS).astype(np.float64)
    tab = -slopes[:, None, None, None] * np.abs(off)[None] + logm[None]
    tab = np.where(mult[None] > 0, tab, NEG_INF)
    return jnp.asarray(tab.astype(np.float32)), n_side


def _window_table(tq):
    n_side = -(-C_RADIUS // tq)
    d = np.arange(-n_side, n_side + 1)[:, None, None] * tq
    off = d + np.arange(tq)[None, None, :] - np.arange(tq)[None, :, None]
    slopes = _alibi_slopes(C_HEADS).astype(np.float64)
    tab = -slopes[:, None, None, None] * np.abs(off)[None]
    tab = np.where((np.abs(off) <= C_RADIUS)[None], tab, NEG_INF)
    return jnp.asarray(tab.astype(np.float32)), n_side


_D_QROWS = 8
_D_KROWS = 16


def _nbr_kernel(q_ref, k_ref, v_ref, tab_ref, o_ref, *, rows):
    t = pl.program_id(2)
    krow = jnp.clip(t * _D_QROWS - D_WIN_R // 2, 0, rows - _D_KROWS)
    start = pl.multiple_of(krow * GRID_W, GRID_W * (D_WIN_R // 2))
    nk = _D_KROWS * GRID_W
    s = lax.dot_general(q_ref[...], k_ref[pl.ds(start, nk), :], _NT, preferred_element_type=F32)
    s = s + tab_ref[0, 0]
    m = jnp.max(s, axis=-1, keepdims=True)
    p = jnp.exp(s - m)
    den = jnp.sum(p, axis=-1, keepdims=True)
    acc = jnp.dot(p.astype(BF16), v_ref[pl.ds(start, nk), :], preferred_element_type=F32)
    o_ref[...] = (acc / den).astype(BF16)


def _nbr_tables(rpb, rows):
    depth, heads, n_dr, n_dc = rpb.shape
    nt = rows // _D_QROWS
    wr = min(D_WIN_R, rows)
    pad_l = GRID_W - 1 - (D_WIN_C - 1)
    vpad = jnp.pad(rpb.astype(F32), ((0, 0), (0, 0), (0, 0), (pad_l, 2 * GRID_W - 1 - n_dc - pad_l)))
    toep = jnp.stack([vpad[..., GRID_W - 1 - c:2 * GRID_W - 1 - c] for c in range(GRID_W)], axis=-2)
    c = np.arange(GRID_W)
    cs = np.clip(c - D_WIN_C // 2, 0, GRID_W - D_WIN_C)
    col_ok = (c[None, :] >= cs[:, None]) & (c[None, :] < cs[:, None] + D_WIN_C)
    colmat = jnp.where(col_ok, toep, NEG_INF)
    masked = jnp.full((depth, heads, 1, GRID_W, GRID_W), NEG_INF, F32)
    colmat = jnp.concatenate([colmat, masked], axis=2)
    sel = np.full((3, _D_QROWS, _D_KROWS), n_dr, np.int32)
    for kind, t in enumerate((0, 1, nt - 1)):
        krow0 = int(np.clip(t * _D_QROWS - D_WIN_R // 2, 0, rows - _D_KROWS))
        for ri in range(_D_QROWS):
            r = t * _D_QROWS + ri
            rs = int(np.clip(r - wr // 2, 0, rows - wr))
            for kj in range(_D_KROWS):
                kr = krow0 + kj
                if rs <= kr < rs + wr:
                    sel[kind, ri, kj] = kr - r + (D_WIN_R - 1)
    blocks = jnp.take(colmat, jnp.asarray(sel.reshape(-1)), axis=2)
    blocks = blocks.reshape(depth * heads, 3, _D_QROWS, _D_KROWS, GRID_W, GRID_W)
    blocks = blocks.transpose(0, 1, 2, 4, 3, 5)
    return blocks.reshape(depth * heads, 3, _D_QROWS * GRID_W, _D_KROWS * GRID_W)


def _nbr_attention(z, table, layer, *, batch, seq, lay):
    t = z.shape[0]
    rows = seq // GRID_W
    nt = rows // _D_QROWS
    tq = _D_QROWS * GRID_W
    nk = _D_KROWS * GRID_W
    kern = functools.partial(_nbr_kernel, rows=rows)
    kind = lambda i: jnp.where(i == 0, 0, jnp.where(i == nt - 1, 2, 1))
    return pl.pallas_call(
        kern,
        out_shape=jax.ShapeDtypeStruct((t, D_HEADS * LANES), BF16),
        grid=(batch, D_HEADS, nt),
        in_specs=[pl.BlockSpec((tq, LANES), lambda b, h, i: (b * nt + i, lay.QD0 + h)),
                  pl.BlockSpec((seq, LANES), lambda b, h, i: (b, lay.KD0 + h)),
                  pl.BlockSpec((seq, LANES), lambda b, h, i: (b, lay.VD0 + h)),
                  pl.BlockSpec((1, 1, tq, nk), lambda b, h, i: (layer * D_HEADS + h, kind(i), 0, 0))],
        out_specs=pl.BlockSpec((tq, LANES), lambda b, h, i: (b * nt + i, h)),
        compiler_params=_params("parallel", "parallel", "arbitrary"),
        name="nbr_attention",
    )(z, z, z, table)


_N_BIAS_PIECES = 3
_KPOS_RADIX = 64


def _bf16_round(v):
    u = np.asarray(v, np.float32).view(np.uint32)
    u = (u + np.uint32(0x7FFF) + ((u >> np.uint32(16)) & np.uint32(1))) & np.uint32(0xFFFF0000)
    return u.view(np.float32)


def _diff_feature_tables(tk):
    c32 = (_alibi_slopes(B_HEADS).astype(np.float64) * np.log2(np.e)).astype(np.float32)
    qconst = np.zeros((B_HEADS, 2, LANES), np.float32)
    for h in range(B_HEADS):
        rem = np.float32(c32[h])
        pieces = []
        for _ in range(_N_BIAS_PIECES):
            p = np.float32(_bf16_round(rem))
            pieces.append(p)
            rem = np.float32(rem - p)
        for br, base in ((0, HALF), (1, 0)):
            for n, p in enumerate(pieces):
                qconst[h, br, base + n] = -_KPOS_RADIX * p
                qconst[h, br, base + _N_BIAS_PIECES + n] = -p
    j = np.arange(tk)
    kfeat = np.zeros((2, tk, LANES), np.float32)
    for br, base in ((0, HALF), (1, 0)):
        for n in range(_N_BIAS_PIECES):
            kfeat[br, :, base + n] = j // _KPOS_RADIX
            kfeat[br, :, base + _N_BIAS_PIECES + n] = j % _KPOS_RADIX
            kfeat[br, :, base + 2 * _N_BIAS_PIECES + n] = 1.0
    return jnp.asarray(qconst), jnp.asarray(kfeat, BF16), jnp.asarray(c32)


def _diff_kernel(c_ref, misc_ref, q_ref, k_ref, v_ref, qconst_ref, kfeat_ref, dl_ref, subg_ref, o_ref,
                 kaug_scr, m_scr, l_scr, acc_scr, *, tq, tk, nk):
    h = pl.program_id(1)
    i = pl.program_id(2)
    c = c_ref[h]
    lam_init = misc_ref[0]
    lane = lax.broadcasted_iota(jnp.int32, (1, LANES), 1)
    lo_lane = lane < HALF

    @pl.when(i == 0)
    def _():
        def aug(cc, carry):
            sl = pl.ds(pl.multiple_of(cc * tk, tk), tk)
            kb = k_ref[sl, :]
            kaug_scr[0, sl, :] = jnp.where(lo_lane, kb, kfeat_ref[0])
            kaug_scr[1, sl, :] = jnp.where(lo_lane, kfeat_ref[1], kb)
            return carry
        lax.fori_loop(0, nk, aug, 0)

    q = q_ref[...].astype(F32)
    ci = c * lax.broadcasted_iota(jnp.int32, (tq, 1), 0).astype(F32)
    r1 = ci.astype(BF16).astype(F32)
    r2 = (ci - r1).astype(BF16).astype(F32)
    r3 = (ci - r1 - r2).astype(BF16).astype(F32)
    q_plain, q_right, q_left = [], [], []
    for br, base in ((0, HALF), (1, 0)):
        keep = lo_lane if br == 0 else jnp.logical_not(lo_lane)
        f0 = base + 2 * _N_BIAS_PIECES
        feat = qconst_ref[0, br:br + 1, :] + jnp.where(
            lane == f0, r1, jnp.where(lane == f0 + 1, r2, jnp.where(lane == f0 + 2, r3, 0.0)))
        q_plain.append(jnp.where(keep, q, 0.0).astype(BF16))
        q_right.append(jnp.where(keep, q, feat).astype(BF16))
        q_left.append(jnp.where(keep, q, -feat).astype(BF16))

    m_scr[...] = jnp.full(m_scr.shape, NEG_INF, F32)
    l_scr[...] = jnp.zeros(l_scr.shape, F32)
    acc_scr[...] = jnp.zeros(acc_scr.shape, F32)

    def update(idx, s, shift, vb):
        m_old = m_scr[idx]
        m_new = jnp.maximum(m_old, jnp.max(s, axis=-1, keepdims=True) + shift)
        alpha = jnp.exp2(m_old - m_new)
        p = jnp.exp2(s + (shift - m_new))
        l_scr[idx] = alpha * l_scr[idx] + jnp.sum(p, axis=-1, keepdims=True)
        acc_scr[idx] = alpha * acc_scr[idx] + jnp.dot(p.astype(BF16), vb, preferred_element_type=F32)
        m_scr[idx] = m_new

    def chunk(cc, qops, shift_sign):
        sl = pl.ds(pl.multiple_of(cc * tk, tk), tk)
        vb = v_ref[sl, :]
        shift = shift_sign * c * (cc * tk - i * tq).astype(F32)
        ss = [lax.dot_general(qops[idx], kaug_scr[idx, sl, :], _NT, preferred_element_type=F32)
              for idx in range(2)]
        for idx in range(2):
            update(idx, ss[idx], shift, vb)

    cdiag = (i * tq) // tk

    def left_body(cc, carry):
        chunk(cc, q_left, 1.0)
        return carry

    def right_body(cc, carry):
        chunk(cc, q_right, -1.0)
        return carry

    lax.fori_loop(0, cdiag, left_body, 0)

    sl = pl.ds(pl.multiple_of(cdiag * tk, tk), tk)
    dji = (lax.broadcasted_iota(jnp.int32, (tq, tk), 1) - lax.broadcasted_iota(jnp.int32, (tq, tk), 0))
    bias = -c * jnp.abs(dji + (cdiag * tk - i * tq)).astype(F32)
    for idx in range(2):
        s = lax.dot_general(q_plain[idx], k_ref[sl, :], _NT, preferred_element_type=F32) + bias
        update(idx, s, 0.0, v_ref[sl, :])

    lax.fori_loop(cdiag + 1, nk, right_body, 0)

    hs = pl.ds(h, 1)
    lam = (jnp.exp(jnp.sum(dl_ref[0, hs, :] * dl_ref[1, hs, :], axis=-1, keepdims=True))
           - jnp.exp(jnp.sum(dl_ref[2, hs, :] * dl_ref[3, hs, :], axis=-1, keepdims=True)) + lam_init)
    o = acc_scr[0] / l_scr[0] - lam * (acc_scr[1] / l_scr[1])
    o = o * lax.rsqrt(jnp.mean(o * o, axis=-1, keepdims=True) + RMS_EPS) * subg_ref[...]
    o_ref[...] = (o * (1.0 - lam_init)).astype(BF16)


def _diff_attention(z, dl, subg, tables, misc, *, batch, seq, lay, tq, tk):
    t = z.shape[0]
    nq = seq // tq
    qconst, kfeat, cvals = tables
    kern = functools.partial(_diff_kernel, tq=tq, tk=tk, nk=seq // tk)
    return pl.pallas_call(
        kern,
        out_shape=jax.ShapeDtypeStruct((t, B_HEADS * LANES), BF16),
        grid=(batch, B_HEADS, nq),
        in_specs=[pl.BlockSpec(memory_space=pltpu.SMEM),
                  pl.BlockSpec(memory_space=pltpu.SMEM),
                  pl.BlockSpec((tq, LANES), lambda b, h, i: (b * nq + i, lay.QB0 + h)),
                  pl.BlockSpec((seq, LANES), lambda b, h, i: (b, lay.KB0 + h)),
                  pl.BlockSpec((seq, LANES), lambda b, h, i: (b, lay.VB0 + h)),
                  pl.BlockSpec((1, 2, LANES), lambda b, h, i: (h, 0, 0)),
                  pl.BlockSpec(kfeat.shape, lambda b, h, i: (0, 0, 0)),
                  pl.BlockSpec(dl.shape, lambda b, h, i: (0, 0, 0)),
                  pl.BlockSpec((1, LANES), lambda b, h, i: (0, 0))],
        out_specs=pl.BlockSpec((tq, LANES), lambda b, h, i: (b * nq + i, h)),
        scratch_shapes=[pltpu.VMEM((2, seq, LANES), BF16),
                        pltpu.VMEM((2, tq, 1), F32), pltpu.VMEM((2, tq, 1), F32),
                        pltpu.VMEM((2, tq, LANES), F32)],
        compiler_params=_params("parallel", "arbitrary", "arbitrary"),
        name="diff_attention",
    )(cvals, misc, z, z, z, qconst, kfeat, dl, subg)


def _merge_kernel(oa_ref, ob_ref, oc_ref, od_ref, gate_ref, wa_ref, wb_ref, wc_ref, wd_ref, o_ref, *, d):
    acc = None
    for idx, (o, w) in enumerate(((oa_ref, wa_ref), (ob_ref, wb_ref), (oc_ref, wc_ref), (od_ref, wd_ref))):
        y = jnp.dot(o[...], w[...], preferred_element_type=F32)
        g = _sigmoid(gate_ref[:, idx * d:(idx + 1) * d].astype(F32))
        acc = g * y if acc is None else acc + g * y
    o_ref[...] = acc.astype(BF16)


def _merge(oa, ob, oc, od, z, wbs, *, d, tm=256):
    t = oa.shape[0]
    tm = min(tm, t)
    row = lambda a: pl.BlockSpec((tm, a.shape[1]), lambda i: (i, 0))
    full = lambda a: pl.BlockSpec(a.shape, lambda i: (0, 0))
    return pl.pallas_call(
        functools.partial(_merge_kernel, d=d),
        out_shape=jax.ShapeDtypeStruct((t, d), BF16),
        grid=(t // tm,),
        in_specs=[row(oa), row(ob), row(oc), row(od),
                  pl.BlockSpec((tm, N_BRANCH * d), lambda i: (i, 0))] + [full(w) for w in wbs],
        out_specs=pl.BlockSpec((tm, d), lambda i: (i, 0)),
        compiler_params=_params("parallel"),
        name="branch_merge",
    )(oa, ob, oc, od, z, *wbs)


def _outproj_kernel(a_ref, w_ref, x_ref, g_ref, o_ref):
    y = jnp.dot(a_ref[...], w_ref[0], preferred_element_type=F32)
    o_ref[...] = x_ref[...] + g_ref[0] * y


def _outproj(a, w, layer, x2, gate, seq, tm=512):
    t, d = x2.shape
    tm = min(tm, seq)
    per_batch = seq // tm
    return pl.pallas_call(
        _outproj_kernel,
        out_shape=jax.ShapeDtypeStruct((t, d), F32),
        grid=(t // tm,),
        in_specs=[pl.BlockSpec((tm, a.shape[1]), lambda i: (i, 0)),
                  pl.BlockSpec((1,) + w.shape[1:], lambda i: (layer, 0, 0)),
                  pl.BlockSpec((tm, d), lambda i: (i, 0)),
                  pl.BlockSpec((1, 1, d), lambda i: (i // per_batch, 0, 0))],
        out_specs=pl.BlockSpec((tm, d), lambda i: (i, 0)),
        compiler_params=_params("parallel"),
        name="out_proj",
    )(a, w, x2, gate)


def _router_kernel(x_ref, mod_ref, g_ref, rwt_ref, rb_ref, h_ref, e_ref, gw_ref):
    h = _mod_norm(x_ref[...], g_ref[...], mod_ref[0, 3:4, :], mod_ref[0, 4:5, :])
    h_ref[...] = h
    logits = lax.dot_general(rwt_ref[...], h.astype(BF16), _NT, preferred_element_type=F32)
    score = _sigmoid(logits)
    sel = score + rb_ref[...]
    sel_r = [sel[e:e + 1, :] for e in range(N_EXPERTS)]
    sc_r = [score[e:e + 1, :] for e in range(N_EXPERTS)]
    n = EXPERTS_PER_GROUP

    def top2_sum(v):
        best = None
        for a in range(n):
            for b in range(a + 1, n):
                s = v[a] + v[b]
                best = s if best is None else jnp.maximum(best, s)
        return best

    gbest = top2_sum(sel_r[0:n])
    gidx = jnp.zeros(gbest.shape, jnp.int32)
    cv = list(sel_r[0:n])
    cs = list(sc_r[0:n])
    for g in range(1, N_GROUPS):
        gs = top2_sum(sel_r[g * n:(g + 1) * n])
        upd = gs > gbest
        gbest = jnp.where(upd, gs, gbest)
        gidx = jnp.where(upd, g, gidx)
        for j in range(n):
            cv[j] = jnp.where(upd, sel_r[g * n + j], cv[j])
            cs[j] = jnp.where(upd, sc_r[g * n + j], cs[j])
    b1, i1, s1 = cv[0], jnp.zeros(gbest.shape, jnp.int32), cs[0]
    for j in range(1, n):
        upd = cv[j] > b1
        b1 = jnp.where(upd, cv[j], b1)
        i1 = jnp.where(upd, j, i1)
        s1 = jnp.where(upd, cs[j], s1)
    low = -3.0e38
    b2 = jnp.full(gbest.shape, low, F32)
    i2 = jnp.zeros(gbest.shape, jnp.int32)
    s2 = jnp.zeros(gbest.shape, F32)
    for j in range(n):
        cand = jnp.where(i1 == j, low, cv[j])
        upd = cand > b2
        b2 = jnp.where(upd, cand, b2)
        i2 = jnp.where(upd, j, i2)
        s2 = jnp.where(upd, cs[j], s2)
    tot = s1 + s2
    e_ref[0:1, :] = gidx * n + i1
    e_ref[1:2, :] = gidx * n + i2
    gw_ref[0:1, :] = s1 / tot
    gw_ref[1:2, :] = s2 / tot


def _router(x2, mod, g, rwt, rb, seq, tm=512):
    t, d = x2.shape
    tm = min(tm, seq)
    per_batch = seq // tm
    return pl.pallas_call(
        _router_kernel,
        out_shape=(jax.ShapeDtypeStruct((t, d), F32),
                   jax.ShapeDtypeStruct((2, t), jnp.int32),
                   jax.ShapeDtypeStruct((2, t), F32)),
        grid=(t // tm,),
        in_specs=[pl.BlockSpec((tm, d), lambda i: (i, 0)),
                  pl.BlockSpec((1, 6, d), lambda i: (i // per_batch, 0, 0)),
                  pl.BlockSpec((1, d), lambda i: (0, 0)),
                  pl.BlockSpec(rwt.shape, lambda i: (0, 0)),
                  pl.BlockSpec(rb.shape, lambda i: (0, 0))],
        out_specs=(pl.BlockSpec((tm, d), lambda i: (i, 0)),
                   pl.BlockSpec((2, tm), lambda i: (0, i)),
                   pl.BlockSpec((2, tm), lambda i: (0, i))),
        compiler_params=_params("parallel"),
        name="norm_router",
    )(x2, mod, g, rwt, rb)


def _rank_kernel(e_ref, rank_ref, cnt_ref, carry):
    @pl.when(pl.program_id(0) == 0)
    def _():
        carry[...] = jnp.zeros(carry.shape, F32)

    tm = e_ref.shape[1]
    eio = lax.broadcasted_iota(jnp.int32, (N_EXPERTS, tm), 0)
    oh0 = eio == e_ref[0:1, :]
    oh1 = eio == e_ref[1:2, :]
    c = jnp.where(jnp.logical_or(oh0, oh1), 1.0, 0.0)
    before = (lax.broadcasted_iota(jnp.int32, (tm, tm), 0) < lax.broadcasted_iota(jnp.int32, (tm, tm), 1))
    pref = jnp.dot(c.astype(BF16), jnp.where(before, 1.0, 0.0).astype(BF16),
                   preferred_element_type=F32) + carry[...]
    rank_ref[0:1, :] = jnp.sum(jnp.where(oh0, pref, 0.0), axis=0, keepdims=True).astype(jnp.int32)
    rank_ref[1:2, :] = jnp.sum(jnp.where(oh1, pref, 0.0), axis=0, keepdims=True).astype(jnp.int32)
    total = carry[...] + jnp.sum(c, axis=1, keepdims=True)
    carry[...] = total
    cnt_ref[...] = jnp.broadcast_to(total, cnt_ref.shape)


def _rank(eidx, tm=512):
    t = eidx.shape[1]
    tm = min(tm, t)
    return pl.pallas_call(
        _rank_kernel,
        out_shape=(jax.ShapeDtypeStruct((2, t), jnp.int32),
                   jax.ShapeDtypeStruct((N_EXPERTS, LANES), F32)),
        grid=(t // tm,),
        in_specs=[pl.BlockSpec((2, tm), lambda i: (0, i))],
        out_specs=(pl.BlockSpec((2, tm), lambda i: (0, i)),
                   pl.BlockSpec((N_EXPERTS, LANES), lambda i: (0, 0))),
        scratch_shapes=[pltpu.VMEM((N_EXPERTS, 1), F32)],
        compiler_params=_params("arbitrary"),
        name="expert_rank",
    )(eidx)


_DMA_UNROLL = 8


def _row_copy(src_hbm, row, dst, slot, r, sem):
    return pltpu.make_async_copy(src_hbm.at[pl.ds(row, 1), :], dst.at[slot, pl.ds(r, 1), :], sem.at[slot])


def _expert_kernel(te_ref, nu_ref, src_ref, h_hbm, w1_ref, w3_ref, w2_ref, y_ref, xbuf, sem, *, tm):
    j = pl.program_id(0)
    n_used = nu_ref[0]

    def start_tile(tile, slot):
        def issue(r, carry):
            _row_copy(h_hbm, src_ref[tile * tm + r], xbuf, slot, r, sem).start()
            return carry
        lax.fori_loop(0, tm, issue, 0, unroll=_DMA_UNROLL)

    def wait_tile(slot):
        pltpu.make_async_copy(h_hbm.at[pl.ds(0, tm), :], xbuf.at[slot], sem.at[slot]).wait()

    @pl.when(jnp.logical_and(j == 0, n_used > 0))
    def _():
        start_tile(0, 0)

    @pl.when(j + 1 < n_used)
    def _():
        start_tile(j + 1, (j + 1) % 2)

    @pl.when(j < n_used)
    def _():
        slot = j % 2
        wait_tile(slot)
        xs = xbuf[slot].astype(BF16)
        g = jnp.dot(xs, w1_ref[0], preferred_element_type=F32)
        u = jnp.dot(xs, w3_ref[0], preferred_element_type=F32)
        mid = (g * _sigmoid(g) * u).astype(BF16)
        y_ref[...] = jnp.dot(mid, w2_ref[0], preferred_element_type=F32)

    @pl.when(j >= n_used)
    def _():
        y_ref[...] = jnp.zeros(y_ref.shape, F32)


def _experts(tile_e, n_used, src, h, w1, w3, w2, layer, tm):
    p = src.shape[0]
    d = h.shape[1]
    f = w1.shape[-1]
    n_exp = w1.shape[1]
    w1r = w1.reshape((-1,) + w1.shape[2:])
    w3r = w3.reshape((-1,) + w3.shape[2:])
    w2r = w2.reshape((-1,) + w2.shape[2:])
    wmap = lambda j, te, nu, src: (layer * n_exp + te[j], 0, 0)
    return pl.pallas_call(
        functools.partial(_expert_kernel, tm=tm),
        out_shape=jax.ShapeDtypeStruct((p, d), F32),
        grid_spec=pltpu.PrefetchScalarGridSpec(
            num_scalar_prefetch=3,
            grid=(p // tm,),
            in_specs=[pl.BlockSpec(memory_space=pl.ANY),
                      pl.BlockSpec((1, d, f), wmap),
                      pl.BlockSpec((1, d, f), wmap),
                      pl.BlockSpec((1, f, d), wmap)],
            out_specs=pl.BlockSpec((tm, d), lambda j, te, nu, src: (j, 0)),
            scratch_shapes=[pltpu.VMEM((2, tm, d), F32), pltpu.SemaphoreType.DMA((2,))]),
        compiler_params=_params("arbitrary"),
        name="experts",
    )(tile_e, n_used, src, h, w1r, w3r, w2r)


def _combine_kernel(pos_ref, x_ref, g_ref, gw_ref, y_hbm, o_ref, buf0, buf1, sem0, sem1, *, tm, t, n_steps):
    i = pl.program_id(0)

    def start_step(step, slot):
        def issue(r, carry):
            _row_copy(y_hbm, pos_ref[step * tm + r], buf0, slot, r, sem0).start()
            _row_copy(y_hbm, pos_ref[t + step * tm + r], buf1, slot, r, sem1).start()
            return carry
        lax.fori_loop(0, tm, issue, 0, unroll=_DMA_UNROLL)

    def wait_step(slot):
        pltpu.make_async_copy(y_hbm.at[pl.ds(0, tm), :], buf0.at[slot], sem0.at[slot]).wait()
        pltpu.make_async_copy(y_hbm.at[pl.ds(0, tm), :], buf1.at[slot], sem1.at[slot]).wait()

    @pl.when(i == 0)
    def _():
        start_step(0, 0)

    @pl.when(i + 1 < n_steps)
    def _():
        start_step(i + 1, (i + 1) % 2)

    slot = i % 2
    wait_step(slot)
    moe = gw_ref[:, 0:1] * buf0[slot] + gw_ref[:, 1:2] * buf1[slot]
    o_ref[...] = x_ref[...] + g_ref[0] * moe


def _combine(pos, x2, gate, gwt, y, seq, tm=256):
    t, d = x2.shape
    tm = min(tm, seq)
    per_batch = seq // tm
    n_steps = t // tm
    return pl.pallas_call(
        functools.partial(_combine_kernel, tm=tm, t=t, n_steps=n_steps),
        out_shape=jax.ShapeDtypeStruct((t, d), F32),
        grid_spec=pltpu.PrefetchScalarGridSpec(
            num_scalar_prefetch=1,
            grid=(n_steps,),
            in_specs=[pl.BlockSpec((tm, d), lambda i, pos: (i, 0)),
                      pl.BlockSpec((1, 1, d), lambda i, pos: (i // per_batch, 0, 0)),
                      pl.BlockSpec((tm, 2), lambda i, pos: (i, 0)),
                      pl.BlockSpec(memory_space=pl.ANY)],
            out_specs=pl.BlockSpec((tm, d), lambda i, pos: (i, 0)),
            scratch_shapes=[pltpu.VMEM((2, tm, d), F32), pltpu.VMEM((2, tm, d), F32),
                            pltpu.SemaphoreType.DMA((2,)), pltpu.SemaphoreType.DMA((2,))]),
        compiler_params=_params("arbitrary"),
        name="moe_combine",
    )(pos, x2, gate, gwt, y)


def _moe(x2, mod, g2n, rwt, rb, w1, w3, w2, layer, *, batch, seq, tm_e=256):
    t, d = x2.shape
    h, eidx, gw = _router(x2, mod, g2n, rwt, rb, seq)
    rank, cnt = _rank(eidx)
    tm_e = min(tm_e, t)
    counts = cnt[:, 0].astype(jnp.int32)
    gsz = ((counts + tm_e - 1) // tm_e) * tm_e
    ends = jnp.cumsum(gsz)
    offs = ends - gsz
    pos = rank
    for e in range(N_EXPERTS):
        pos = pos + jnp.where(eidx == e, offs[e], 0)
    n_tiles = (2 * t) // tm_e + N_EXPERTS
    p = n_tiles * tm_e
    flat = pos.reshape(-1)
    tok = jnp.broadcast_to(jnp.arange(t, dtype=jnp.int32)[None, :], (2, t)).reshape(-1)
    src = jnp.zeros((p,), jnp.int32).at[flat].set(tok)
    tile_e = jnp.minimum(jnp.searchsorted(ends, jnp.arange(n_tiles, dtype=jnp.int32) * tm_e, side='right'),
                         N_EXPERTS - 1).astype(jnp.int32)
    n_used = (ends[-1:] // tm_e).astype(jnp.int32)
    y = _experts(tile_e, n_used, src, h, w1, w3, w2, layer, tm_e)
    return _combine(flat, x2, mod[:, 5:6, :], gw.T, y, seq)


def kernel(x, c, ada_w, ada_b, norm1_g, norm2_g, w_in, qk_norm_a, qk_norm_b, qk_norm_c, qk_norm_d,
           diff_lambda, diff_subln_g, sink_logits, rpb, w_branch, w_out, router_w, router_b, w1, w3, w2):
    batch, seq, d = x.shape
    depth = ada_w.shape[0]
    t = batch * seq
    lay = _Layout(d)
    bw = lay.bw

    mod_all = _ada_mod(c, ada_w, ada_b)
    wz = _build_wz(w_in, lay)
    coef = jnp.stack([_build_coef(lay, qk_norm_a[l], qk_norm_b[l], qk_norm_c[l], qk_norm_d[l])
                      for l in range(depth)])
    wb16 = w_branch.astype(BF16)
    c_hd = bw // C_HEADS
    grp = C_HEADS // C_KV_HEADS
    wc_rows = []
    for hq in range(C_HEADS):
        blk = jnp.zeros((depth, LANES, d), BF16)
        off = (hq // grp) * c_hd
        wc_rows.append(blk.at[:, off:off + c_hd].set(wb16[:, 2, hq * c_hd:(hq + 1) * c_hd]))
    wc_ext = jnp.concatenate(wc_rows, axis=1)
    wout16 = w_out.astype(BF16)
    w1b, w3b, w2b = w1.astype(BF16), w3.astype(BF16), w2.astype(BF16)
    rwt = router_w.T.astype(BF16)
    rb = router_b.reshape(N_EXPERTS, 1).astype(F32)

    tq = 256
    tab_a, side_a = _dilated_table(tq)
    tab_c, side_c = _window_table(tq)
    ones_mask = jnp.ones((A_HEADS, 1, LANES), BF16)
    lane = np.arange(LANES)
    cmask = np.stack([(lane // HALF) == (hq % 2) for hq in range(C_HEADS)]).astype(np.float32)
    cmask = jnp.asarray(cmask.reshape(C_HEADS, 1, LANES), BF16)
    no_sink = jnp.full((A_HEADS,), NEG_INF, F32)
    tq_b, tk_b = min(512, seq), min(1024, seq)
    tables_b = _diff_feature_tables(tk_b)
    tab_d = _nbr_tables(rpb, seq // GRID_W)

    x2 = x.reshape(t, d)
    for l in range(depth):
        mod = mod_all[l, :batch].reshape(batch, 6, d)
        z = _inproj(x2, mod, norm1_g[l].reshape(1, d), wz, coef, l, lay, seq)
        o_a = _band_attention(z, tab_a, ones_mask, no_sink, batch=batch, seq=seq, n_groups=A_HEADS, hp=1,
                              q_width=LANES, q_of=lambda h: 0,
                              qblk=lambda g: lay.QA0 + g, kblk=lambda g: lay.KA0 + g,
                              vblk=lambda g: lay.VA0 + g, n_side=side_a)
        lam_init = 0.8 - 0.6 * float(np.exp(-0.3 * l))
        o_b = _diff_attention(z, diff_lambda[l], diff_subln_g[l].reshape(1, LANES), tables_b,
                              jnp.full((1,), lam_init, F32), batch=batch, seq=seq, lay=lay,
                              tq=tq_b, tk=tk_b)
        o_c = _band_attention(z, tab_c, cmask, sink_logits[l].astype(F32), batch=batch, seq=seq,
                              n_groups=C_KV_HEADS, hp=grp, q_width=(grp // 2) * LANES,
                              q_of=lambda h: h // 2,
                              qblk=lambda g: lay.QC0 // (grp // 2) + g, kblk=lambda g: lay.KC0 + g,
                              vblk=lambda g: lay.VC0, n_side=side_c)
        o_d = _nbr_attention(z, tab_d, l, batch=batch, seq=seq, lay=lay)
        merged = _merge(o_a, o_b, o_c, o_d, z, (wb16[l, 0], wb16[l, 1], wc_ext[l], wb16[l, 3]), d=d)
        x2 = _outproj(merged, wout16, l, x2, mod[:, 2:3, :], seq)
        x2 = _moe(x2, mod, norm2_g[l].reshape(1, d), rwt, rb, w1b, w3b, w2b, l, batch=batch, seq=seq)
    return x2.reshape(batch, seq, d)
```

```python
import functools

import numpy as np
import jax
import jax.numpy as jnp
from jax import lax
from jax.experimental import pallas as pl
from jax.experimental.pallas import tpu as pltpu

F32 = jnp.float32
BF16 = jnp.bfloat16
LANES = 128
HALF = LANES // 2
RMS_EPS = 1e-6
NEG_INF = -1e30
VMEM_LIMIT = 56 * 1024 * 1024

GRID_W = 64
N_BRANCH = 4
A_HEADS = 4
A_PATTERNS = ((128, 1), (512, 4), (2048, 16))
B_HEADS = 4
C_HEADS = 8
C_KV_HEADS = 2
C_RADIUS = 128
D_HEADS = 4
D_WIN_R = 8
D_WIN_C = 16
N_EXPERTS = 16
N_GROUPS = 4
EXPERTS_PER_GROUP = N_EXPERTS // N_GROUPS

_NT = (((1,), (1,)), ((), ()))


def _params(*sem):
    return pltpu.CompilerParams(dimension_semantics=sem, vmem_limit_bytes=VMEM_LIMIT)


def _sigmoid(v):
    return 1.0 / (1.0 + jnp.exp(-v))


class _Layout:
    def __init__(self, d_model):
        bw = d_model // N_BRANCH
        self.bw = bw
        hb = bw // LANES
        o = {}
        o['qa'], o['ka'], o['va'] = 0, bw, 2 * bw
        o['qb'], o['kb'], o['vb'] = 3 * bw, 4 * bw, 5 * bw
        o['qc'] = 6 * bw
        o['kc'] = 7 * bw
        o['vc'] = 7 * bw + LANES
        o['qd'], o['kd'], o['vd'] = 7 * bw + 2 * LANES, 8 * bw + 2 * LANES, 9 * bw + 2 * LANES
        o['gates'] = 10 * bw + 2 * LANES
        self.orig = o
        gate_blocks = N_BRANCH * d_model // LANES
        self.GATE0 = 0
        b = gate_blocks
        self.QA0, self.KA0, self.VA0 = b, b + hb, b + 2 * hb
        b += 3 * hb
        self.QD0, self.KD0, self.VD0 = b, b + hb, b + 2 * hb
        b += 3 * hb
        self.QB0, self.KB0, self.VB0 = b, b + hb, b + 2 * hb
        b += 3 * hb
        self.QC0 = b
        self.KC0 = b + hb
        self.VC0 = b + hb + 2
        b += hb + 3
        self.n_real = b
        self.chunks_per_tile = 6
        self.n_blocks = -(-b // self.chunks_per_tile) * self.chunks_per_tile
        self.nz = self.n_blocks * LANES
        self.first_norm_tile = gate_blocks // self.chunks_per_tile
        assert self.QC0 % (C_HEADS // C_KV_HEADS // 2) == 0
        norm = [0] * self.n_blocks
        for blk0, nblk, group in ((self.QA0, 2 * hb, bw // A_HEADS), (self.QD0, 2 * hb, bw // D_HEADS),
                                  (self.QB0, 2 * hb, bw // (2 * B_HEADS)),
                                  (self.QC0, hb + 2, bw // C_HEADS)):
            assert group in (LANES, HALF)
            norm[blk0:blk0 + nblk] = [group] * nblk
        self.norm_group = tuple(norm)

    def weight_segments(self):
        o, bw = self.orig, self.bw
        kc0, kc1 = o['kc'], o['kc'] + HALF
        segs = [(o['gates'], N_BRANCH * N_BRANCH * bw),
                (o['qa'], bw), (o['ka'], bw), (o['va'], bw),
                (o['qd'], bw), (o['kd'], bw), (o['vd'], bw),
                (o['qb'], bw), (o['kb'], bw), (o['vb'], bw),
                (o['qc'], bw),
                (kc0, HALF), (kc0, HALF), (kc1, HALF), (kc1, HALF),
                (o['vc'], LANES)]
        pad = self.nz - self.n_real * LANES
        if pad:
            segs.append((None, pad))
        return segs


def _build_wz(w_in, lay):
    parts = []
    for start, width in lay.weight_segments():
        if start is None:
            parts.append(jnp.zeros(w_in.shape[:-1] + (width,), w_in.dtype))
        else:
            parts.append(w_in[..., start:start + width])
    return jnp.concatenate(parts, axis=-1).astype(BF16)


def _build_coef(lay, qk_a, qk_b, qk_c, qk_d):
    hb = lay.bw // LANES
    a_hd = lay.bw // A_HEADS
    b_dk = lay.bw // (2 * B_HEADS)
    c_hd = lay.bw // C_HEADS
    d_hd = lay.bw // D_HEADS
    gain = jnp.ones((lay.nz,), F32)

    def put(gain, blk0, nblk, g, scale):
        lo, hi = blk0 * LANES, (blk0 + nblk) * LANES
        reps = (hi - lo) // g.shape[0]
        return gain.at[lo:hi].set(jnp.tile(g.astype(F32), reps) * scale)

    gain = put(gain, lay.QA0, hb, qk_a[0], a_hd ** -0.5)
    gain = put(gain, lay.KA0, hb, qk_a[1], 1.0)
    gain = put(gain, lay.QD0, hb, qk_d[0], d_hd ** -0.5)
    gain = put(gain, lay.KD0, hb, qk_d[1], 1.0)
    gain = put(gain, lay.QB0, hb, qk_b[0], b_dk ** -0.5 * float(np.log2(np.e)))
    gain = put(gain, lay.KB0, hb, qk_b[1], 1.0)
    gain = put(gain, lay.QC0, hb, qk_c[0], c_hd ** -0.5)
    gain = put(gain, lay.KC0, 2, qk_c[1], 1.0)
    return gain.reshape(1, lay.nz)


def _ada_kernel(c_ref, w_ref, b_ref, o_ref):
    c = c_ref[...]
    cond = (c * _sigmoid(c)).astype(BF16)
    o_ref[0] = jnp.dot(cond, w_ref[0].astype(BF16), preferred_element_type=F32) + b_ref[0]


def _ada_mod(c, ada_w, ada_b, tn=1024):
    depth, d, n = ada_w.shape
    rows = 8
    cp = jnp.zeros((rows, d), F32).at[:c.shape[0]].set(c)
    tn = min(tn, n)
    return pl.pallas_call(
        _ada_kernel,
        out_shape=jax.ShapeDtypeStruct((depth, rows, n), F32),
        grid=(depth, n // tn),
        in_specs=[pl.BlockSpec((rows, d), lambda l, j: (0, 0)),
                  pl.BlockSpec((1, d, tn), lambda l, j: (l, 0, j)),
                  pl.BlockSpec((1, 1, tn), lambda l, j: (l, 0, j))],
        out_specs=pl.BlockSpec((1, rows, tn), lambda l, j: (l, 0, j)),
        compiler_params=_params("parallel", "parallel"),
        name="ada_mod",
    )(cp, ada_w, ada_b.reshape(depth, 1, n))


def _mod_norm(x, g, shift, scale):
    y = x * lax.rsqrt(jnp.mean(x * x, axis=-1, keepdims=True) + RMS_EPS) * g
    return y * (1.0 + scale) + shift


def _inproj_kernel(x_ref, mod_ref, g_ref, w_ref, gain_ref, z_ref, h_scr, *, first_norm_tile, n_chunks,
                   norm_group):
    j = pl.program_id(1)

    @pl.when(j == 0)
    def _():
        h = _mod_norm(x_ref[...], g_ref[...], mod_ref[0, 0:1, :], mod_ref[0, 1:2, :])
        h_scr[...] = h.astype(BF16)

    acc = jnp.dot(h_scr[...], w_ref[0], preferred_element_type=F32)

    @pl.when(j < first_norm_tile)
    def _():
        z_ref[...] = acc.astype(BF16)

    for tile in range(first_norm_tile, len(norm_group) // n_chunks):
        @pl.when(j == tile)
        def _(tile=tile):
            lo_lane = lax.broadcasted_iota(jnp.int32, (1, LANES), 1) < HALF
            for c in range(n_chunks):
                sl = slice(c * LANES, (c + 1) * LANES)
                group = norm_group[tile * n_chunks + c]
                y = acc[:, sl]
                if group == 0:
                    z_ref[:, sl] = y.astype(BF16)
                    continue
                y2 = y * y
                if group == LANES:
                    ss = jnp.sum(y2, axis=-1, keepdims=True)
                else:
                    lo = jnp.sum(jnp.where(lo_lane, y2, 0.0), axis=-1, keepdims=True)
                    hi = jnp.sum(jnp.where(lo_lane, 0.0, y2), axis=-1, keepdims=True)
                    ss = jnp.where(lo_lane, lo, hi)
                r = lax.rsqrt(ss * (1.0 / group) + RMS_EPS) * gain_ref[0, :, sl]
                z_ref[:, sl] = (y * r).astype(BF16)


def _inproj(x2, mod, g, wz, coef, layer, lay, seq, tm=1024):
    t, d = x2.shape
    tm = min(tm, seq)
    tn = lay.chunks_per_tile * LANES
    per_batch = seq // tm
    kern = functools.partial(_inproj_kernel, first_norm_tile=lay.first_norm_tile,
                             n_chunks=lay.chunks_per_tile, norm_group=lay.norm_group)
    return pl.pallas_call(
        kern,
        out_shape=jax.ShapeDtypeStruct((t, lay.nz), BF16),
        grid=(t // tm, lay.nz // tn),
        in_specs=[pl.BlockSpec((tm, d), lambda i, j: (i, 0)),
                  pl.BlockSpec((1, 6, d), lambda i, j: (i // per_batch, 0, 0)),
                  pl.BlockSpec((1, d), lambda i, j: (0, 0)),
                  pl.BlockSpec((1, d, tn), lambda i, j: (layer, 0, j)),
                  pl.BlockSpec((1, 1, tn), lambda i, j: (layer, 0, j))],
        out_specs=pl.BlockSpec((tm, tn), lambda i, j: (i, j)),
        scratch_shapes=[pltpu.VMEM((tm, d), BF16)],
        compiler_params=_params("parallel", "arbitrary"),
        name="inproj",
    )(x2, mod, g, wz, coef)


def _band_kernel(sink_ref, q_ref, k_ref, v_ref, tab_ref, qmask_ref, o_ref, *, n_side, ck, nkc, hp, q_of,
                 nsub):
    g = pl.program_id(1)
    i = pl.program_id(2)
    for u in range(nsub):
        qi = i * nsub + u
        rows = slice(u * ck, (u + 1) * ck)
        chunks = []
        for d in range(-n_side, n_side + 1):
            kc = qi + d
            valid = jnp.logical_and(kc >= 0, kc < nkc)
            start = pl.multiple_of(jnp.clip(kc, 0, nkc - 1) * ck, ck)
            chunks.append((d + n_side, start, jnp.where(valid, 0.0, NEG_INF)))
        for h in range(hp):
            q = q_ref[rows, q_of(h) * LANES:(q_of(h) + 1) * LANES] * qmask_ref[h]
            sink = sink_ref[g * hp + h]
            m = jnp.full((ck, 1), sink, F32)
            scores = []
            for di, start, penalty in chunks:
                s = lax.dot_general(q, k_ref[pl.ds(start, ck), :], _NT, preferred_element_type=F32)
                s = s + tab_ref[h, di] + penalty
                m = jnp.maximum(m, jnp.max(s, axis=-1, keepdims=True))
                scores.append(s)
            den = jnp.exp(sink - m)
            acc = jnp.zeros((ck, LANES), F32)
            for s, (di, start, penalty) in zip(scores, chunks):
                p = jnp.exp(s - m)
                den = den + jnp.sum(p, axis=-1, keepdims=True)
                acc = acc + jnp.dot(p.astype(BF16), v_ref[pl.ds(start, ck), :],
                                    preferred_element_type=F32)
            o_ref[rows, h * LANES:(h + 1) * LANES] = (acc / den).astype(BF16)


def _band_attention(z, table, qmask, sink, *, batch, seq, n_groups, hp, q_width, q_of, qblk, kblk, vblk,
                    n_side, nsub=2):
    t = z.shape[0]
    ck = table.shape[-1]
    nsub = min(nsub, seq // ck)
    tq = nsub * ck
    nq = seq // tq
    kern = functools.partial(_band_kernel, n_side=n_side, ck=ck, nkc=seq // ck, hp=hp, q_of=q_of, nsub=nsub)
    nd = 2 * n_side + 1
    return pl.pallas_call(
        kern,
        out_shape=jax.ShapeDtypeStruct((t, n_groups * hp * LANES), BF16),
        grid=(batch, n_groups, nq),
        in_specs=[pl.BlockSpec(memory_space=pltpu.SMEM),
                  pl.BlockSpec((tq, q_width), lambda b, g, i: (b * nq + i, qblk(g))),
                  pl.BlockSpec((seq, LANES), lambda b, g, i: (b, kblk(g))),
                  pl.BlockSpec((seq, LANES), lambda b, g, i: (b, vblk(g))),
                  pl.BlockSpec((hp, nd, ck, ck), lambda b, g, i: (g, 0, 0, 0)),
                  pl.BlockSpec((hp, 1, LANES), lambda b, g, i: (g, 0, 0))],
        out_specs=pl.BlockSpec((tq, hp * LANES), lambda b, g, i: (b * nq + i, g)),
        compiler_params=_params("parallel", "parallel", "arbitrary"),
        name="band_attention",
    )(sink, z, z, z, table, qmask)


def _alibi_slopes(n):
    return np.array([2.0 ** (-8.0 * (i + 1) / n) for i in range(n)], dtype=np.float32)


def _dilated_table(tq):
    reach = max(w // 2 for w, _ in A_PATTERNS)
    n_side = -(-reach // tq)
    d = np.arange(-n_side, n_side + 1)[:, None, None] * tq
    off = d + np.arange(tq)[None, None, :] - np.arange(tq)[None, :, None]
    mult = np.zeros(off.shape, np.float64)
    for window, dil in A_PATTERNS:
        mult += ((off % dil) == 0) & (np.abs(off) <= window // 2)
    logm = np.where(mult > 0, np.log(np.maximum(mult, 1.0)), 0.0)
    slopes = _alibi_slopes(A_HEADS).astype(np.float64)
    tab = -slopes[:, None, None, None] * np.abs(off)[None] + logm[None]
    tab = np.where(mult[None] > 0, tab, NEG_INF)
    return jnp.asarray(tab.astype(np.float32)), n_side


def _window_table(tq):
    n_side = -(-C_RADIUS // tq)
    d = np.arange(-n_side, n_side + 1)[:, None, None] * tq
    off = d + np.arange(tq)[None, None, :] - np.arange(tq)[None, :, None]
    slopes = _alibi_slopes(C_HEADS).astype(np.float64)
    tab = -slopes[:, None, None, None] * np.abs(off)[None]
    tab = np.where((np.abs(off) <= C_RADIUS)[None], tab, NEG_INF)
    return jnp.asarray(tab.astype(np.float32)), n_side


_D_QROWS = 8
_D_KROWS = 16


def _nbr_kernel(q_ref, k_ref, v_ref, tab_ref, o_ref, *, rows):
    t = pl.program_id(2)
    krow = jnp.clip(t * _D_QROWS - D_WIN_R // 2, 0, rows - _D_KROWS)
    start = pl.multiple_of(krow * GRID_W, GRID_W * (D_WIN_R // 2))
    nk = _D_KROWS * GRID_W
    s = lax.dot_general(q_ref[...], k_ref[pl.ds(start, nk), :], _NT, preferred_element_type=F32)
    s = s + tab_ref[0, 0]
    m = jnp.max(s, axis=-1, keepdims=True)
    p = jnp.exp(s - m)
    den = jnp.sum(p, axis=-1, keepdims=True)
    acc = jnp.dot(p.astype(BF16), v_ref[pl.ds(start, nk), :], preferred_element_type=F32)
    o_ref[...] = (acc / den).astype(BF16)


def _nbr_tables(rpb, rows):
    depth, heads, n_dr, n_dc = rpb.shape
    nt = rows // _D_QROWS
    wr = min(D_WIN_R, rows)
    pad_l = GRID_W - 1 - (D_WIN_C - 1)
    vpad = jnp.pad(rpb.astype(F32), ((0, 0), (0, 0), (0, 0), (pad_l, 2 * GRID_W - 1 - n_dc - pad_l)))
    toep = jnp.stack([vpad[..., GRID_W - 1 - c:2 * GRID_W - 1 - c] for c in range(GRID_W)], axis=-2)
    c = np.arange(GRID_W)
    cs = np.clip(c - D_WIN_C // 2, 0, GRID_W - D_WIN_C)
    col_ok = (c[None, :] >= cs[:, None]) & (c[None, :] < cs[:, None] + D_WIN_C)
    colmat = jnp.where(col_ok, toep, NEG_INF)
    masked = jnp.full((depth, heads, 1, GRID_W, GRID_W), NEG_INF, F32)
    colmat = jnp.concatenate([colmat, masked], axis=2)
    sel = np.full((3, _D_QROWS, _D_KROWS), n_dr, np.int32)
    for kind, t in enumerate((0, 1, nt - 1)):
        krow0 = int(np.clip(t * _D_QROWS - D_WIN_R // 2, 0, rows - _D_KROWS))
        for ri in range(_D_QROWS):
            r = t * _D_QROWS + ri
            rs = int(np.clip(r - wr // 2, 0, rows - wr))
            for kj in range(_D_KROWS):
                kr = krow0 + kj
                if rs <= kr < rs + wr:
                    sel[kind, ri, kj] = kr - r + (D_WIN_R - 1)
    blocks = jnp.take(colmat, jnp.asarray(sel.reshape(-1)), axis=2)
    blocks = blocks.reshape(depth * heads, 3, _D_QROWS, _D_KROWS, GRID_W, GRID_W)
    blocks = blocks.transpose(0, 1, 2, 4, 3, 5)
    return blocks.reshape(depth * heads, 3, _D_QROWS * GRID_W, _D_KROWS * GRID_W)


def _nbr_attention(z, table, layer, *, batch, seq, lay):
    t = z.shape[0]
    rows = seq // GRID_W
    nt = rows // _D_QROWS
    tq = _D_QROWS * GRID_W
    nk = _D_KROWS * GRID_W
    kern = functools.partial(_nbr_kernel, rows=rows)
    kind = lambda i: jnp.where(i == 0, 0, jnp.where(i == nt - 1, 2, 1))
    return pl.pallas_call(
        kern,
        out_shape=jax.ShapeDtypeStruct((t, D_HEADS * LANES), BF16),
        grid=(batch, D_HEADS, nt),
        in_specs=[pl.BlockSpec((tq, LANES), lambda b, h, i: (b * nt + i, lay.QD0 + h)),
                  pl.BlockSpec((seq, LANES), lambda b, h, i: (b, lay.KD0 + h)),
                  pl.BlockSpec((seq, LANES), lambda b, h, i: (b, lay.VD0 + h)),
                  pl.BlockSpec((1, 1, tq, nk), lambda b, h, i: (layer * D_HEADS + h, kind(i), 0, 0))],
        out_specs=pl.BlockSpec((tq, LANES), lambda b, h, i: (b * nt + i, h)),
        compiler_params=_params("parallel", "parallel", "arbitrary"),
        name="nbr_attention",
    )(z, z, z, table)


_N_BIAS_PIECES = 3
_KPOS_RADIX = 64


def _bf16_round(v):
    u = np.asarray(v, np.float32).view(np.uint32)
    u = (u + np.uint32(0x7FFF) + ((u >> np.uint32(16)) & np.uint32(1))) & np.uint32(0xFFFF0000)
    return u.view(np.float32)


def _diff_feature_tables(tk):
    c32 = (_alibi_slopes(B_HEADS).astype(np.float64) * np.log2(np.e)).astype(np.float32)
    qconst = np.zeros((B_HEADS, 2, LANES), np.float32)
    for h in range(B_HEADS):
        rem = np.float32(c32[h])
        pieces = []
        for _ in range(_N_BIAS_PIECES):
            p = np.float32(_bf16_round(rem))
            pieces.append(p)
            rem = np.float32(rem - p)
        for br, base in ((0, HALF), (1, 0)):
            for n, p in enumerate(pieces):
                qconst[h, br, base + n] = -_KPOS_RADIX * p
                qconst[h, br, base + _N_BIAS_PIECES + n] = -p
    j = np.arange(tk)
    kfeat = np.zeros((2, tk, LANES), np.float32)
    for br, base in ((0, HALF), (1, 0)):
        for n in range(_N_BIAS_PIECES):
            kfeat[br, :, base + n] = j // _KPOS_RADIX
            kfeat[br, :, base + _N_BIAS_PIECES + n] = j % _KPOS_RADIX
            kfeat[br, :, base + 2 * _N_BIAS_PIECES + n] = 1.0
    return jnp.asarray(qconst), jnp.asarray(kfeat, BF16), jnp.asarray(c32)


def _diff_kernel(c_ref, misc_ref, q_ref, k_ref, v_ref, qconst_ref, kfeat_ref, dl_ref, subg_ref, o_ref,
                 kaug_scr, *, tq, tk, nk):
    h = pl.program_id(1)
    i = pl.program_id(2)
    c = c_ref[h]
    lam_init = misc_ref[0]
    lane = lax.broadcasted_iota(jnp.int32, (1, LANES), 1)
    lo_lane = lane < HALF

    @pl.when(i == 0)
    def _():
        def aug(cc, carry):
            sl = pl.ds(pl.multiple_of(cc * tk, tk), tk)
            kb = k_ref[sl, :]
            kaug_scr[0, sl, :] = jnp.where(lo_lane, kb, kfeat_ref[0])
            kaug_scr[1, sl, :] = jnp.where(lo_lane, kfeat_ref[1], kb)
            return carry
        lax.fori_loop(0, nk, aug, 0)

    q = q_ref[...].astype(F32)
    ci = c * lax.broadcasted_iota(jnp.int32, (tq, 1), 0).astype(F32)
    r1 = ci.astype(BF16).astype(F32)
    r2 = (ci - r1).astype(BF16).astype(F32)
    r3 = (ci - r1 - r2).astype(BF16).astype(F32)
    q_plain, q_right, q_left = [], [], []
    for br, base in ((0, HALF), (1, 0)):
        keep = lo_lane if br == 0 else jnp.logical_not(lo_lane)
        f0 = base + 2 * _N_BIAS_PIECES
        feat = qconst_ref[0, br:br + 1, :] + jnp.where(
            lane == f0, r1, jnp.where(lane == f0 + 1, r2, jnp.where(lane == f0 + 2, r3, 0.0)))
        q_plain.append(jnp.where(keep, q, 0.0).astype(BF16))
        q_right.append(jnp.where(keep, q, feat).astype(BF16))
        q_left.append(jnp.where(keep, q, -feat).astype(BF16))

    state = [(jnp.full((tq, 1), NEG_INF, F32), jnp.zeros((tq, 1), F32), jnp.zeros((tq, LANES), F32))
             for _ in range(2)]

    def update(st, s, shift, vb):
        m_old, l_old, acc_old = st
        m_new = jnp.maximum(m_old, jnp.max(s, axis=-1, keepdims=True) + shift)
        alpha = jnp.exp2(m_old - m_new)
        p = jnp.exp2(s + (shift - m_new))
        l_new = alpha * l_old + jnp.sum(p, axis=-1, keepdims=True)
        acc_new = alpha * acc_old + jnp.dot(p.astype(BF16), vb, preferred_element_type=F32)
        return m_new, l_new, acc_new

    cdiag = (i * tq) // tk
    for step in range(nk):
        ct = cdiag + step
        on_right = ct < nk
        cc = jnp.where(on_right, ct, ct - nk)
        sl = pl.ds(pl.multiple_of(cc * tk, tk), tk)
        vb = v_ref[sl, :]
        if step == 0:
            dji = (lax.broadcasted_iota(jnp.int32, (tq, tk), 1)
                   - lax.broadcasted_iota(jnp.int32, (tq, tk), 0))
            bias = -c * jnp.abs(dji + (cdiag * tk - i * tq)).astype(F32)
            ss = [lax.dot_general(q_plain[idx], kaug_scr[idx, sl, :], _NT, preferred_element_type=F32) + bias
                  for idx in range(2)]
            shift = 0.0
        else:
            ss = [lax.dot_general(jnp.where(on_right, q_right[idx], q_left[idx]), kaug_scr[idx, sl, :], _NT,
                                  preferred_element_type=F32) for idx in range(2)]
            shift = jnp.where(on_right, -1.0, 1.0) * c * (cc * tk - i * tq).astype(F32)
        state = [update(state[idx], ss[idx], shift, vb) for idx in range(2)]

    hs = pl.ds(h, 1)
    lam = (jnp.exp(jnp.sum(dl_ref[0, hs, :] * dl_ref[1, hs, :], axis=-1, keepdims=True))
           - jnp.exp(jnp.sum(dl_ref[2, hs, :] * dl_ref[3, hs, :], axis=-1, keepdims=True)) + lam_init)
    o = state[0][2] / state[0][1] - lam * (state[1][2] / state[1][1])
    o = o * lax.rsqrt(jnp.mean(o * o, axis=-1, keepdims=True) + RMS_EPS) * subg_ref[...]
    o_ref[...] = (o * (1.0 - lam_init)).astype(BF16)


def _diff_attention(z, dl, subg, tables, misc, *, batch, seq, lay, tq, tk):
    t = z.shape[0]
    nq = seq // tq
    qconst, kfeat, cvals = tables
    kern = functools.partial(_diff_kernel, tq=tq, tk=tk, nk=seq // tk)
    return pl.pallas_call(
        kern,
        out_shape=jax.ShapeDtypeStruct((t, B_HEADS * LANES), BF16),
        grid=(batch, B_HEADS, nq),
        in_specs=[pl.BlockSpec(memory_space=pltpu.SMEM),
                  pl.BlockSpec(memory_space=pltpu.SMEM),
                  pl.BlockSpec((tq, LANES), lambda b, h, i: (b * nq + i, lay.QB0 + h)),
                  pl.BlockSpec((seq, LANES), lambda b, h, i: (b, lay.KB0 + h)),
                  pl.BlockSpec((seq, LANES), lambda b, h, i: (b, lay.VB0 + h)),
                  pl.BlockSpec((1, 2, LANES), lambda b, h, i: (h, 0, 0)),
                  pl.BlockSpec(kfeat.shape, lambda b, h, i: (0, 0, 0)),
                  pl.BlockSpec(dl.shape, lambda b, h, i: (0, 0, 0)),
                  pl.BlockSpec((1, LANES), lambda b, h, i: (0, 0))],
        out_specs=pl.BlockSpec((tq, LANES), lambda b, h, i: (b * nq + i, h)),
        scratch_shapes=[pltpu.VMEM((2, seq, LANES), BF16)],
        compiler_params=_params("parallel", "arbitrary", "arbitrary"),
        name="diff_attention",
    )(cvals, misc, z, z, z, qconst, kfeat, dl, subg)


def _merge_kernel(oa_ref, ob_ref, oc_ref, od_ref, gate_ref, wa_ref, wb_ref, wc_ref, wd_ref, o_ref, *, d):
    acc = None
    for idx, (o, w) in enumerate(((oa_ref, wa_ref), (ob_ref, wb_ref), (oc_ref, wc_ref), (od_ref, wd_ref))):
        y = jnp.dot(o[...], w[...], preferred_element_type=F32)
        g = _sigmoid(gate_ref[:, idx * d:(idx + 1) * d].astype(F32))
        acc = g * y if acc is None else acc + g * y
    o_ref[...] = acc.astype(BF16)


def _merge(oa, ob, oc, od, z, wbs, *, d, tm=256):
    t = oa.shape[0]
    tm = min(tm, t)
    row = lambda a: pl.BlockSpec((tm, a.shape[1]), lambda i: (i, 0))
    full = lambda a: pl.BlockSpec(a.shape, lambda i: (0, 0))
    return pl.pallas_call(
        functools.partial(_merge_kernel, d=d),
        out_shape=jax.ShapeDtypeStruct((t, d), BF16),
        grid=(t // tm,),
        in_specs=[row(oa), row(ob), row(oc), row(od),
                  pl.BlockSpec((tm, N_BRANCH * d), lambda i: (i, 0))] + [full(w) for w in wbs],
        out_specs=pl.BlockSpec((tm, d), lambda i: (i, 0)),
        compiler_params=_params("parallel"),
        name="branch_merge",
    )(oa, ob, oc, od, z, *wbs)


def _outproj_kernel(a_ref, w_ref, x_ref, g_ref, o_ref):
    y = jnp.dot(a_ref[...], w_ref[0], preferred_element_type=F32)
    o_ref[...] = x_ref[...] + g_ref[0] * y


def _outproj(a, w, layer, x2, gate, seq, tm=512):
    t, d = x2.shape
    tm = min(tm, seq)
    per_batch = seq // tm
    return pl.pallas_call(
        _outproj_kernel,
        out_shape=jax.ShapeDtypeStruct((t, d), F32),
        grid=(t // tm,),
        in_specs=[pl.BlockSpec((tm, a.shape[1]), lambda i: (i, 0)),
                  pl.BlockSpec((1,) + w.shape[1:], lambda i: (layer, 0, 0)),
                  pl.BlockSpec((tm, d), lambda i: (i, 0)),
                  pl.BlockSpec((1, 1, d), lambda i: (i // per_batch, 0, 0))],
        out_specs=pl.BlockSpec((tm, d), lambda i: (i, 0)),
        compiler_params=_params("parallel"),
        name="out_proj",
    )(a, w, x2, gate)


def _router_kernel(x_ref, mod_ref, g_ref, rwt_ref, rb_ref, h_ref, e_ref, gw_ref):
    h = _mod_norm(x_ref[...], g_ref[...], mod_ref[0, 3:4, :], mod_ref[0, 4:5, :])
    h_ref[...] = h
    logits = lax.dot_general(rwt_ref[...], h.astype(BF16), _NT, preferred_element_type=F32)
    score = _sigmoid(logits)
    sel = score + rb_ref[...]
    sel_r = [sel[e:e + 1, :] for e in range(N_EXPERTS)]
    sc_r = [score[e:e + 1, :] for e in range(N_EXPERTS)]
    n = EXPERTS_PER_GROUP

    def top2_sum(v):
        best = None
        for a in range(n):
            for b in range(a + 1, n):
                s = v[a] + v[b]
                best = s if best is None else jnp.maximum(best, s)
        return best

    gbest = top2_sum(sel_r[0:n])
    gidx = jnp.zeros(gbest.shape, jnp.int32)
    cv = list(sel_r[0:n])
    cs = list(sc_r[0:n])
    for g in range(1, N_GROUPS):
        gs = top2_sum(sel_r[g * n:(g + 1) * n])
        upd = gs > gbest
        gbest = jnp.where(upd, gs, gbest)
        gidx = jnp.where(upd, g, gidx)
        for j in range(n):
            cv[j] = jnp.where(upd, sel_r[g * n + j], cv[j])
            cs[j] = jnp.where(upd, sc_r[g * n + j], cs[j])
    b1, i1, s1 = cv[0], jnp.zeros(gbest.shape, jnp.int32), cs[0]
    for j in range(1, n):
        upd = cv[j] > b1
        b1 = jnp.where(upd, cv[j], b1)
        i1 = jnp.where(upd, j, i1)
        s1 = jnp.where(upd, cs[j], s1)
    low = -3.0e38
    b2 = jnp.full(gbest.shape, low, F32)
    i2 = jnp.zeros(gbest.shape, jnp.int32)
    s2 = jnp.zeros(gbest.shape, F32)
    for j in range(n):
        cand = jnp.where(i1 == j, low, cv[j])
        upd = cand > b2
        b2 = jnp.where(upd, cand, b2)
        i2 = jnp.where(upd, j, i2)
        s2 = jnp.where(upd, cs[j], s2)
    tot = s1 + s2
    e_ref[0:1, :] = gidx * n + i1
    e_ref[1:2, :] = gidx * n + i2
    gw_ref[0:1, :] = s1 / tot
    gw_ref[1:2, :] = s2 / tot


def _router(x2, mod, g, rwt, rb, seq, tm=512):
    t, d = x2.shape
    tm = min(tm, seq)
    per_batch = seq // tm
    return pl.pallas_call(
        _router_kernel,
        out_shape=(jax.ShapeDtypeStruct((t, d), F32),
                   jax.ShapeDtypeStruct((2, t), jnp.int32),
                   jax.ShapeDtypeStruct((2, t), F32)),
        grid=(t // tm,),
        in_specs=[pl.BlockSpec((tm, d), lambda i: (i, 0)),
                  pl.BlockSpec((1, 6, d), lambda i: (i // per_batch, 0, 0)),
                  pl.BlockSpec((1, d), lambda i: (0, 0)),
                  pl.BlockSpec(rwt.shape, lambda i: (0, 0)),
                  pl.BlockSpec(rb.shape, lambda i: (0, 0))],
        out_specs=(pl.BlockSpec((tm, d), lambda i: (i, 0)),
                   pl.BlockSpec((2, tm), lambda i: (0, i)),
                   pl.BlockSpec((2, tm), lambda i: (0, i))),
        compiler_params=_params("parallel"),
        name="norm_router",
    )(x2, mod, g, rwt, rb)


def _rank_kernel(e_ref, rank_ref, cnt_ref, carry):
    @pl.when(pl.program_id(0) == 0)
    def _():
        carry[...] = jnp.zeros(carry.shape, F32)

    tm = e_ref.shape[1]
    eio = lax.broadcasted_iota(jnp.int32, (N_EXPERTS, tm), 0)
    oh0 = eio == e_ref[0:1, :]
    oh1 = eio == e_ref[1:2, :]
    c = jnp.where(jnp.logical_or(oh0, oh1), 1.0, 0.0)
    before = (lax.broadcasted_iota(jnp.int32, (tm, tm), 0) < lax.broadcasted_iota(jnp.int32, (tm, tm), 1))
    pref = jnp.dot(c.astype(BF16), jnp.where(before, 1.0, 0.0).astype(BF16),
                   preferred_element_type=F32) + carry[...]
    rank_ref[0:1, :] = jnp.sum(jnp.where(oh0, pref, 0.0), axis=0, keepdims=True).astype(jnp.int32)
    rank_ref[1:2, :] = jnp.sum(jnp.where(oh1, pref, 0.0), axis=0, keepdims=True).astype(jnp.int32)
    total = carry[...] + jnp.sum(c, axis=1, keepdims=True)
    carry[...] = total
    cnt_ref[...] = jnp.broadcast_to(total, cnt_ref.shape)


def _rank(eidx, tm=512):
    t = eidx.shape[1]
    tm = min(tm, t)
    return pl.pallas_call(
        _rank_kernel,
        out_shape=(jax.ShapeDtypeStruct((2, t), jnp.int32),
                   jax.ShapeDtypeStruct((N_EXPERTS, LANES), F32)),
        grid=(t // tm,),
        in_specs=[pl.BlockSpec((2, tm), lambda i: (0, i))],
        out_specs=(pl.BlockSpec((2, tm), lambda i: (0, i)),
                   pl.BlockSpec((N_EXPERTS, LANES), lambda i: (0, 0))),
        scratch_shapes=[pltpu.VMEM((N_EXPERTS, 1), F32)],
        compiler_params=_params("arbitrary"),
        name="expert_rank",
    )(eidx)


_DMA_UNROLL = 8


def _row_copy(src_hbm, row, dst, slot, r, sem):
    return pltpu.make_async_copy(src_hbm.at[pl.ds(row, 1), :], dst.at[slot, pl.ds(r, 1), :], sem.at[slot])


def _expert_kernel(te_ref, nu_ref, src_ref, h_hbm, w1_ref, w3_ref, w2_ref, y_ref, xbuf, sem, *, tm):
    j = pl.program_id(0)
    n_used = nu_ref[0]

    def wait_tile(slot):
        pltpu.make_async_copy(h_hbm.at[pl.ds(0, tm), :], xbuf.at[slot], sem.at[slot]).wait()

    @pl.when(jnp.logical_and(j == 0, n_used > 0))
    def _():
        def issue(r, carry):
            _row_copy(h_hbm, src_ref[r], xbuf, 0, r, sem).start()
            return carry
        lax.fori_loop(0, tm, issue, 0, unroll=_DMA_UNROLL)

    @pl.when(j < n_used)
    def _():
        slot = j % 2
        wait_tile(slot)
        nxt = jnp.minimum(j + 1, n_used - 1) * tm
        for r in range(tm):
            _row_copy(h_hbm, src_ref[nxt + r], xbuf, 1 - slot, r, sem).start()
        xs = xbuf[slot].astype(BF16)
        g = jnp.dot(xs, w1_ref[0], preferred_element_type=F32)
        u = jnp.dot(xs, w3_ref[0], preferred_element_type=F32)
        mid = (g * _sigmoid(g) * u).astype(BF16)
        y_ref[...] = jnp.dot(mid, w2_ref[0], preferred_element_type=F32)

    @pl.when(j == n_used - 1)
    def _():
        wait_tile(1 - j % 2)

    @pl.when(j >= n_used)
    def _():
        y_ref[...] = jnp.zeros(y_ref.shape, F32)


def _experts(tile_e, n_used, src, h, w1, w3, w2, layer, tm):
    p = src.shape[0]
    d = h.shape[1]
    f = w1.shape[-1]
    n_exp = w1.shape[1]
    w1r = w1.reshape((-1,) + w1.shape[2:])
    w3r = w3.reshape((-1,) + w3.shape[2:])
    w2r = w2.reshape((-1,) + w2.shape[2:])
    wmap = lambda j, te, nu, src: (layer * n_exp + te[j], 0, 0)
    return pl.pallas_call(
        functools.partial(_expert_kernel, tm=tm),
        out_shape=jax.ShapeDtypeStruct((p, d), F32),
        grid_spec=pltpu.PrefetchScalarGridSpec(
            num_scalar_prefetch=3,
            grid=(p // tm,),
            in_specs=[pl.BlockSpec(memory_space=pl.ANY),
                      pl.BlockSpec((1, d, f), wmap),
                      pl.BlockSpec((1, d, f), wmap),
                      pl.BlockSpec((1, f, d), wmap)],
            out_specs=pl.BlockSpec((tm, d), lambda j, te, nu, src: (j, 0)),
            scratch_shapes=[pltpu.VMEM((2, tm, d), F32), pltpu.SemaphoreType.DMA((2,))]),
        compiler_params=_params("arbitrary"),
        name="experts",
    )(tile_e, n_used, src, h, w1r, w3r, w2r)


def _combine_kernel(pos_ref, x_ref, g_ref, gw_ref, y_hbm, o_ref, buf0, buf1, sem0, sem1, *, tm, t, n_steps):
    i = pl.program_id(0)

    def start_step(step, slot):
        def issue(r, carry):
            _row_copy(y_hbm, pos_ref[step * tm + r], buf0, slot, r, sem0).start()
            _row_copy(y_hbm, pos_ref[t + step * tm + r], buf1, slot, r, sem1).start()
            return carry
        lax.fori_loop(0, tm, issue, 0, unroll=_DMA_UNROLL)

    def wait_step(slot):
        pltpu.make_async_copy(y_hbm.at[pl.ds(0, tm), :], buf0.at[slot], sem0.at[slot]).wait()
        pltpu.make_async_copy(y_hbm.at[pl.ds(0, tm), :], buf1.at[slot], sem1.at[slot]).wait()

    @pl.when(i == 0)
    def _():
        start_step(0, 0)

    @pl.when(i + 1 < n_steps)
    def _():
        start_step(i + 1, (i + 1) % 2)

    slot = i % 2
    wait_step(slot)
    moe = gw_ref[:, 0:1] * buf0[slot] + gw_ref[:, 1:2] * buf1[slot]
    o_ref[...] = x_ref[...] + g_ref[0] * moe


def _combine(pos, x2, gate, gwt, y, seq, tm=256):
    t, d = x2.shape
    tm = min(tm, seq)
    per_batch = seq // tm
    n_steps = t // tm
    return pl.pallas_call(
        functools.partial(_combine_kernel, tm=tm, t=t, n_steps=n_steps),
        out_shape=jax.ShapeDtypeStruct((t, d), F32),
        grid_spec=pltpu.PrefetchScalarGridSpec(
            num_scalar_prefetch=1,
            grid=(n_steps,),
            in_specs=[pl.BlockSpec((tm, d), lambda i, pos: (i, 0)),
                      pl.BlockSpec((1, 1, d), lambda i, pos: (i // per_batch, 0, 0)),
                      pl.BlockSpec((tm, 2), lambda i, pos: (i, 0)),
                      pl.BlockSpec(memory_space=pl.ANY)],
            out_specs=pl.BlockSpec((tm, d), lambda i, pos: (i, 0)),
            scratch_shapes=[pltpu.VMEM((2, tm, d), F32), pltpu.VMEM((2, tm, d), F32),
                            pltpu.SemaphoreType.DMA((2,)), pltpu.SemaphoreType.DMA((2,))]),
        compiler_params=_params("arbitrary"),
        name="moe_combine",
    )(pos, x2, gate, gwt, y)


def _moe(x2, mod, g2n, rwt, rb, w1, w3, w2, layer, *, batch, seq, tm_e=256):
    t, d = x2.shape
    h, eidx, gw = _router(x2, mod, g2n, rwt, rb, seq)
    rank, cnt = _rank(eidx)
    tm_e = min(tm_e, t)
    counts = cnt[:, 0].astype(jnp.int32)
    gsz = ((counts + tm_e - 1) // tm_e) * tm_e
    ends = jnp.cumsum(gsz)
    offs = ends - gsz
    pos = rank
    for e in range(N_EXPERTS):
        pos = pos + jnp.where(eidx == e, offs[e], 0)
    n_tiles = (2 * t) // tm_e + N_EXPERTS
    p = n_tiles * tm_e
    flat = pos.reshape(-1)
    tok = jnp.broadcast_to(jnp.arange(t, dtype=jnp.int32)[None, :], (2, t)).reshape(-1)
    src = jnp.zeros((p,), jnp.int32).at[flat].set(tok)
    tile_e = jnp.minimum(jnp.searchsorted(ends, jnp.arange(n_tiles, dtype=jnp.int32) * tm_e, side='right'),
                         N_EXPERTS - 1).astype(jnp.int32)
    n_used = (ends[-1:] // tm_e).astype(jnp.int32)
    y = _experts(tile_e, n_used, src, h, w1, w3, w2, layer, tm_e)
    return _combine(flat, x2, mod[:, 5:6, :], gw.T, y, seq)


def kernel(x, c, ada_w, ada_b, norm1_g, norm2_g, w_in, qk_norm_a, qk_norm_b, qk_norm_c, qk_norm_d,
           diff_lambda, diff_subln_g, sink_logits, rpb, w_branch, w_out, router_w, router_b, w1, w3, w2):
    batch, seq, d = x.shape
    depth = ada_w.shape[0]
    t = batch * seq
    lay = _Layout(d)
    bw = lay.bw

    mod_all = _ada_mod(c, ada_w, ada_b)
    wz = _build_wz(w_in, lay)
    coef = jnp.stack([_build_coef(lay, qk_norm_a[l], qk_norm_b[l], qk_norm_c[l], qk_norm_d[l])
                      for l in range(depth)])
    wb16 = w_branch.astype(BF16)
    c_hd = bw // C_HEADS
    grp = C_HEADS // C_KV_HEADS
    wc_rows = []
    for hq in range(C_HEADS):
        blk = jnp.zeros((depth, LANES, d), BF16)
        off = (hq // grp) * c_hd
        wc_rows.append(blk.at[:, off:off + c_hd].set(wb16[:, 2, hq * c_hd:(hq + 1) * c_hd]))
    wc_ext = jnp.concatenate(wc_rows, axis=1)
    wout16 = w_out.astype(BF16)
    w1b, w3b, w2b = w1.astype(BF16), w3.astype(BF16), w2.astype(BF16)
    rwt = router_w.T.astype(BF16)
    rb = router_b.reshape(N_EXPERTS, 1).astype(F32)

    tq = 256
    tab_a, side_a = _dilated_table(tq)
    tab_c, side_c = _window_table(tq)
    ones_mask = jnp.ones((A_HEADS, 1, LANES), BF16)
    lane = np.arange(LANES)
    cmask = np.stack([(lane // HALF) == (hq % 2) for hq in range(C_HEADS)]).astype(np.float32)
    cmask = jnp.asarray(cmask.reshape(C_HEADS, 1, LANES), BF16)
    no_sink = jnp.full((A_HEADS,), NEG_INF, F32)
    tq_b, tk_b = min(512, seq), min(1024, seq)
    tables_b = _diff_feature_tables(tk_b)
    tab_d = _nbr_tables(rpb, seq // GRID_W)

    x2 = x.reshape(t, d)
    for l in range(depth):
        mod = mod_all[l, :batch].reshape(batch, 6, d)
        z = _inproj(x2, mod, norm1_g[l].reshape(1, d), wz, coef, l, lay, seq)
        o_a = _band_attention(z, tab_a, ones_mask, no_sink, batch=batch, seq=seq, n_groups=A_HEADS, hp=1,
                              q_width=LANES, q_of=lambda h: 0,
                              qblk=lambda g: lay.QA0 + g, kblk=lambda g: lay.KA0 + g,
                              vblk=lambda g: lay.VA0 + g, n_side=side_a)
        lam_init = 0.8 - 0.6 * float(np.exp(-0.3 * l))
        o_b = _diff_attention(z, diff_lambda[l], diff_subln_g[l].reshape(1, LANES), tables_b,
                              jnp.full((1,), lam_init, F32), batch=batch, seq=seq, lay=lay,
                              tq=tq_b, tk=tk_b)
        o_c = _band_attention(z, tab_c, cmask, sink_logits[l].astype(F32), batch=batch, seq=seq,
                              n_groups=C_KV_HEADS, hp=grp, q_width=(grp // 2) * LANES,
                              q_of=lambda h: h // 2,
                              qblk=lambda g: lay.QC0 // (grp // 2) + g, kblk=lambda g: lay.KC0 + g,
                              vblk=lambda g: lay.VC0, n_side=side_c)
        o_d = _nbr_attention(z, tab_d, l, batch=batch, seq=seq, lay=lay)
        merged = _merge(o_a, o_b, o_c, o_d, z, (wb16[l, 0], wb16[l, 1], wc_ext[l], wb16[l, 3]), d=d)
        x2 = _outproj(merged, wout16, l, x2, mod[:, 2:3, :], seq)
        x2 = _moe(x2, mod, norm2_g[l].reshape(1, d), rwt, rb, w1b, w3b, w2b, l, batch=batch, seq=seq)
    return x2.reshape(batch, seq, d)
```

```python
import functools

import numpy as np
import jax
import jax.numpy as jnp
from jax import lax
from jax.experimental import pallas as pl
from jax.experimental.pallas import tpu as pltpu

F32 = jnp.float32
BF16 = jnp.bfloat16
LANES = 128
HALF = LANES // 2
RMS_EPS = 1e-6
NEG_INF = -1e30
_LOG2E = float(np.log2(np.e))
VMEM_LIMIT = 56 * 1024 * 1024

GRID_W = 64
N_BRANCH = 4
A_HEADS = 4
A_PATTERNS = ((128, 1), (512, 4), (2048, 16))
B_HEADS = 4
C_HEADS = 8
C_KV_HEADS = 2
C_RADIUS = 128
D_HEADS = 4
D_WIN_R = 8
D_WIN_C = 16
N_EXPERTS = 16
N_GROUPS = 4
EXPERTS_PER_GROUP = N_EXPERTS // N_GROUPS

_NT = (((1,), (1,)), ((), ()))


def _params(*sem):
    return pltpu.CompilerParams(dimension_semantics=sem, vmem_limit_bytes=VMEM_LIMIT)


def _sigmoid(v):
    return 1.0 / (1.0 + jnp.exp(-v))


class _Layout:
    def __init__(self, d_model):
        bw = d_model // N_BRANCH
        self.bw = bw
        hb = bw // LANES
        o = {}
        o['qa'], o['ka'], o['va'] = 0, bw, 2 * bw
        o['qb'], o['kb'], o['vb'] = 3 * bw, 4 * bw, 5 * bw
        o['qc'] = 6 * bw
        o['kc'] = 7 * bw
        o['vc'] = 7 * bw + LANES
        o['qd'], o['kd'], o['vd'] = 7 * bw + 2 * LANES, 8 * bw + 2 * LANES, 9 * bw + 2 * LANES
        o['gates'] = 10 * bw + 2 * LANES
        self.orig = o
        gate_blocks = N_BRANCH * d_model // LANES
        self.GATE0 = 0
        b = gate_blocks
        self.QA0, self.KA0, self.VA0 = b, b + hb, b + 2 * hb
        b += 3 * hb
        self.QD0, self.KD0, self.VD0 = b, b + hb, b + 2 * hb
        b += 3 * hb
        self.QB0, self.KB0, self.VB0 = b, b + hb, b + 2 * hb
        b += 3 * hb
        self.QC0 = b
        self.KC0 = b + hb
        self.VC0 = b + hb + 2
        b += hb + 3
        self.n_real = b
        self.chunks_per_tile = 12
        self.n_blocks = -(-b // self.chunks_per_tile) * self.chunks_per_tile
        self.nz = self.n_blocks * LANES
        self.first_norm_tile = gate_blocks // self.chunks_per_tile
        assert self.QC0 % (C_HEADS // C_KV_HEADS // 2) == 0
        norm = [0] * self.n_blocks
        for blk0, nblk, group in ((self.QA0, 2 * hb, bw // A_HEADS), (self.QD0, 2 * hb, bw // D_HEADS),
                                  (self.QB0, 2 * hb, bw // (2 * B_HEADS)),
                                  (self.QC0, hb + 2, bw // C_HEADS)):
            assert group in (LANES, HALF)
            norm[blk0:blk0 + nblk] = [group] * nblk
        self.norm_group = tuple(norm)

    def weight_segments(self):
        o, bw = self.orig, self.bw
        kc0, kc1 = o['kc'], o['kc'] + HALF
        segs = [(o['gates'], N_BRANCH * N_BRANCH * bw),
                (o['qa'], bw), (o['ka'], bw), (o['va'], bw),
                (o['qd'], bw), (o['kd'], bw), (o['vd'], bw),
                (o['qb'], bw), (o['kb'], bw), (o['vb'], bw),
                (o['qc'], bw),
                (kc0, HALF), (kc0, HALF), (kc1, HALF), (kc1, HALF),
                (o['vc'], LANES)]
        pad = self.nz - self.n_real * LANES
        if pad:
            segs.append((None, pad))
        return segs


def _build_wz(w_in, lay):
    parts = []
    for start, width in lay.weight_segments():
        if start is None:
            parts.append(jnp.zeros(w_in.shape[:-1] + (width,), w_in.dtype))
        else:
            parts.append(w_in[..., start:start + width])
    return jnp.concatenate(parts, axis=-1).astype(BF16)


def _build_coef(lay, qk_a, qk_b, qk_c, qk_d):
    hb = lay.bw // LANES
    a_hd = lay.bw // A_HEADS
    b_dk = lay.bw // (2 * B_HEADS)
    c_hd = lay.bw // C_HEADS
    d_hd = lay.bw // D_HEADS
    gain = jnp.ones((lay.nz,), F32)

    def put(gain, blk0, nblk, g, scale):
        lo, hi = blk0 * LANES, (blk0 + nblk) * LANES
        reps = (hi - lo) // g.shape[0]
        return gain.at[lo:hi].set(jnp.tile(g.astype(F32), reps) * scale)

    gain = put(gain, lay.QA0, hb, qk_a[0], a_hd ** -0.5 * _LOG2E)
    gain = put(gain, lay.KA0, hb, qk_a[1], 1.0)
    gain = put(gain, lay.QD0, hb, qk_d[0], d_hd ** -0.5 * _LOG2E)
    gain = put(gain, lay.KD0, hb, qk_d[1], 1.0)
    gain = put(gain, lay.QB0, hb, qk_b[0], b_dk ** -0.5 * _LOG2E)
    gain = put(gain, lay.KB0, hb, qk_b[1], 1.0)
    gain = put(gain, lay.QC0, hb, qk_c[0], c_hd ** -0.5 * _LOG2E)
    gain = put(gain, lay.KC0, 2, qk_c[1], 1.0)
    return gain.reshape(1, lay.nz)


def _ada_kernel(c_ref, w_ref, b_ref, o_ref):
    c = c_ref[...]
    cond = (c * _sigmoid(c)).astype(BF16)
    o_ref[0] = jnp.dot(cond, w_ref[0].astype(BF16), preferred_element_type=F32) + b_ref[0]


def _ada_mod(c, ada_w, ada_b, tn=1024):
    depth, d, n = ada_w.shape
    rows = 8
    cp = jnp.zeros((rows, d), F32).at[:c.shape[0]].set(c)
    tn = min(tn, n)
    return pl.pallas_call(
        _ada_kernel,
        out_shape=jax.ShapeDtypeStruct((depth, rows, n), F32),
        grid=(depth, n // tn),
        in_specs=[pl.BlockSpec((rows, d), lambda l, j: (0, 0)),
                  pl.BlockSpec((1, d, tn), lambda l, j: (l, 0, j)),
                  pl.BlockSpec((1, 1, tn), lambda l, j: (l, 0, j))],
        out_specs=pl.BlockSpec((1, rows, tn), lambda l, j: (l, 0, j)),
        compiler_params=_params("parallel", "parallel"),
        name="ada_mod",
    )(cp, ada_w, ada_b.reshape(depth, 1, n))


def _mod_norm(x, g, shift, scale):
    y = x * lax.rsqrt(jnp.mean(x * x, axis=-1, keepdims=True) + RMS_EPS) * g
    return y * (1.0 + scale) + shift


def _inproj_kernel(x_ref, mod_ref, g_ref, w_ref, gain_ref, z_ref, h_scr, *, first_norm_tile, n_chunks,
                   norm_group):
    j = pl.program_id(1)

    @pl.when(j == 0)
    def _():
        h = _mod_norm(x_ref[...], g_ref[...], mod_ref[0, 0:1, :], mod_ref[0, 1:2, :])
        h_scr[...] = h.astype(BF16)

    acc = jnp.dot(h_scr[...], w_ref[0], preferred_element_type=F32)

    @pl.when(j < first_norm_tile)
    def _():
        z_ref[...] = acc.astype(BF16)

    for tile in range(first_norm_tile, len(norm_group) // n_chunks):
        @pl.when(j == tile)
        def _(tile=tile):
            lo_lane = lax.broadcasted_iota(jnp.int32, (1, LANES), 1) < HALF
            for c in range(n_chunks):
                sl = slice(c * LANES, (c + 1) * LANES)
                group = norm_group[tile * n_chunks + c]
                y = acc[:, sl]
                if group == 0:
                    z_ref[:, sl] = y.astype(BF16)
                    continue
                y2 = y * y
                if group == LANES:
                    ss = jnp.sum(y2, axis=-1, keepdims=True)
                else:
                    lo = jnp.sum(jnp.where(lo_lane, y2, 0.0), axis=-1, keepdims=True)
                    hi = jnp.sum(jnp.where(lo_lane, 0.0, y2), axis=-1, keepdims=True)
                    ss = jnp.where(lo_lane, lo, hi)
                r = lax.rsqrt(ss * (1.0 / group) + RMS_EPS) * gain_ref[0, :, sl]
                z_ref[:, sl] = (y * r).astype(BF16)


def _inproj(x2, mod, g, wz, coef, layer, lay, seq, tm=1024):
    t, d = x2.shape
    tm = min(tm, seq)
    tn = lay.chunks_per_tile * LANES
    per_batch = seq // tm
    kern = functools.partial(_inproj_kernel, first_norm_tile=lay.first_norm_tile,
                             n_chunks=lay.chunks_per_tile, norm_group=lay.norm_group)
    return pl.pallas_call(
        kern,
        out_shape=jax.ShapeDtypeStruct((t, lay.nz), BF16),
        grid=(t // tm, lay.nz // tn),
        in_specs=[pl.BlockSpec((tm, d), lambda i, j: (i, 0)),
                  pl.BlockSpec((1, 6, d), lambda i, j: (i // per_batch, 0, 0)),
                  pl.BlockSpec((1, d), lambda i, j: (0, 0)),
                  pl.BlockSpec((1, d, tn), lambda i, j: (layer, 0, j)),
                  pl.BlockSpec((1, 1, tn), lambda i, j: (layer, 0, j))],
        out_specs=pl.BlockSpec((tm, tn), lambda i, j: (i, j)),
        scratch_shapes=[pltpu.VMEM((tm, d), BF16)],
        compiler_params=_params("parallel", "arbitrary"),
        name="inproj",
    )(x2, mod, g, wz, coef)


def _band_kernel(sink_ref, q_ref, k_ref, v_ref, tab_ref, qmask_ref, o_ref, *, n_side, ck, nkc, hp, q_of,
                 nsub):
    g = pl.program_id(1)
    i = pl.program_id(2)
    for u in range(nsub):
        qi = i * nsub + u
        rows = slice(u * ck, (u + 1) * ck)
        chunks = []
        for d in range(-n_side, n_side + 1):
            kc = qi + d
            valid = jnp.logical_and(kc >= 0, kc < nkc)
            start = pl.multiple_of(jnp.clip(kc, 0, nkc - 1) * ck, ck)
            chunks.append((d + n_side, start, jnp.where(valid, 0.0, NEG_INF)))
        for h in range(hp):
            q = q_ref[rows, q_of(h) * LANES:(q_of(h) + 1) * LANES] * qmask_ref[h]
            sink = sink_ref[g * hp + h]
            m = jnp.full((ck, 1), sink, F32)
            scores = []
            for di, start, penalty in chunks:
                s = lax.dot_general(q, k_ref[pl.ds(start, ck), :], _NT, preferred_element_type=F32)
                s = s + tab_ref[h, di] + penalty
                m = jnp.maximum(m, jnp.max(s, axis=-1, keepdims=True))
                scores.append(s)
            den = jnp.exp2(sink - m)
            acc = jnp.zeros((ck, LANES), F32)
            for s, (di, start, penalty) in zip(scores, chunks):
                p = jnp.exp2(s - m)
                den = den + jnp.sum(p, axis=-1, keepdims=True)
                acc = acc + jnp.dot(p.astype(BF16), v_ref[pl.ds(start, ck), :],
                                    preferred_element_type=F32)
            o_ref[rows, h * LANES:(h + 1) * LANES] = (acc / den).astype(BF16)


def _band_attention(z, table, qmask, sink, *, batch, seq, n_groups, hp, q_width, q_of, qblk, kblk, vblk,
                    n_side, nsub=2):
    t = z.shape[0]
    ck = table.shape[-1]
    nsub = min(nsub, seq // ck)
    tq = nsub * ck
    nq = seq // tq
    kern = functools.partial(_band_kernel, n_side=n_side, ck=ck, nkc=seq // ck, hp=hp, q_of=q_of, nsub=nsub)
    nd = 2 * n_side + 1
    return pl.pallas_call(
        kern,
        out_shape=jax.ShapeDtypeStruct((t, n_groups * hp * LANES), BF16),
        grid=(batch, n_groups, nq),
        in_specs=[pl.BlockSpec(memory_space=pltpu.SMEM),
                  pl.BlockSpec((tq, q_width), lambda b, g, i: (b * nq + i, qblk(g))),
                  pl.BlockSpec((seq, LANES), lambda b, g, i: (b, kblk(g))),
                  pl.BlockSpec((seq, LANES), lambda b, g, i: (b, vblk(g))),
                  pl.BlockSpec((hp, nd, ck, ck), lambda b, g, i: (g, 0, 0, 0)),
                  pl.BlockSpec((hp, 1, LANES), lambda b, g, i: (g, 0, 0))],
        out_specs=pl.BlockSpec((tq, hp * LANES), lambda b, g, i: (b * nq + i, g)),
        compiler_params=_params("parallel", "parallel", "arbitrary"),
        name="band_attention",
    )(sink, z, z, z, table, qmask)


def _alibi_slopes(n):
    return np.array([2.0 ** (-8.0 * (i + 1) / n) for i in range(n)], dtype=np.float32)


def _dilated_table(tq):
    reach = max(w // 2 for w, _ in A_PATTERNS)
    n_side = -(-reach // tq)
    d = np.arange(-n_side, n_side + 1)[:, None, None] * tq
    off = d + np.arange(tq)[None, None, :] - np.arange(tq)[None, :, None]
    mult = np.zeros(off.shape, np.float64)
    for window, dil in A_PATTERNS:
        mult += ((off % dil) == 0) & (np.abs(off) <= window // 2)
    logm = np.where(mult > 0, np.log(np.maximum(mult, 1.0)), 0.0)
    slopes = _alibi_slopes(A_HEADS).astype(np.float64)
    tab = (-slopes[:, None, None, None] * np.abs(off)[None] + logm[None]) * _LOG2E
    tab = np.where(mult[None] > 0, tab, NEG_INF)
    return jnp.asarray(tab.astype(np.float32)), n_side


def _window_table(tq):
    n_side = -(-C_RADIUS // tq)
    d = np.arange(-n_side, n_side + 1)[:, None, None] * tq
    off = d + np.arange(tq)[None, None, :] - np.arange(tq)[None, :, None]
    slopes = _alibi_slopes(C_HEADS).astype(np.float64)
    tab = -slopes[:, None, None, None] * np.abs(off)[None] * _LOG2E
    tab = np.where((np.abs(off) <= C_RADIUS)[None], tab, NEG_INF)
    return jnp.asarray(tab.astype(np.float32)), n_side


_D_QROWS = 8
_D_KROWS = 16


_D_HEADS_PER_STEP = 2


def _nbr_kernel(q_ref, k_ref, v_ref, tab_ref, o_ref, *, rows):
    t = pl.program_id(2)
    krow = jnp.clip(t * _D_QROWS - D_WIN_R // 2, 0, rows - _D_KROWS)
    start = pl.multiple_of(krow * GRID_W, GRID_W * (D_WIN_R // 2))
    nk = _D_KROWS * GRID_W
    for h in range(_D_HEADS_PER_STEP):
        lanes = slice(h * LANES, (h + 1) * LANES)
        s = lax.dot_general(q_ref[:, lanes], k_ref[pl.ds(start, nk), lanes], _NT,
                            preferred_element_type=F32)
        s = s + tab_ref[h, 0]
        m = jnp.max(s, axis=-1, keepdims=True)
        p = jnp.exp2(s - m)
        den = jnp.sum(p, axis=-1, keepdims=True)
        acc = jnp.dot(p.astype(BF16), v_ref[pl.ds(start, nk), lanes], preferred_element_type=F32)
        o_ref[:, lanes] = (acc / den).astype(BF16)


def _nbr_tables(rpb, rows):
    depth, heads, n_dr, n_dc = rpb.shape
    nt = rows // _D_QROWS
    wr = min(D_WIN_R, rows)
    pad_l = GRID_W - 1 - (D_WIN_C - 1)
    vpad = jnp.pad(rpb.astype(F32), ((0, 0), (0, 0), (0, 0), (pad_l, 2 * GRID_W - 1 - n_dc - pad_l)))
    toep = jnp.stack([vpad[..., GRID_W - 1 - c:2 * GRID_W - 1 - c] for c in range(GRID_W)], axis=-2)
    c = np.arange(GRID_W)
    cs = np.clip(c - D_WIN_C // 2, 0, GRID_W - D_WIN_C)
    col_ok = (c[None, :] >= cs[:, None]) & (c[None, :] < cs[:, None] + D_WIN_C)
    colmat = jnp.where(col_ok, toep * _LOG2E, NEG_INF)
    masked = jnp.full((depth, heads, 1, GRID_W, GRID_W), NEG_INF, F32)
    colmat = jnp.concatenate([colmat, masked], axis=2)
    sel = np.full((3, _D_QROWS, _D_KROWS), n_dr, np.int32)
    for kind, t in enumerate((0, 1, nt - 1)):
        krow0 = int(np.clip(t * _D_QROWS - D_WIN_R // 2, 0, rows - _D_KROWS))
        for ri in range(_D_QROWS):
            r = t * _D_QROWS + ri
            rs = int(np.clip(r - wr // 2, 0, rows - wr))
            for kj in range(_D_KROWS):
                kr = krow0 + kj
                if rs <= kr < rs + wr:
                    sel[kind, ri, kj] = kr - r + (D_WIN_R - 1)
    blocks = jnp.take(colmat, jnp.asarray(sel.reshape(-1)), axis=2)
    blocks = blocks.reshape(depth * heads, 3, _D_QROWS, _D_KROWS, GRID_W, GRID_W)
    blocks = blocks.transpose(0, 1, 2, 4, 3, 5)
    return blocks.reshape(depth * heads, 3, _D_QROWS * GRID_W, _D_KROWS * GRID_W)


def _nbr_attention(z, table, layer, *, batch, seq, lay):
    t = z.shape[0]
    rows = seq // GRID_W
    nt = rows // _D_QROWS
    tq = _D_QROWS * GRID_W
    nk = _D_KROWS * GRID_W
    kern = functools.partial(_nbr_kernel, rows=rows)
    hp = _D_HEADS_PER_STEP
    groups = D_HEADS // hp
    width = hp * LANES
    assert lay.QD0 % hp == 0 and lay.KD0 % hp == 0 and lay.VD0 % hp == 0
    kind = lambda i: jnp.where(i == 0, 0, jnp.where(i == nt - 1, 2, 1))
    return pl.pallas_call(
        kern,
        out_shape=jax.ShapeDtypeStruct((t, D_HEADS * LANES), BF16),
        grid=(batch, groups, nt),
        in_specs=[pl.BlockSpec((tq, width), lambda b, g, i: (b * nt + i, lay.QD0 // hp + g)),
                  pl.BlockSpec((seq, width), lambda b, g, i: (b, lay.KD0 // hp + g)),
                  pl.BlockSpec((seq, width), lambda b, g, i: (b, lay.VD0 // hp + g)),
                  pl.BlockSpec((hp, 1, tq, nk), lambda b, g, i: (layer * groups + g, kind(i), 0, 0))],
        out_specs=pl.BlockSpec((tq, width), lambda b, g, i: (b * nt + i, g)),
        compiler_params=_params("parallel", "parallel", "arbitrary"),
        name="nbr_attention",
    )(z, z, z, table)


_N_BIAS_PIECES = 3
_KPOS_RADIX = 64


def _bf16_round(v):
    u = np.asarray(v, np.float32).view(np.uint32)
    u = (u + np.uint32(0x7FFF) + ((u >> np.uint32(16)) & np.uint32(1))) & np.uint32(0xFFFF0000)
    return u.view(np.float32)


def _diff_feature_tables(tk):
    c32 = (_alibi_slopes(B_HEADS).astype(np.float64) * np.log2(np.e)).astype(np.float32)
    qconst = np.zeros((B_HEADS, 2, LANES), np.float32)
    for h in range(B_HEADS):
        rem = np.float32(c32[h])
        pieces = []
        for _ in range(_N_BIAS_PIECES):
            p = np.float32(_bf16_round(rem))
            pieces.append(p)
            rem = np.float32(rem - p)
        for br, base in ((0, HALF), (1, 0)):
            for n, p in enumerate(pieces):
                qconst[h, br, base + n] = -_KPOS_RADIX * p
                qconst[h, br, base + _N_BIAS_PIECES + n] = -p
    j = np.arange(tk)
    kfeat = np.zeros((2, tk, LANES), np.float32)
    for br, base in ((0, HALF), (1, 0)):
        for n in range(_N_BIAS_PIECES):
            kfeat[br, :, base + n] = j // _KPOS_RADIX
            kfeat[br, :, base + _N_BIAS_PIECES + n] = j % _KPOS_RADIX
            kfeat[br, :, base + 2 * _N_BIAS_PIECES + n] = 1.0
    return jnp.asarray(qconst), jnp.asarray(kfeat, BF16), jnp.asarray(c32)


def _diff_kernel(c_ref, misc_ref, q_ref, k_ref, v_ref, qconst_ref, kfeat_ref, dl_ref, subg_ref, o_ref,
                 kaug_scr, *, tq, tk, nk):
    h = pl.program_id(1)
    i = pl.program_id(2)
    c = c_ref[h]
    lam_init = misc_ref[0]
    lane = lax.broadcasted_iota(jnp.int32, (1, LANES), 1)
    lo_lane = lane < HALF

    @pl.when(i == 0)
    def _():
        def aug(cc, carry):
            sl = pl.ds(pl.multiple_of(cc * tk, tk), tk)
            kb = k_ref[sl, :]
            kaug_scr[0, sl, :] = jnp.where(lo_lane, kb, kfeat_ref[0])
            kaug_scr[1, sl, :] = jnp.where(lo_lane, kfeat_ref[1], kb)
            return carry
        lax.fori_loop(0, nk, aug, 0)

    q = q_ref[...].astype(F32)
    ci = c * lax.broadcasted_iota(jnp.int32, (tq, 1), 0).astype(F32)
    r1 = ci.astype(BF16).astype(F32)
    r2 = (ci - r1).astype(BF16).astype(F32)
    r3 = (ci - r1 - r2).astype(BF16).astype(F32)
    q_plain, q_right, q_left = [], [], []
    for br, base in ((0, HALF), (1, 0)):
        keep = lo_lane if br == 0 else jnp.logical_not(lo_lane)
        f0 = base + 2 * _N_BIAS_PIECES
        feat = qconst_ref[0, br:br + 1, :] + jnp.where(
            lane == f0, r1, jnp.where(lane == f0 + 1, r2, jnp.where(lane == f0 + 2, r3, 0.0)))
        q_plain.append(jnp.where(keep, q, 0.0).astype(BF16))
        q_right.append(jnp.where(keep, q, feat).astype(BF16))
        q_left.append(jnp.where(keep, q, -feat).astype(BF16))

    state = [(jnp.full((tq, 1), NEG_INF, F32), jnp.zeros((tq, 1), F32), jnp.zeros((tq, LANES), F32))
             for _ in range(2)]

    def update(st, s, shift, vb):
        m_old, l_old, acc_old = st
        m_new = jnp.maximum(m_old, jnp.max(s, axis=-1, keepdims=True) + shift)
        alpha = jnp.exp2(m_old - m_new)
        p = jnp.exp2(s + (shift - m_new))
        l_new = alpha * l_old + jnp.sum(p, axis=-1, keepdims=True)
        acc_new = alpha * acc_old + jnp.dot(p.astype(BF16), vb, preferred_element_type=F32)
        return m_new, l_new, acc_new

    cdiag = (i * tq) // tk
    for step in range(nk):
        ct = cdiag + step
        on_right = ct < nk
        cc = jnp.where(on_right, ct, ct - nk)
        sl = pl.ds(pl.multiple_of(cc * tk, tk), tk)
        vb = v_ref[sl, :]
        if step == 0:
            dji = (lax.broadcasted_iota(jnp.int32, (tq, tk), 1)
                   - lax.broadcasted_iota(jnp.int32, (tq, tk), 0))
            bias = -c * jnp.abs(dji + (cdiag * tk - i * tq)).astype(F32)
            ss = [lax.dot_general(q_plain[idx], kaug_scr[idx, sl, :], _NT, preferred_element_type=F32) + bias
                  for idx in range(2)]
            shift = 0.0
        else:
            ss = [lax.dot_general(jnp.where(on_right, q_right[idx], q_left[idx]), kaug_scr[idx, sl, :], _NT,
                                  preferred_element_type=F32) for idx in range(2)]
            shift = jnp.where(on_right, -1.0, 1.0) * c * (cc * tk - i * tq).astype(F32)
        state = [update(state[idx], ss[idx], shift, vb) for idx in range(2)]

    hs = pl.ds(h, 1)
    lam = (jnp.exp(jnp.sum(dl_ref[0, hs, :] * dl_ref[1, hs, :], axis=-1, keepdims=True))
           - jnp.exp(jnp.sum(dl_ref[2, hs, :] * dl_ref[3, hs, :], axis=-1, keepdims=True)) + lam_init)
    o = state[0][2] / state[0][1] - lam * (state[1][2] / state[1][1])
    o = o * lax.rsqrt(jnp.mean(o * o, axis=-1, keepdims=True) + RMS_EPS) * subg_ref[...]
    o_ref[...] = (o * (1.0 - lam_init)).astype(BF16)


def _diff_attention(z, dl, subg, tables, misc, *, batch, seq, lay, tq, tk):
    t = z.shape[0]
    nq = seq // tq
    qconst, kfeat, cvals = tables
    kern = functools.partial(_diff_kernel, tq=tq, tk=tk, nk=seq // tk)
    return pl.pallas_call(
        kern,
        out_shape=jax.ShapeDtypeStruct((t, B_HEADS * LANES), BF16),
        grid=(batch, B_HEADS, nq),
        in_specs=[pl.BlockSpec(memory_space=pltpu.SMEM),
                  pl.BlockSpec(memory_space=pltpu.SMEM),
                  pl.BlockSpec((tq, LANES), lambda b, h, i: (b * nq + i, lay.QB0 + h)),
                  pl.BlockSpec((seq, LANES), lambda b, h, i: (b, lay.KB0 + h)),
                  pl.BlockSpec((seq, LANES), lambda b, h, i: (b, lay.VB0 + h)),
                  pl.BlockSpec((1, 2, LANES), lambda b, h, i: (h, 0, 0)),
                  pl.BlockSpec(kfeat.shape, lambda b, h, i: (0, 0, 0)),
                  pl.BlockSpec(dl.shape, lambda b, h, i: (0, 0, 0)),
                  pl.BlockSpec((1, LANES), lambda b, h, i: (0, 0))],
        out_specs=pl.BlockSpec((tq, LANES), lambda b, h, i: (b * nq + i, h)),
        scratch_shapes=[pltpu.VMEM((2, seq, LANES), BF16)],
        compiler_params=_params("parallel", "arbitrary", "arbitrary"),
        name="diff_attention",
    )(cvals, misc, z, z, z, qconst, kfeat, dl, subg)


def _merge_kernel(oa_ref, ob_ref, oc_ref, od_ref, gate_ref, wa_ref, wb_ref, wc_ref, wd_ref, o_ref, *, d):
    acc = None
    for idx, (o, w) in enumerate(((oa_ref, wa_ref), (ob_ref, wb_ref), (oc_ref, wc_ref), (od_ref, wd_ref))):
        y = jnp.dot(o[...], w[...], preferred_element_type=F32)
        g = _sigmoid(gate_ref[:, idx * d:(idx + 1) * d].astype(F32))
        acc = g * y if acc is None else acc + g * y
    o_ref[...] = acc.astype(BF16)


def _merge(oa, ob, oc, od, z, wbs, *, d, tm=256):
    t = oa.shape[0]
    tm = min(tm, t)
    row = lambda a: pl.BlockSpec((tm, a.shape[1]), lambda i: (i, 0))
    full = lambda a: pl.BlockSpec(a.shape, lambda i: (0, 0))
    return pl.pallas_call(
        functools.partial(_merge_kernel, d=d),
        out_shape=jax.ShapeDtypeStruct((t, d), BF16),
        grid=(t // tm,),
        in_specs=[row(oa), row(ob), row(oc), row(od),
                  pl.BlockSpec((tm, N_BRANCH * d), lambda i: (i, 0))] + [full(w) for w in wbs],
        out_specs=pl.BlockSpec((tm, d), lambda i: (i, 0)),
        compiler_params=_params("parallel"),
        name="branch_merge",
    )(oa, ob, oc, od, z, *wbs)


def _outproj_kernel(a_ref, w_ref, x_ref, g_ref, o_ref):
    y = jnp.dot(a_ref[...], w_ref[0], preferred_element_type=F32)
    o_ref[...] = x_ref[...] + g_ref[0] * y


def _outproj(a, w, layer, x2, gate, seq, tm=512):
    t, d = x2.shape
    tm = min(tm, seq)
    per_batch = seq // tm
    return pl.pallas_call(
        _outproj_kernel,
        out_shape=jax.ShapeDtypeStruct((t, d), F32),
        grid=(t // tm,),
        in_specs=[pl.BlockSpec((tm, a.shape[1]), lambda i: (i, 0)),
                  pl.BlockSpec((1,) + w.shape[1:], lambda i: (layer, 0, 0)),
                  pl.BlockSpec((tm, d), lambda i: (i, 0)),
                  pl.BlockSpec((1, 1, d), lambda i: (i // per_batch, 0, 0))],
        out_specs=pl.BlockSpec((tm, d), lambda i: (i, 0)),
        compiler_params=_params("parallel"),
        name="out_proj",
    )(a, w, x2, gate)


def _router_kernel(x_ref, mod_ref, g_ref, rwt_ref, rb_ref, h_ref, e_ref, gw_ref):
    h = _mod_norm(x_ref[...], g_ref[...], mod_ref[0, 3:4, :], mod_ref[0, 4:5, :])
    h_ref[...] = h
    logits = lax.dot_general(rwt_ref[...], h.astype(BF16), _NT, preferred_element_type=F32)
    score = _sigmoid(logits)
    sel = score + rb_ref[...]
    sel_r = [sel[e:e + 1, :] for e in range(N_EXPERTS)]
    sc_r = [score[e:e + 1, :] for e in range(N_EXPERTS)]
    n = EXPERTS_PER_GROUP

    def top2_sum(v):
        best = None
        for a in range(n):
            for b in range(a + 1, n):
                s = v[a] + v[b]
                best = s if best is None else jnp.maximum(best, s)
        return best

    gbest = top2_sum(sel_r[0:n])
    gidx = jnp.zeros(gbest.shape, jnp.int32)
    cv = list(sel_r[0:n])
    cs = list(sc_r[0:n])
    for g in range(1, N_GROUPS):
        gs = top2_sum(sel_r[g * n:(g + 1) * n])
        upd = gs > gbest
        gbest = jnp.where(upd, gs, gbest)
        gidx = jnp.where(upd, g, gidx)
        for j in range(n):
            cv[j] = jnp.where(upd, sel_r[g * n + j], cv[j])
            cs[j] = jnp.where(upd, sc_r[g * n + j], cs[j])
    b1, i1, s1 = cv[0], jnp.zeros(gbest.shape, jnp.int32), cs[0]
    for j in range(1, n):
        upd = cv[j] > b1
        b1 = jnp.where(upd, cv[j], b1)
        i1 = jnp.where(upd, j, i1)
        s1 = jnp.where(upd, cs[j], s1)
    low = -3.0e38
    b2 = jnp.full(gbest.shape, low, F32)
    i2 = jnp.zeros(gbest.shape, jnp.int32)
    s2 = jnp.zeros(gbest.shape, F32)
    for j in range(n):
        cand = jnp.where(i1 == j, low, cv[j])
        upd = cand > b2
        b2 = jnp.where(upd, cand, b2)
        i2 = jnp.where(upd, j, i2)
        s2 = jnp.where(upd, cs[j], s2)
    tot = s1 + s2
    e_ref[0:1, :] = gidx * n + i1
    e_ref[1:2, :] = gidx * n + i2
    gw_ref[0:1, :] = s1 / tot
    gw_ref[1:2, :] = s2 / tot


def _router(x2, mod, g, rwt, rb, seq, tm=512):
    t, d = x2.shape
    tm = min(tm, seq)
    per_batch = seq // tm
    return pl.pallas_call(
        _router_kernel,
        out_shape=(jax.ShapeDtypeStruct((t, d), F32),
                   jax.ShapeDtypeStruct((2, t), jnp.int32),
                   jax.ShapeDtypeStruct((2, t), F32)),
        grid=(t // tm,),
        in_specs=[pl.BlockSpec((tm, d), lambda i: (i, 0)),
                  pl.BlockSpec((1, 6, d), lambda i: (i // per_batch, 0, 0)),
                  pl.BlockSpec((1, d), lambda i: (0, 0)),
                  pl.BlockSpec(rwt.shape, lambda i: (0, 0)),
                  pl.BlockSpec(rb.shape, lambda i: (0, 0))],
        out_specs=(pl.BlockSpec((tm, d), lambda i: (i, 0)),
                   pl.BlockSpec((2, tm), lambda i: (0, i)),
                   pl.BlockSpec((2, tm), lambda i: (0, i))),
        compiler_params=_params("parallel"),
        name="norm_router",
    )(x2, mod, g, rwt, rb)


def _rank_kernel(e_ref, rank_ref, cnt_ref, carry):
    @pl.when(pl.program_id(0) == 0)
    def _():
        carry[...] = jnp.zeros(carry.shape, F32)

    tm = e_ref.shape[1]
    eio = lax.broadcasted_iota(jnp.int32, (N_EXPERTS, tm), 0)
    oh0 = eio == e_ref[0:1, :]
    oh1 = eio == e_ref[1:2, :]
    c = jnp.where(jnp.logical_or(oh0, oh1), 1.0, 0.0)
    before = (lax.broadcasted_iota(jnp.int32, (tm, tm), 0) < lax.broadcasted_iota(jnp.int32, (tm, tm), 1))
    pref = jnp.dot(c.astype(BF16), jnp.where(before, 1.0, 0.0).astype(BF16),
                   preferred_element_type=F32) + carry[...]
    rank_ref[0:1, :] = jnp.sum(jnp.where(oh0, pref, 0.0), axis=0, keepdims=True).astype(jnp.int32)
    rank_ref[1:2, :] = jnp.sum(jnp.where(oh1, pref, 0.0), axis=0, keepdims=True).astype(jnp.int32)
    total = carry[...] + jnp.sum(c, axis=1, keepdims=True)
    carry[...] = total
    cnt_ref[...] = jnp.broadcast_to(total, cnt_ref.shape)


def _rank(eidx, tm=512):
    t = eidx.shape[1]
    tm = min(tm, t)
    return pl.pallas_call(
        _rank_kernel,
        out_shape=(jax.ShapeDtypeStruct((2, t), jnp.int32),
                   jax.ShapeDtypeStruct((N_EXPERTS, LANES), F32)),
        grid=(t // tm,),
        in_specs=[pl.BlockSpec((2, tm), lambda i: (0, i))],
        out_specs=(pl.BlockSpec((2, tm), lambda i: (0, i)),
                   pl.BlockSpec((N_EXPERTS, LANES), lambda i: (0, 0))),
        scratch_shapes=[pltpu.VMEM((N_EXPERTS, 1), F32)],
        compiler_params=_params("arbitrary"),
        name="expert_rank",
    )(eidx)


_DMA_UNROLL = 8


def _row_copy(src_hbm, row, dst, slot, r, sem):
    return pltpu.make_async_copy(src_hbm.at[pl.ds(row, 1), :], dst.at[slot, pl.ds(r, 1), :], sem.at[slot])


def _expert_kernel(te_ref, nu_ref, src_ref, h_hbm, w1_ref, w3_ref, w2_ref, y_ref, xbuf, sem, *, tm):
    j = pl.program_id(0)
    n_used = nu_ref[0]

    def wait_tile(slot):
        pltpu.make_async_copy(h_hbm.at[pl.ds(0, tm), :], xbuf.at[slot], sem.at[slot]).wait()

    @pl.when(jnp.logical_and(j == 0, n_used > 0))
    def _():
        def issue(r, carry):
            _row_copy(h_hbm, src_ref[r], xbuf, 0, r, sem).start()
            return carry
        lax.fori_loop(0, tm, issue, 0, unroll=_DMA_UNROLL)

    @pl.when(j < n_used)
    def _():
        slot = j % 2
        wait_tile(slot)
        nxt = jnp.minimum(j + 1, n_used - 1) * tm
        for r in range(tm):
            _row_copy(h_hbm, src_ref[nxt + r], xbuf, 1 - slot, r, sem).start()
        xs = xbuf[slot].astype(BF16)
        g = jnp.dot(xs, w1_ref[0], preferred_element_type=F32)
        u = jnp.dot(xs, w3_ref[0], preferred_element_type=F32)
        mid = (g * _sigmoid(g) * u).astype(BF16)
        y_ref[...] = jnp.dot(mid, w2_ref[0], preferred_element_type=F32)

    @pl.when(j == n_used - 1)
    def _():
        wait_tile(1 - j % 2)

    @pl.when(j >= n_used)
    def _():
        y_ref[...] = jnp.zeros(y_ref.shape, F32)


def _experts(tile_e, n_used, src, h, w1, w3, w2, layer, tm):
    p = src.shape[0]
    d = h.shape[1]
    f = w1.shape[-1]
    n_exp = w1.shape[1]
    w1r = w1.reshape((-1,) + w1.shape[2:])
    w3r = w3.reshape((-1,) + w3.shape[2:])
    w2r = w2.reshape((-1,) + w2.shape[2:])
    wmap = lambda j, te, nu, src: (layer * n_exp + te[j], 0, 0)
    return pl.pallas_call(
        functools.partial(_expert_kernel, tm=tm),
        out_shape=jax.ShapeDtypeStruct((p, d), F32),
        grid_spec=pltpu.PrefetchScalarGridSpec(
            num_scalar_prefetch=3,
            grid=(p // tm,),
            in_specs=[pl.BlockSpec(memory_space=pl.ANY),
                      pl.BlockSpec((1, d, f), wmap),
                      pl.BlockSpec((1, d, f), wmap),
                      pl.BlockSpec((1, f, d), wmap)],
            out_specs=pl.BlockSpec((tm, d), lambda j, te, nu, src: (j, 0)),
            scratch_shapes=[pltpu.VMEM((2, tm, d), F32), pltpu.SemaphoreType.DMA((2,))]),
        compiler_params=_params("arbitrary"),
        name="experts",
    )(tile_e, n_used, src, h, w1r, w3r, w2r)


def _combine_kernel(pos_ref, x_ref, g_ref, gw_ref, y_hbm, o_ref, buf0, buf1, sem0, sem1, *, tm, t, n_steps):
    i = pl.program_id(0)

    def start_step(step, slot):
        def issue(r, carry):
            _row_copy(y_hbm, pos_ref[step * tm + r], buf0, slot, r, sem0).start()
            _row_copy(y_hbm, pos_ref[t + step * tm + r], buf1, slot, r, sem1).start()
            return carry
        lax.fori_loop(0, tm, issue, 0, unroll=_DMA_UNROLL)

    def wait_step(slot):
        pltpu.make_async_copy(y_hbm.at[pl.ds(0, tm), :], buf0.at[slot], sem0.at[slot]).wait()
        pltpu.make_async_copy(y_hbm.at[pl.ds(0, tm), :], buf1.at[slot], sem1.at[slot]).wait()

    @pl.when(i == 0)
    def _():
        start_step(0, 0)

    @pl.when(i + 1 < n_steps)
    def _():
        start_step(i + 1, (i + 1) % 2)

    slot = i % 2
    wait_step(slot)
    moe = gw_ref[:, 0:1] * buf0[slot] + gw_ref[:, 1:2] * buf1[slot]
    o_ref[...] = x_ref[...] + g_ref[0] * moe


def _combine(pos, x2, gate, gwt, y, seq, tm=256):
    t, d = x2.shape
    tm = min(tm, seq)
    per_batch = seq // tm
    n_steps = t // tm
    return pl.pallas_call(
        functools.partial(_combine_kernel, tm=tm, t=t, n_steps=n_steps),
        out_shape=jax.ShapeDtypeStruct((t, d), F32),
        grid_spec=pltpu.PrefetchScalarGridSpec(
            num_scalar_prefetch=1,
            grid=(n_steps,),
            in_specs=[pl.BlockSpec((tm, d), lambda i, pos: (i, 0)),
                      pl.BlockSpec((1, 1, d), lambda i, pos: (i // per_batch, 0, 0)),
                      pl.BlockSpec((tm, 2), lambda i, pos: (i, 0)),
                      pl.BlockSpec(memory_space=pl.ANY)],
            out_specs=pl.BlockSpec((tm, d), lambda i, pos: (i, 0)),
            scratch_shapes=[pltpu.VMEM((2, tm, d), F32), pltpu.VMEM((2, tm, d), F32),
                            pltpu.SemaphoreType.DMA((2,)), pltpu.SemaphoreType.DMA((2,))]),
        compiler_params=_params("arbitrary"),
        name="moe_combine",
    )(pos, x2, gate, gwt, y)


def _moe(x2, mod, g2n, rwt, rb, w1, w3, w2, layer, *, batch, seq, tm_e=256):
    t, d = x2.shape
    h, eidx, gw = _router(x2, mod, g2n, rwt, rb, seq)
    rank, cnt = _rank(eidx)
    tm_e = min(tm_e, t)
    counts = cnt[:, 0].astype(jnp.int32)
    gsz = ((counts + tm_e - 1) // tm_e) * tm_e
    ends = jnp.cumsum(gsz)
    offs = ends - gsz
    pos = rank
    for e in range(N_EXPERTS):
        pos = pos + jnp.where(eidx == e, offs[e], 0)
    n_tiles = (2 * t) // tm_e + N_EXPERTS
    p = n_tiles * tm_e
    flat = pos.reshape(-1)
    tok = jnp.broadcast_to(jnp.arange(t, dtype=jnp.int32)[None, :], (2, t)).reshape(-1)
    src = jnp.zeros((p,), jnp.int32).at[flat].set(tok)
    tile_start = jnp.arange(n_tiles, dtype=jnp.int32) * tm_e
    tile_e = jnp.minimum(jnp.sum(tile_start[:, None] >= ends[None, :], axis=1),
                         N_EXPERTS - 1).astype(jnp.int32)
    n_used = (ends[-1:] // tm_e).astype(jnp.int32)
    y = _experts(tile_e, n_used, src, h, w1, w3, w2, layer, tm_e)
    return _combine(flat, x2, mod[:, 5:6, :], gw.T, y, seq)


def kernel(x, c, ada_w, ada_b, norm1_g, norm2_g, w_in, qk_norm_a, qk_norm_b, qk_norm_c, qk_norm_d,
           diff_lambda, diff_subln_g, sink_logits, rpb, w_branch, w_out, router_w, router_b, w1, w3, w2):
    batch, seq, d = x.shape
    depth = ada_w.shape[0]
    t = batch * seq
    lay = _Layout(d)
    bw = lay.bw

    mod_all = _ada_mod(c, ada_w, ada_b)
    wz = _build_wz(w_in, lay)
    coef = jnp.stack([_build_coef(lay, qk_norm_a[l], qk_norm_b[l], qk_norm_c[l], qk_norm_d[l])
                      for l in range(depth)])
    wb16 = w_branch.astype(BF16)
    c_hd = bw // C_HEADS
    grp = C_HEADS // C_KV_HEADS
    wc_rows = []
    for hq in range(C_HEADS):
        blk = jnp.zeros((depth, LANES, d), BF16)
        off = (hq // grp) * c_hd
        wc_rows.append(blk.at[:, off:off + c_hd].set(wb16[:, 2, hq * c_hd:(hq + 1) * c_hd]))
    wc_ext = jnp.concatenate(wc_rows, axis=1)
    wout16 = w_out.astype(BF16)
    w1b, w3b, w2b = w1.astype(BF16), w3.astype(BF16), w2.astype(BF16)
    rwt = router_w.T.astype(BF16)
    rb = router_b.reshape(N_EXPERTS, 1).astype(F32)

    tq = 256
    tab_a, side_a = _dilated_table(tq)
    tab_c, side_c = _window_table(tq)
    ones_mask = jnp.ones((A_HEADS, 1, LANES), BF16)
    lane = np.arange(LANES)
    cmask = np.stack([(lane // HALF) == (hq % 2) for hq in range(C_HEADS)]).astype(np.float32)
    cmask = jnp.asarray(cmask.reshape(C_HEADS, 1, LANES), BF16)
    no_sink = jnp.full((A_HEADS,), NEG_INF, F32)
    tq_b, tk_b = min(512, seq), min(1024, seq)
    tables_b = _diff_feature_tables(tk_b)
    tab_d = _nbr_tables(rpb, seq // GRID_W)

    x2 = x.reshape(t, d)
    for l in range(depth):
        mod = mod_all[l, :batch].reshape(batch, 6, d)
        z = _inproj(x2, mod, norm1_g[l].reshape(1, d), wz, coef, l, lay, seq)
        o_a = _band_attention(z, tab_a, ones_mask, no_sink, batch=batch, seq=seq, n_groups=A_HEADS, hp=1,
                              q_width=LANES, q_of=lambda h: 0,
                              qblk=lambda g: lay.QA0 + g, kblk=lambda g: lay.KA0 + g,
                              vblk=lambda g: lay.VA0 + g, n_side=side_a)
        lam_init = 0.8 - 0.6 * float(np.exp(-0.3 * l))
        o_b = _diff_attention(z, diff_lambda[l], diff_subln_g[l].reshape(1, LANES), tables_b,
                              jnp.full((1,), lam_init, F32), batch=batch, seq=seq, lay=lay,
                              tq=tq_b, tk=tk_b)
        o_c = _band_attention(z, tab_c, cmask, sink_logits[l].astype(F32) * _LOG2E, batch=batch, seq=seq,
                              n_groups=C_KV_HEADS, hp=grp, q_width=(grp // 2) * LANES,
                              q_of=lambda h: h // 2,
                              qblk=lambda g: lay.QC0 // (grp // 2) + g, kblk=lambda g: lay.KC0 + g,
                              vblk=lambda g: lay.VC0, n_side=side_c)
        o_d = _nbr_attention(z, tab_d, l, batch=batch, seq=seq, lay=lay)
        merged = _merge(o_a, o_b, o_c, o_d, z, (wb16[l, 0], wb16[l, 1], wc_ext[l], wb16[l, 3]), d=d)
        x2 = _outproj(merged, wout16, l, x2, mod[:, 2:3, :], seq)
        x2 = _moe(x2, mod, norm2_g[l].reshape(1, d), rwt, rb, w1b, w3b, w2b, l, batch=batch, seq=seq)
    return x2.reshape(batch, seq, d)
```

```python
import functools

import numpy as np
import jax
import jax.numpy as jnp
from jax import lax
from jax.experimental import pallas as pl
from jax.experimental.pallas import tpu as pltpu

F32 = jnp.float32
BF16 = jnp.bfloat16
LANES = 128
HALF = LANES // 2
RMS_EPS = 1e-6
NEG_INF = -1e30
_LOG2E = float(np.log2(np.e))
VMEM_LIMIT = 56 * 1024 * 1024

GRID_W = 64
N_BRANCH = 4
A_HEADS = 4
A_PATTERNS = ((128, 1), (512, 4), (2048, 16))
B_HEADS = 4
C_HEADS = 8
C_KV_HEADS = 2
C_RADIUS = 128
D_HEADS = 4
D_WIN_R = 8
D_WIN_C = 16
N_EXPERTS = 16
N_GROUPS = 4
EXPERTS_PER_GROUP = N_EXPERTS // N_GROUPS

_NT = (((1,), (1,)), ((), ()))


def _params(*sem):
    return pltpu.CompilerParams(dimension_semantics=sem, vmem_limit_bytes=VMEM_LIMIT)


def _sigmoid(v):
    return 1.0 / (1.0 + jnp.exp(-v))


class _Layout:
    def __init__(self, d_model):
        bw = d_model // N_BRANCH
        self.bw = bw
        hb = bw // LANES
        o = {}
        o['qa'], o['ka'], o['va'] = 0, bw, 2 * bw
        o['qb'], o['kb'], o['vb'] = 3 * bw, 4 * bw, 5 * bw
        o['qc'] = 6 * bw
        o['kc'] = 7 * bw
        o['vc'] = 7 * bw + LANES
        o['qd'], o['kd'], o['vd'] = 7 * bw + 2 * LANES, 8 * bw + 2 * LANES, 9 * bw + 2 * LANES
        o['gates'] = 10 * bw + 2 * LANES
        self.orig = o
        gate_blocks = N_BRANCH * d_model // LANES
        self.GATE0 = 0
        b = gate_blocks
        self.QA0, self.KA0, self.VA0 = b, b + hb, b + 2 * hb
        b += 3 * hb
        self.QD0, self.KD0, self.VD0 = b, b + hb, b + 2 * hb
        b += 3 * hb
        self.QB0, self.KB0, self.VB0 = b, b + hb, b + 2 * hb
        b += 3 * hb
        self.QC0 = b
        self.KC0 = b + hb
        self.VC0 = b + hb + 2
        b += hb + 3
        self.n_real = b
        self.chunks_per_tile = 12
        self.n_blocks = -(-b // self.chunks_per_tile) * self.chunks_per_tile
        self.nz = self.n_blocks * LANES
        self.first_norm_tile = gate_blocks // self.chunks_per_tile
        assert self.QC0 % (C_HEADS // C_KV_HEADS // 2) == 0
        norm = [0] * self.n_blocks
        for blk0, nblk, group in ((self.QA0, 2 * hb, bw // A_HEADS), (self.QD0, 2 * hb, bw // D_HEADS),
                                  (self.QB0, 2 * hb, bw // (2 * B_HEADS)),
                                  (self.QC0, hb + 2, bw // C_HEADS)):
            assert group in (LANES, HALF)
            norm[blk0:blk0 + nblk] = [group] * nblk
        self.norm_group = tuple(norm)

    def weight_segments(self):
        o, bw = self.orig, self.bw
        kc0, kc1 = o['kc'], o['kc'] + HALF
        segs = [(o['gates'], N_BRANCH * N_BRANCH * bw),
                (o['qa'], bw), (o['ka'], bw), (o['va'], bw),
                (o['qd'], bw), (o['kd'], bw), (o['vd'], bw),
                (o['qb'], bw), (o['kb'], bw), (o['vb'], bw),
                (o['qc'], bw),
                (kc0, HALF), (kc0, HALF), (kc1, HALF), (kc1, HALF),
                (o['vc'], LANES)]
        pad = self.nz - self.n_real * LANES
        if pad:
            segs.append((None, pad))
        return segs


def _build_wz(w_in, lay):
    parts = []
    for start, width in lay.weight_segments():
        if start is None:
            parts.append(jnp.zeros(w_in.shape[:-1] + (width,), w_in.dtype))
        else:
            parts.append(w_in[..., start:start + width])
    return jnp.concatenate(parts, axis=-1).astype(BF16)


def _build_coef(lay, qk_a, qk_b, qk_c, qk_d):
    hb = lay.bw // LANES
    a_hd = lay.bw // A_HEADS
    b_dk = lay.bw // (2 * B_HEADS)
    c_hd = lay.bw // C_HEADS
    d_hd = lay.bw // D_HEADS
    gain = jnp.ones((lay.nz,), F32)

    def put(gain, blk0, nblk, g, scale):
        lo, hi = blk0 * LANES, (blk0 + nblk) * LANES
        reps = (hi - lo) // g.shape[0]
        return gain.at[lo:hi].set(jnp.tile(g.astype(F32), reps) * scale)

    gain = put(gain, lay.QA0, hb, qk_a[0], a_hd ** -0.5 * _LOG2E)
    gain = put(gain, lay.KA0, hb, qk_a[1], 1.0)
    gain = put(gain, lay.QD0, hb, qk_d[0], d_hd ** -0.5 * _LOG2E)
    gain = put(gain, lay.KD0, hb, qk_d[1], 1.0)
    gain = put(gain, lay.QB0, hb, qk_b[0], b_dk ** -0.5 * _LOG2E)
    gain = put(gain, lay.KB0, hb, qk_b[1], 1.0)
    gain = put(gain, lay.QC0, hb, qk_c[0], c_hd ** -0.5 * _LOG2E)
    gain = put(gain, lay.KC0, 2, qk_c[1], 1.0)
    return gain.reshape(1, lay.nz)


def _ada_kernel(c_ref, w_ref, b_ref, o_ref):
    c = c_ref[...]
    cond = (c * _sigmoid(c)).astype(BF16)
    o_ref[0] = jnp.dot(cond, w_ref[0].astype(BF16), preferred_element_type=F32) + b_ref[0]


def _ada_mod(c, ada_w, ada_b, tn=1024):
    depth, d, n = ada_w.shape
    rows = 8
    cp = jnp.zeros((rows, d), F32).at[:c.shape[0]].set(c)
    tn = min(tn, n)
    return pl.pallas_call(
        _ada_kernel,
        out_shape=jax.ShapeDtypeStruct((depth, rows, n), F32),
        grid=(depth, n // tn),
        in_specs=[pl.BlockSpec((rows, d), lambda l, j: (0, 0)),
                  pl.BlockSpec((1, d, tn), lambda l, j: (l, 0, j)),
                  pl.BlockSpec((1, 1, tn), lambda l, j: (l, 0, j))],
        out_specs=pl.BlockSpec((1, rows, tn), lambda l, j: (l, 0, j)),
        compiler_params=_params("parallel", "parallel"),
        name="ada_mod",
    )(cp, ada_w, ada_b.reshape(depth, 1, n))


def _mod_norm(x, g, shift, scale):
    y = x * lax.rsqrt(jnp.mean(x * x, axis=-1, keepdims=True) + RMS_EPS) * g
    return y * (1.0 + scale) + shift


def _inproj_kernel(x_ref, mod_ref, g_ref, w_ref, gain_ref, z_ref, h_scr, *, first_norm_tile, n_chunks,
                   norm_group):
    j = pl.program_id(1)

    @pl.when(j == 0)
    def _():
        h = _mod_norm(x_ref[...], g_ref[...], mod_ref[0, 0:1, :], mod_ref[0, 1:2, :])
        h_scr[...] = h.astype(BF16)

    acc = jnp.dot(h_scr[...], w_ref[0], preferred_element_type=F32)

    @pl.when(j < first_norm_tile)
    def _():
        z_ref[...] = acc.astype(BF16)

    for tile in range(first_norm_tile, len(norm_group) // n_chunks):
        @pl.when(j == tile)
        def _(tile=tile):
            lo_lane = lax.broadcasted_iota(jnp.int32, (1, LANES), 1) < HALF
            for c in range(n_chunks):
                sl = slice(c * LANES, (c + 1) * LANES)
                group = norm_group[tile * n_chunks + c]
                y = acc[:, sl]
                if group == 0:
                    z_ref[:, sl] = y.astype(BF16)
                    continue
                y2 = y * y
                if group == LANES:
                    ss = jnp.sum(y2, axis=-1, keepdims=True)
                else:
                    lo = jnp.sum(jnp.where(lo_lane, y2, 0.0), axis=-1, keepdims=True)
                    hi = jnp.sum(jnp.where(lo_lane, 0.0, y2), axis=-1, keepdims=True)
                    ss = jnp.where(lo_lane, lo, hi)
                r = lax.rsqrt(ss * (1.0 / group) + RMS_EPS) * gain_ref[0, :, sl]
                z_ref[:, sl] = (y * r).astype(BF16)


def _inproj(x2, mod, g, wz, coef, layer, lay, seq, tm=1024):
    t, d = x2.shape
    tm = min(tm, seq)
    tn = lay.chunks_per_tile * LANES
    per_batch = seq // tm
    kern = functools.partial(_inproj_kernel, first_norm_tile=lay.first_norm_tile,
                             n_chunks=lay.chunks_per_tile, norm_group=lay.norm_group)
    return pl.pallas_call(
        kern,
        out_shape=jax.ShapeDtypeStruct((t, lay.nz), BF16),
        grid=(t // tm, lay.nz // tn),
        in_specs=[pl.BlockSpec((tm, d), lambda i, j: (i, 0)),
                  pl.BlockSpec((1, 6, d), lambda i, j: (i // per_batch, 0, 0)),
                  pl.BlockSpec((1, d), lambda i, j: (0, 0)),
                  pl.BlockSpec((1, d, tn), lambda i, j: (layer, 0, j)),
                  pl.BlockSpec((1, 1, tn), lambda i, j: (layer, 0, j))],
        out_specs=pl.BlockSpec((tm, tn), lambda i, j: (i, j)),
        scratch_shapes=[pltpu.VMEM((tm, d), BF16)],
        compiler_params=_params("parallel", "arbitrary"),
        name="inproj",
    )(x2, mod, g, wz, coef)


def _band_kernel(sink_ref, q_ref, k_ref, v_ref, tab_ref, qmask_ref, o_ref, *, n_side, ck, nkc, hp, q_of,
                 nsub):
    g = pl.program_id(1)
    i = pl.program_id(2)
    for u in range(nsub):
        qi = i * nsub + u
        rows = slice(u * ck, (u + 1) * ck)
        chunks = []
        for d in range(-n_side, n_side + 1):
            kc = qi + d
            valid = jnp.logical_and(kc >= 0, kc < nkc)
            start = pl.multiple_of(jnp.clip(kc, 0, nkc - 1) * ck, ck)
            chunks.append((d + n_side, start, jnp.where(valid, 0.0, NEG_INF)))
        for h in range(hp):
            q = q_ref[rows, q_of(h) * LANES:(q_of(h) + 1) * LANES] * qmask_ref[h]
            sink = sink_ref[g * hp + h]
            m = jnp.full((ck, 1), sink, F32)
            scores = []
            for di, start, penalty in chunks:
                s = lax.dot_general(q, k_ref[pl.ds(start, ck), :], _NT, preferred_element_type=F32)
                s = s + tab_ref[h, di] + penalty
                m = jnp.maximum(m, jnp.max(s, axis=-1, keepdims=True))
                scores.append(s)
            den = jnp.exp2(sink - m)
            acc = jnp.zeros((ck, LANES), F32)
            for s, (di, start, penalty) in zip(scores, chunks):
                p = jnp.exp2(s - m)
                den = den + jnp.sum(p, axis=-1, keepdims=True)
                acc = acc + jnp.dot(p.astype(BF16), v_ref[pl.ds(start, ck), :],
                                    preferred_element_type=F32)
            o_ref[rows, h * LANES:(h + 1) * LANES] = (acc / den).astype(BF16)


def _band_attention(z, table, qmask, sink, *, batch, seq, n_groups, hp, q_width, q_of, qblk, kblk, vblk,
                    n_side, nsub=2):
    t = z.shape[0]
    ck = table.shape[-1]
    nsub = min(nsub, seq // ck)
    tq = nsub * ck
    nq = seq // tq
    kern = functools.partial(_band_kernel, n_side=n_side, ck=ck, nkc=seq // ck, hp=hp, q_of=q_of, nsub=nsub)
    nd = 2 * n_side + 1
    return pl.pallas_call(
        kern,
        out_shape=jax.ShapeDtypeStruct((t, n_groups * hp * LANES), BF16),
        grid=(batch, n_groups, nq),
        in_specs=[pl.BlockSpec(memory_space=pltpu.SMEM),
                  pl.BlockSpec((tq, q_width), lambda b, g, i: (b * nq + i, qblk(g))),
                  pl.BlockSpec((seq, LANES), lambda b, g, i: (b, kblk(g))),
                  pl.BlockSpec((seq, LANES), lambda b, g, i: (b, vblk(g))),
                  pl.BlockSpec((hp, nd, ck, ck), lambda b, g, i: (g, 0, 0, 0)),
                  pl.BlockSpec((hp, 1, LANES), lambda b, g, i: (g, 0, 0))],
        out_specs=pl.BlockSpec((tq, hp * LANES), lambda b, g, i: (b * nq + i, g)),
        compiler_params=_params("parallel", "parallel", "arbitrary"),
        name="band_attention",
    )(sink, z, z, z, table, qmask)


def _alibi_slopes(n):
    return np.array([2.0 ** (-8.0 * (i + 1) / n) for i in range(n)], dtype=np.float32)


def _dilated_table(tq):
    reach = max(w // 2 for w, _ in A_PATTERNS)
    n_side = -(-reach // tq)
    d = np.arange(-n_side, n_side + 1)[:, None, None] * tq
    off = d + np.arange(tq)[None, None, :] - np.arange(tq)[None, :, None]
    mult = np.zeros(off.shape, np.float64)
    for window, dil in A_PATTERNS:
        mult += ((off % dil) == 0) & (np.abs(off) <= window // 2)
    logm = np.where(mult > 0, np.log(np.maximum(mult, 1.0)), 0.0)
    slopes = _alibi_slopes(A_HEADS).astype(np.float64)
    tab = (-slopes[:, None, None, None] * np.abs(off)[None] + logm[None]) * _LOG2E
    tab = np.where(mult[None] > 0, tab, NEG_INF)
    return jnp.asarray(tab.astype(np.float32)), n_side


def _window_table(tq):
    n_side = -(-C_RADIUS // tq)
    d = np.arange(-n_side, n_side + 1)[:, None, None] * tq
    off = d + np.arange(tq)[None, None, :] - np.arange(tq)[None, :, None]
    slopes = _alibi_slopes(C_HEADS).astype(np.float64)
    tab = -slopes[:, None, None, None] * np.abs(off)[None] * _LOG2E
    tab = np.where((np.abs(off) <= C_RADIUS)[None], tab, NEG_INF)
    return jnp.asarray(tab.astype(np.float32)), n_side


_D_QROWS = 8
_D_KROWS = 16


_D_HEADS_PER_STEP = 2


def _nbr_kernel(q_ref, k_ref, v_ref, tab_ref, o_ref, *, rows):
    t = pl.program_id(2)
    krow = jnp.clip(t * _D_QROWS - D_WIN_R // 2, 0, rows - _D_KROWS)
    start = pl.multiple_of(krow * GRID_W, GRID_W * (D_WIN_R // 2))
    nk = _D_KROWS * GRID_W
    for h in range(_D_HEADS_PER_STEP):
        lanes = slice(h * LANES, (h + 1) * LANES)
        s = lax.dot_general(q_ref[:, lanes], k_ref[pl.ds(start, nk), lanes], _NT,
                            preferred_element_type=F32)
        s = s + tab_ref[h, 0]
        m = jnp.max(s, axis=-1, keepdims=True)
        p = jnp.exp2(s - m)
        den = jnp.sum(p, axis=-1, keepdims=True)
        acc = jnp.dot(p.astype(BF16), v_ref[pl.ds(start, nk), lanes], preferred_element_type=F32)
        o_ref[:, lanes] = (acc / den).astype(BF16)


def _nbr_tables(rpb, rows):
    depth, heads, n_dr, n_dc = rpb.shape
    nt = rows // _D_QROWS
    wr = min(D_WIN_R, rows)
    pad_l = GRID_W - 1 - (D_WIN_C - 1)
    vpad = jnp.pad(rpb.astype(F32), ((0, 0), (0, 0), (0, 0), (pad_l, 2 * GRID_W - 1 - n_dc - pad_l)))
    toep = jnp.stack([vpad[..., GRID_W - 1 - c:2 * GRID_W - 1 - c] for c in range(GRID_W)], axis=-2)
    c = np.arange(GRID_W)
    cs = np.clip(c - D_WIN_C // 2, 0, GRID_W - D_WIN_C)
    col_ok = (c[None, :] >= cs[:, None]) & (c[None, :] < cs[:, None] + D_WIN_C)
    colmat = jnp.where(col_ok, toep * _LOG2E, NEG_INF)
    masked = jnp.full((depth, heads, 1, GRID_W, GRID_W), NEG_INF, F32)
    colmat = jnp.concatenate([colmat, masked], axis=2)
    sel = np.full((3, _D_QROWS, _D_KROWS), n_dr, np.int32)
    for kind, t in enumerate((0, 1, nt - 1)):
        krow0 = int(np.clip(t * _D_QROWS - D_WIN_R // 2, 0, rows - _D_KROWS))
        for ri in range(_D_QROWS):
            r = t * _D_QROWS + ri
            rs = int(np.clip(r - wr // 2, 0, rows - wr))
            for kj in range(_D_KROWS):
                kr = krow0 + kj
                if rs <= kr < rs + wr:
                    sel[kind, ri, kj] = kr - r + (D_WIN_R - 1)
    blocks = jnp.take(colmat, jnp.asarray(sel.reshape(-1)), axis=2)
    blocks = blocks.reshape(depth * heads, 3, _D_QROWS, _D_KROWS, GRID_W, GRID_W)
    blocks = blocks.transpose(0, 1, 2, 4, 3, 5)
    return blocks.reshape(depth * heads, 3, _D_QROWS * GRID_W, _D_KROWS * GRID_W)


def _nbr_attention(z, table, layer, *, batch, seq, lay):
    t = z.shape[0]
    rows = seq // GRID_W
    nt = rows // _D_QROWS
    tq = _D_QROWS * GRID_W
    nk = _D_KROWS * GRID_W
    kern = functools.partial(_nbr_kernel, rows=rows)
    hp = _D_HEADS_PER_STEP
    groups = D_HEADS // hp
    width = hp * LANES
    assert lay.QD0 % hp == 0 and lay.KD0 % hp == 0 and lay.VD0 % hp == 0
    kind = lambda i: jnp.where(i == 0, 0, jnp.where(i == nt - 1, 2, 1))
    return pl.pallas_call(
        kern,
        out_shape=jax.ShapeDtypeStruct((t, D_HEADS * LANES), BF16),
        grid=(batch, groups, nt),
        in_specs=[pl.BlockSpec((tq, width), lambda b, g, i: (b * nt + i, lay.QD0 // hp + g)),
                  pl.BlockSpec((seq, width), lambda b, g, i: (b, lay.KD0 // hp + g)),
                  pl.BlockSpec((seq, width), lambda b, g, i: (b, lay.VD0 // hp + g)),
                  pl.BlockSpec((hp, 1, tq, nk), lambda b, g, i: (layer * groups + g, kind(i), 0, 0))],
        out_specs=pl.BlockSpec((tq, width), lambda b, g, i: (b * nt + i, g)),
        compiler_params=_params("parallel", "parallel", "arbitrary"),
        name="nbr_attention",
    )(z, z, z, table)


_N_BIAS_PIECES = 3
_KPOS_RADIX = 64


def _bf16_round(v):
    u = np.asarray(v, np.float32).view(np.uint32)
    u = (u + np.uint32(0x7FFF) + ((u >> np.uint32(16)) & np.uint32(1))) & np.uint32(0xFFFF0000)
    return u.view(np.float32)


def _diff_feature_tables(tk):
    c32 = (_alibi_slopes(B_HEADS).astype(np.float64) * np.log2(np.e)).astype(np.float32)
    qconst = np.zeros((B_HEADS, 2, LANES), np.float32)
    for h in range(B_HEADS):
        rem = np.float32(c32[h])
        pieces = []
        for _ in range(_N_BIAS_PIECES):
            p = np.float32(_bf16_round(rem))
            pieces.append(p)
            rem = np.float32(rem - p)
        for br, base in ((0, HALF), (1, 0)):
            for n, p in enumerate(pieces):
                qconst[h, br, base + n] = -_KPOS_RADIX * p
                qconst[h, br, base + _N_BIAS_PIECES + n] = -p
    j = np.arange(tk)
    kfeat = np.zeros((2, tk, LANES), np.float32)
    for br, base in ((0, HALF), (1, 0)):
        for n in range(_N_BIAS_PIECES):
            kfeat[br, :, base + n] = j // _KPOS_RADIX
            kfeat[br, :, base + _N_BIAS_PIECES + n] = j % _KPOS_RADIX
            kfeat[br, :, base + 2 * _N_BIAS_PIECES + n] = 1.0
    return jnp.asarray(qconst), jnp.asarray(kfeat, BF16), jnp.asarray(c32)


def _diff_kernel(c_ref, misc_ref, q_ref, k_ref, v_ref, qconst_ref, kfeat_ref, dl_ref, subg_ref, o_ref,
                 kaug_scr, *, tq, tk, nk):
    h = pl.program_id(1)
    i = pl.program_id(2)
    c = c_ref[h]
    lam_init = misc_ref[0]
    lane = lax.broadcasted_iota(jnp.int32, (1, LANES), 1)
    lo_lane = lane < HALF

    @pl.when(i == 0)
    def _():
        def aug(cc, carry):
            sl = pl.ds(pl.multiple_of(cc * tk, tk), tk)
            kb = k_ref[sl, :]
            kaug_scr[0, sl, :] = jnp.where(lo_lane, kb, kfeat_ref[0])
            kaug_scr[1, sl, :] = jnp.where(lo_lane, kfeat_ref[1], kb)
            return carry
        lax.fori_loop(0, nk, aug, 0)

    q = q_ref[...].astype(F32)
    ci = c * lax.broadcasted_iota(jnp.int32, (tq, 1), 0).astype(F32)
    r1 = ci.astype(BF16).astype(F32)
    r2 = (ci - r1).astype(BF16).astype(F32)
    r3 = (ci - r1 - r2).astype(BF16).astype(F32)
    q_plain, q_right, q_left = [], [], []
    for br, base in ((0, HALF), (1, 0)):
        keep = lo_lane if br == 0 else jnp.logical_not(lo_lane)
        f0 = base + 2 * _N_BIAS_PIECES
        feat = qconst_ref[0, br:br + 1, :] + jnp.where(
            lane == f0, r1, jnp.where(lane == f0 + 1, r2, jnp.where(lane == f0 + 2, r3, 0.0)))
        q_plain.append(jnp.where(keep, q, 0.0).astype(BF16))
        q_right.append(jnp.where(keep, q, feat).astype(BF16))
        q_left.append(jnp.where(keep, q, -feat).astype(BF16))

    state = [(jnp.full((tq, 1), NEG_INF, F32), jnp.zeros((tq, 1), F32), jnp.zeros((tq, LANES), F32))
             for _ in range(2)]

    def update(st, s, shift, vb):
        m_old, l_old, acc_old = st
        m_new = jnp.maximum(m_old, jnp.max(s, axis=-1, keepdims=True) + shift)
        alpha = jnp.exp2(m_old - m_new)
        p = jnp.exp2(s + (shift - m_new))
        l_new = alpha * l_old + jnp.sum(p, axis=-1, keepdims=True)
        acc_new = alpha * acc_old + jnp.dot(p.astype(BF16), vb, preferred_element_type=F32)
        return m_new, l_new, acc_new

    cdiag = (i * tq) // tk
    for step in range(nk):
        ct = cdiag + step
        on_right = ct < nk
        cc = jnp.where(on_right, ct, ct - nk)
        sl = pl.ds(pl.multiple_of(cc * tk, tk), tk)
        vb = v_ref[sl, :]
        if step == 0:
            dji = (lax.broadcasted_iota(jnp.int32, (tq, tk), 1)
                   - lax.broadcasted_iota(jnp.int32, (tq, tk), 0))
            bias = -c * jnp.abs(dji + (cdiag * tk - i * tq)).astype(F32)
            ss = [lax.dot_general(q_plain[idx], kaug_scr[idx, sl, :], _NT, preferred_element_type=F32) + bias
                  for idx in range(2)]
            shift = 0.0
        else:
            ss = [lax.dot_general(jnp.where(on_right, q_right[idx], q_left[idx]), kaug_scr[idx, sl, :], _NT,
                                  preferred_element_type=F32) for idx in range(2)]
            shift = jnp.where(on_right, -1.0, 1.0) * c * (cc * tk - i * tq).astype(F32)
        state = [update(state[idx], ss[idx], shift, vb) for idx in range(2)]

    hs = pl.ds(h, 1)
    lam = (jnp.exp(jnp.sum(dl_ref[0, hs, :] * dl_ref[1, hs, :], axis=-1, keepdims=True))
           - jnp.exp(jnp.sum(dl_ref[2, hs, :] * dl_ref[3, hs, :], axis=-1, keepdims=True)) + lam_init)
    o = state[0][2] / state[0][1] - lam * (state[1][2] / state[1][1])
    o = o * lax.rsqrt(jnp.mean(o * o, axis=-1, keepdims=True) + RMS_EPS) * subg_ref[...]
    o_ref[...] = (o * (1.0 - lam_init)).astype(BF16)


def _diff_attention(z, dl, subg, tables, misc, *, batch, seq, lay, tq, tk):
    t = z.shape[0]
    nq = seq // tq
    qconst, kfeat, cvals = tables
    kern = functools.partial(_diff_kernel, tq=tq, tk=tk, nk=seq // tk)
    return pl.pallas_call(
        kern,
        out_shape=jax.ShapeDtypeStruct((t, B_HEADS * LANES), BF16),
        grid=(batch, B_HEADS, nq),
        in_specs=[pl.BlockSpec(memory_space=pltpu.SMEM),
                  pl.BlockSpec(memory_space=pltpu.SMEM),
                  pl.BlockSpec((tq, LANES), lambda b, h, i: (b * nq + i, lay.QB0 + h)),
                  pl.BlockSpec((seq, LANES), lambda b, h, i: (b, lay.KB0 + h)),
                  pl.BlockSpec((seq, LANES), lambda b, h, i: (b, lay.VB0 + h)),
                  pl.BlockSpec((1, 2, LANES), lambda b, h, i: (h, 0, 0)),
                  pl.BlockSpec(kfeat.shape, lambda b, h, i: (0, 0, 0)),
                  pl.BlockSpec(dl.shape, lambda b, h, i: (0, 0, 0)),
                  pl.BlockSpec((1, LANES), lambda b, h, i: (0, 0))],
        out_specs=pl.BlockSpec((tq, LANES), lambda b, h, i: (b * nq + i, h)),
        scratch_shapes=[pltpu.VMEM((2, seq, LANES), BF16)],
        compiler_params=_params("parallel", "arbitrary", "arbitrary"),
        name="diff_attention",
    )(cvals, misc, z, z, z, qconst, kfeat, dl, subg)


def _merge_kernel(oa_ref, ob_ref, oc_ref, od_ref, gate_ref, wa_ref, wb_ref, wc_ref, wd_ref, o_ref, *, d):
    acc = None
    for idx, (o, w) in enumerate(((oa_ref, wa_ref), (ob_ref, wb_ref), (oc_ref, wc_ref), (od_ref, wd_ref))):
        y = jnp.dot(o[...], w[...], preferred_element_type=F32)
        g = _sigmoid(gate_ref[:, idx * d:(idx + 1) * d].astype(F32))
        acc = g * y if acc is None else acc + g * y
    o_ref[...] = acc.astype(BF16)


def _merge(oa, ob, oc, od, z, wbs, *, d, tm=256):
    t = oa.shape[0]
    tm = min(tm, t)
    row = lambda a: pl.BlockSpec((tm, a.shape[1]), lambda i: (i, 0))
    full = lambda a: pl.BlockSpec(a.shape, lambda i: (0, 0))
    return pl.pallas_call(
        functools.partial(_merge_kernel, d=d),
        out_shape=jax.ShapeDtypeStruct((t, d), BF16),
        grid=(t // tm,),
        in_specs=[row(oa), row(ob), row(oc), row(od),
                  pl.BlockSpec((tm, N_BRANCH * d), lambda i: (i, 0))] + [full(w) for w in wbs],
        out_specs=pl.BlockSpec((tm, d), lambda i: (i, 0)),
        compiler_params=_params("parallel"),
        name="branch_merge",
    )(oa, ob, oc, od, z, *wbs)


def _outproj_kernel(a_ref, w_ref, x_ref, g_ref, o_ref):
    y = jnp.dot(a_ref[...], w_ref[0], preferred_element_type=F32)
    o_ref[...] = x_ref[...] + g_ref[0] * y


def _outproj(a, w, layer, x2, gate, seq, tm=512):
    t, d = x2.shape
    tm = min(tm, seq)
    per_batch = seq // tm
    return pl.pallas_call(
        _outproj_kernel,
        out_shape=jax.ShapeDtypeStruct((t, d), F32),
        grid=(t // tm,),
        in_specs=[pl.BlockSpec((tm, a.shape[1]), lambda i: (i, 0)),
                  pl.BlockSpec((1,) + w.shape[1:], lambda i: (layer, 0, 0)),
                  pl.BlockSpec((tm, d), lambda i: (i, 0)),
                  pl.BlockSpec((1, 1, d), lambda i: (i // per_batch, 0, 0))],
        out_specs=pl.BlockSpec((tm, d), lambda i: (i, 0)),
        compiler_params=_params("parallel"),
        name="out_proj",
    )(a, w, x2, gate)


def _router_kernel(x_ref, mod_ref, g_ref, rwt_ref, rb_ref, h_ref, e_ref, gw_ref):
    h = _mod_norm(x_ref[...], g_ref[...], mod_ref[0, 3:4, :], mod_ref[0, 4:5, :])
    h_ref[...] = h
    logits = lax.dot_general(rwt_ref[...], h.astype(BF16), _NT, preferred_element_type=F32)
    score = _sigmoid(logits)
    sel = score + rb_ref[...]
    sel_r = [sel[e:e + 1, :] for e in range(N_EXPERTS)]
    sc_r = [score[e:e + 1, :] for e in range(N_EXPERTS)]
    n = EXPERTS_PER_GROUP

    def top2_sum(v):
        best = None
        for a in range(n):
            for b in range(a + 1, n):
                s = v[a] + v[b]
                best = s if best is None else jnp.maximum(best, s)
        return best

    gbest = top2_sum(sel_r[0:n])
    gidx = jnp.zeros(gbest.shape, jnp.int32)
    cv = list(sel_r[0:n])
    cs = list(sc_r[0:n])
    for g in range(1, N_GROUPS):
        gs = top2_sum(sel_r[g * n:(g + 1) * n])
        upd = gs > gbest
        gbest = jnp.where(upd, gs, gbest)
        gidx = jnp.where(upd, g, gidx)
        for j in range(n):
            cv[j] = jnp.where(upd, sel_r[g * n + j], cv[j])
            cs[j] = jnp.where(upd, sc_r[g * n + j], cs[j])
    b1, i1, s1 = cv[0], jnp.zeros(gbest.shape, jnp.int32), cs[0]
    for j in range(1, n):
        upd = cv[j] > b1
        b1 = jnp.where(upd, cv[j], b1)
        i1 = jnp.where(upd, j, i1)
        s1 = jnp.where(upd, cs[j], s1)
    low = -3.0e38
    b2 = jnp.full(gbest.shape, low, F32)
    i2 = jnp.zeros(gbest.shape, jnp.int32)
    s2 = jnp.zeros(gbest.shape, F32)
    for j in range(n):
        cand = jnp.where(i1 == j, low, cv[j])
        upd = cand > b2
        b2 = jnp.where(upd, cand, b2)
        i2 = jnp.where(upd, j, i2)
        s2 = jnp.where(upd, cs[j], s2)
    tot = s1 + s2
    e_ref[0:1, :] = gidx * n + i1
    e_ref[1:2, :] = gidx * n + i2
    gw_ref[0:1, :] = s1 / tot
    gw_ref[1:2, :] = s2 / tot


def _router(x2, mod, g, rwt, rb, seq, tm=512):
    t, d = x2.shape
    tm = min(tm, seq)
    per_batch = seq // tm
    return pl.pallas_call(
        _router_kernel,
        out_shape=(jax.ShapeDtypeStruct((t, d), F32),
                   jax.ShapeDtypeStruct((2, t), jnp.int32),
                   jax.ShapeDtypeStruct((2, t), F32)),
        grid=(t // tm,),
        in_specs=[pl.BlockSpec((tm, d), lambda i: (i, 0)),
                  pl.BlockSpec((1, 6, d), lambda i: (i // per_batch, 0, 0)),
                  pl.BlockSpec((1, d), lambda i: (0, 0)),
                  pl.BlockSpec(rwt.shape, lambda i: (0, 0)),
                  pl.BlockSpec(rb.shape, lambda i: (0, 0))],
        out_specs=(pl.BlockSpec((tm, d), lambda i: (i, 0)),
                   pl.BlockSpec((2, tm), lambda i: (0, i)),
                   pl.BlockSpec((2, tm), lambda i: (0, i))),
        compiler_params=_params("parallel"),
        name="norm_router",
    )(x2, mod, g, rwt, rb)


def _rank_kernel(e_ref, rank_ref, cnt_ref, carry):
    @pl.when(pl.program_id(0) == 0)
    def _():
        carry[...] = jnp.zeros(carry.shape, F32)

    tm = e_ref.shape[1]
    eio = lax.broadcasted_iota(jnp.int32, (N_EXPERTS, tm), 0)
    oh0 = eio == e_ref[0:1, :]
    oh1 = eio == e_ref[1:2, :]
    c = jnp.where(jnp.logical_or(oh0, oh1), 1.0, 0.0)
    before = (lax.broadcasted_iota(jnp.int32, (tm, tm), 0) < lax.broadcasted_iota(jnp.int32, (tm, tm), 1))
    pref = jnp.dot(c.astype(BF16), jnp.where(before, 1.0, 0.0).astype(BF16),
                   preferred_element_type=F32) + carry[...]
    rank_ref[0:1, :] = jnp.sum(jnp.where(oh0, pref, 0.0), axis=0, keepdims=True).astype(jnp.int32)
    rank_ref[1:2, :] = jnp.sum(jnp.where(oh1, pref, 0.0), axis=0, keepdims=True).astype(jnp.int32)
    total = carry[...] + jnp.sum(c, axis=1, keepdims=True)
    carry[...] = total
    cnt_ref[...] = jnp.broadcast_to(total, cnt_ref.shape)


def _rank(eidx, tm=512):
    t = eidx.shape[1]
    tm = min(tm, t)
    return pl.pallas_call(
        _rank_kernel,
        out_shape=(jax.ShapeDtypeStruct((2, t), jnp.int32),
                   jax.ShapeDtypeStruct((N_EXPERTS, LANES), F32)),
        grid=(t // tm,),
        in_specs=[pl.BlockSpec((2, tm), lambda i: (0, i))],
        out_specs=(pl.BlockSpec((2, tm), lambda i: (0, i)),
                   pl.BlockSpec((N_EXPERTS, LANES), lambda i: (0, 0))),
        scratch_shapes=[pltpu.VMEM((N_EXPERTS, 1), F32)],
        compiler_params=_params("arbitrary"),
        name="expert_rank",
    )(eidx)


_DMA_UNROLL = 8


def _row_copy(src_hbm, row, dst, r, sem):
    return pltpu.make_async_copy(src_hbm.at[pl.ds(row, 1), :], dst.at[pl.ds(r, 1), :], sem)


def _expert_kernel(te_ref, nu_ref, src_ref, h_hbm, w1_ref, w3_ref, w2_ref, y_ref, xbuf0, xbuf1, sem, *, tm):
    j = pl.program_id(0)
    n_used = nu_ref[0]

    def wait_tile(buf, s):
        pltpu.make_async_copy(h_hbm.at[pl.ds(0, tm), :], buf, s).wait()

    @pl.when(jnp.logical_and(j == 0, n_used > 0))
    def _():
        def issue(r, carry):
            _row_copy(h_hbm, src_ref[r], xbuf0, r, sem.at[0]).start()
            return carry
        lax.fori_loop(0, tm, issue, 0, unroll=_DMA_UNROLL)

    def tile(cur, cur_sem, nxt, nxt_sem):
        wait_tile(cur, cur_sem)
        xs = cur[...].astype(BF16)
        base = jnp.minimum(j + 1, n_used - 1) * tm
        for r in range(tm):
            _row_copy(h_hbm, src_ref[base + r], nxt, r, nxt_sem).start()
        g = jnp.dot(xs, w1_ref[0], preferred_element_type=F32)
        u = jnp.dot(xs, w3_ref[0], preferred_element_type=F32)
        mid = (g * _sigmoid(g) * u).astype(BF16)
        y_ref[...] = jnp.dot(mid, w2_ref[0], preferred_element_type=F32)

        @pl.when(j == n_used - 1)
        def _():
            wait_tile(nxt, nxt_sem)

    @pl.when(jnp.logical_and(j < n_used, j % 2 == 0))
    def _():
        tile(xbuf0, sem.at[0], xbuf1, sem.at[1])

    @pl.when(jnp.logical_and(j < n_used, j % 2 == 1))
    def _():
        tile(xbuf1, sem.at[1], xbuf0, sem.at[0])

    @pl.when(j >= n_used)
    def _():
        y_ref[...] = jnp.zeros(y_ref.shape, F32)


def _experts(tile_e, n_used, src, h, w1, w3, w2, layer, tm):
    p = src.shape[0]
    d = h.shape[1]
    f = w1.shape[-1]
    n_exp = w1.shape[1]
    w1r = w1.reshape((-1,) + w1.shape[2:])
    w3r = w3.reshape((-1,) + w3.shape[2:])
    w2r = w2.reshape((-1,) + w2.shape[2:])
    wmap = lambda j, te, nu, src: (layer * n_exp + te[j], 0, 0)
    return pl.pallas_call(
        functools.partial(_expert_kernel, tm=tm),
        out_shape=jax.ShapeDtypeStruct((p, d), F32),
        grid_spec=pltpu.PrefetchScalarGridSpec(
            num_scalar_prefetch=3,
            grid=(p // tm,),
            in_specs=[pl.BlockSpec(memory_space=pl.ANY),
                      pl.BlockSpec((1, d, f), wmap),
                      pl.BlockSpec((1, d, f), wmap),
                      pl.BlockSpec((1, f, d), wmap)],
            out_specs=pl.BlockSpec((tm, d), lambda j, te, nu, src: (j, 0)),
            scratch_shapes=[pltpu.VMEM((tm, d), F32), pltpu.VMEM((tm, d), F32),
                            pltpu.SemaphoreType.DMA((2,))]),
        compiler_params=_params("arbitrary"),
        name="experts",
    )(tile_e, n_used, src, h, w1r, w3r, w2r)


def _combine_kernel(pos_ref, x_ref, g_ref, gw_ref, y_hbm, o_ref, a0, a1, b0, b1, sem, *, tm, t, n_steps):
    i = pl.program_id(0)

    def wait_pair(buf0, buf1, s0, s1):
        pltpu.make_async_copy(y_hbm.at[pl.ds(0, tm), :], buf0, s0).wait()
        pltpu.make_async_copy(y_hbm.at[pl.ds(0, tm), :], buf1, s1).wait()

    @pl.when(i == 0)
    def _():
        def issue(r, carry):
            _row_copy(y_hbm, pos_ref[r], a0, r, sem.at[0]).start()
            _row_copy(y_hbm, pos_ref[t + r], a1, r, sem.at[1]).start()
            return carry
        lax.fori_loop(0, tm, issue, 0, unroll=_DMA_UNROLL)

    def step(cur, cur_sems, nxt, nxt_sems):
        wait_pair(cur[0], cur[1], cur_sems[0], cur_sems[1])
        base = jnp.minimum(i + 1, n_steps - 1) * tm
        for r in range(tm):
            _row_copy(y_hbm, pos_ref[base + r], nxt[0], r, nxt_sems[0]).start()
            _row_copy(y_hbm, pos_ref[t + base + r], nxt[1], r, nxt_sems[1]).start()
        moe = gw_ref[:, 0:1] * cur[0][...] + gw_ref[:, 1:2] * cur[1][...]
        o_ref[...] = x_ref[...] + g_ref[0] * moe

        @pl.when(i == n_steps - 1)
        def _():
            wait_pair(nxt[0], nxt[1], nxt_sems[0], nxt_sems[1])

    sa = (sem.at[0], sem.at[1])
    sb = (sem.at[2], sem.at[3])

    @pl.when(i % 2 == 0)
    def _():
        step((a0, a1), sa, (b0, b1), sb)

    @pl.when(i % 2 == 1)
    def _():
        step((b0, b1), sb, (a0, a1), sa)


def _combine(pos, x2, gate, gwt, y, seq, tm=256):
    t, d = x2.shape
    tm = min(tm, seq)
    per_batch = seq // tm
    n_steps = t // tm
    return pl.pallas_call(
        functools.partial(_combine_kernel, tm=tm, t=t, n_steps=n_steps),
        out_shape=jax.ShapeDtypeStruct((t, d), F32),
        grid_spec=pltpu.PrefetchScalarGridSpec(
            num_scalar_prefetch=1,
            grid=(n_steps,),
            in_specs=[pl.BlockSpec((tm, d), lambda i, pos: (i, 0)),
                      pl.BlockSpec((1, 1, d), lambda i, pos: (i // per_batch, 0, 0)),
                      pl.BlockSpec((tm, 2), lambda i, pos: (i, 0)),
                      pl.BlockSpec(memory_space=pl.ANY)],
            out_specs=pl.BlockSpec((tm, d), lambda i, pos: (i, 0)),
            scratch_shapes=[pltpu.VMEM((tm, d), F32)] * 4 + [pltpu.SemaphoreType.DMA((4,))]),
        compiler_params=_params("arbitrary"),
        name="moe_combine",
    )(pos, x2, gate, gwt, y)


def _moe(x2, mod, g2n, rwt, rb, w1, w3, w2, layer, *, batch, seq, tm_e=256):
    t, d = x2.shape
    h, eidx, gw = _router(x2, mod, g2n, rwt, rb, seq)
    rank, cnt = _rank(eidx)
    tm_e = min(tm_e, t)
    counts = cnt[:, 0].astype(jnp.int32)
    gsz = ((counts + tm_e - 1) // tm_e) * tm_e
    ends = jnp.cumsum(gsz)
    offs = ends - gsz
    pos = rank
    for e in range(N_EXPERTS):
        pos = pos + jnp.where(eidx == e, offs[e], 0)
    n_tiles = (2 * t) // tm_e + N_EXPERTS
    p = n_tiles * tm_e
    flat = pos.reshape(-1)
    tok = jnp.broadcast_to(jnp.arange(t, dtype=jnp.int32)[None, :], (2, t)).reshape(-1)
    src = jnp.zeros((p,), jnp.int32).at[flat].set(tok)
    tile_start = jnp.arange(n_tiles, dtype=jnp.int32) * tm_e
    tile_e = jnp.minimum(jnp.sum(tile_start[:, None] >= ends[None, :], axis=1),
                         N_EXPERTS - 1).astype(jnp.int32)
    n_used = (ends[-1:] // tm_e).astype(jnp.int32)
    y = _experts(tile_e, n_used, src, h, w1, w3, w2, layer, tm_e)
    return _combine(flat, x2, mod[:, 5:6, :], gw.T, y, seq)


def kernel(x, c, ada_w, ada_b, norm1_g, norm2_g, w_in, qk_norm_a, qk_norm_b, qk_norm_c, qk_norm_d,
           diff_lambda, diff_subln_g, sink_logits, rpb, w_branch, w_out, router_w, router_b, w1, w3, w2):
    batch, seq, d = x.shape
    depth = ada_w.shape[0]
    t = batch * seq
    lay = _Layout(d)
    bw = lay.bw

    mod_all = _ada_mod(c, ada_w, ada_b)
    wz = _build_wz(w_in, lay)
    coef = jnp.stack([_build_coef(lay, qk_norm_a[l], qk_norm_b[l], qk_norm_c[l], qk_norm_d[l])
                      for l in range(depth)])
    wb16 = w_branch.astype(BF16)
    c_hd = bw // C_HEADS
    grp = C_HEADS // C_KV_HEADS
    wc_rows = []
    for hq in range(C_HEADS):
        blk = jnp.zeros((depth, LANES, d), BF16)
        off = (hq // grp) * c_hd
        wc_rows.append(blk.at[:, off:off + c_hd].set(wb16[:, 2, hq * c_hd:(hq + 1) * c_hd]))
    wc_ext = jnp.concatenate(wc_rows, axis=1)
    wout16 = w_out.astype(BF16)
    w1b, w3b, w2b = w1.astype(BF16), w3.astype(BF16), w2.astype(BF16)
    rwt = router_w.T.astype(BF16)
    rb = router_b.reshape(N_EXPERTS, 1).astype(F32)

    tq = 256
    tab_a, side_a = _dilated_table(tq)
    tab_c, side_c = _window_table(tq)
    ones_mask = jnp.ones((A_HEADS, 1, LANES), BF16)
    lane = np.arange(LANES)
    cmask = np.stack([(lane // HALF) == (hq % 2) for hq in range(C_HEADS)]).astype(np.float32)
    cmask = jnp.asarray(cmask.reshape(C_HEADS, 1, LANES), BF16)
    no_sink = jnp.full((A_HEADS,), NEG_INF, F32)
    tq_b, tk_b = min(512, seq), min(1024, seq)
    tables_b = _diff_feature_tables(tk_b)
    tab_d = _nbr_tables(rpb, seq // GRID_W)

    x2 = x.reshape(t, d)
    for l in range(depth):
        mod = mod_all[l, :batch].reshape(batch, 6, d)
        z = _inproj(x2, mod, norm1_g[l].reshape(1, d), wz, coef, l, lay, seq)
        o_a = _band_attention(z, tab_a, ones_mask, no_sink, batch=batch, seq=seq, n_groups=A_HEADS, hp=1,
                              q_width=LANES, q_of=lambda h: 0,
                              qblk=lambda g: lay.QA0 + g, kblk=lambda g: lay.KA0 + g,
                              vblk=lambda g: lay.VA0 + g, n_side=side_a)
        lam_init = 0.8 - 0.6 * float(np.exp(-0.3 * l))
        o_b = _diff_attention(z, diff_lambda[l], diff_subln_g[l].reshape(1, LANES), tables_b,
                              jnp.full((1,), lam_init, F32), batch=batch, seq=seq, lay=lay,
                              tq=tq_b, tk=tk_b)
        o_c = _band_attention(z, tab_c, cmask, sink_logits[l].astype(F32) * _LOG2E, batch=batch, seq=seq,
                              n_groups=C_KV_HEADS, hp=grp, q_width=(grp // 2) * LANES,
                              q_of=lambda h: h // 2,
                              qblk=lambda g: lay.QC0 // (grp // 2) + g, kblk=lambda g: lay.KC0 + g,
                              vblk=lambda g: lay.VC0, n_side=side_c)
        o_d = _nbr_attention(z, tab_d, l, batch=batch, seq=seq, lay=lay)
        merged = _merge(o_a, o_b, o_c, o_d, z, (wb16[l, 0], wb16[l, 1], wc_ext[l], wb16[l, 3]), d=d)
        x2 = _outproj(merged, wout16, l, x2, mod[:, 2:3, :], seq)
        x2 = _moe(x2, mod, norm2_g[l].reshape(1, d), rwt, rb, w1b, w3b, w2b, l, batch=batch, seq=seq)
    return x2.reshape(batch, seq, d)
```

```python
import functools

import numpy as np
import jax
import jax.numpy as jnp
from jax import lax
from jax.experimental import pallas as pl
from jax.experimental.pallas import tpu as pltpu

F32 = jnp.float32
BF16 = jnp.bfloat16
LANES = 128
HALF = LANES // 2
RMS_EPS = 1e-6
NEG_INF = -1e30
_LOG2E = float(np.log2(np.e))
VMEM_LIMIT = 56 * 1024 * 1024

GRID_W = 64
N_BRANCH = 4
A_HEADS = 4
A_PATTERNS = ((128, 1), (512, 4), (2048, 16))
B_HEADS = 4
C_HEADS = 8
C_KV_HEADS = 2
C_RADIUS = 128
D_HEADS = 4
D_WIN_R = 8
D_WIN_C = 16
N_EXPERTS = 16
N_GROUPS = 4
EXPERTS_PER_GROUP = N_EXPERTS // N_GROUPS

_NT = (((1,), (1,)), ((), ()))


def _params(*sem):
    return pltpu.CompilerParams(dimension_semantics=sem, vmem_limit_bytes=VMEM_LIMIT)


def _sigmoid(v):
    return 1.0 / (1.0 + jnp.exp(-v))


class _Layout:
    def __init__(self, d_model):
        bw = d_model // N_BRANCH
        self.bw = bw
        hb = bw // LANES
        o = {}
        o['qa'], o['ka'], o['va'] = 0, bw, 2 * bw
        o['qb'], o['kb'], o['vb'] = 3 * bw, 4 * bw, 5 * bw
        o['qc'] = 6 * bw
        o['kc'] = 7 * bw
        o['vc'] = 7 * bw + LANES
        o['qd'], o['kd'], o['vd'] = 7 * bw + 2 * LANES, 8 * bw + 2 * LANES, 9 * bw + 2 * LANES
        o['gates'] = 10 * bw + 2 * LANES
        self.orig = o
        gate_blocks = N_BRANCH * d_model // LANES
        self.GATE0 = 0
        b = gate_blocks
        self.QA0, self.KA0, self.VA0 = b, b + hb, b + 2 * hb
        b += 3 * hb
        self.QD0, self.KD0, self.VD0 = b, b + hb, b + 2 * hb
        b += 3 * hb
        self.QB0, self.KB0, self.VB0 = b, b + hb, b + 2 * hb
        b += 3 * hb
        self.QC0 = b
        self.KC0 = b + hb
        self.VC0 = b + hb + 2
        b += hb + 3
        self.n_real = b
        self.chunks_per_tile = 12
        self.n_blocks = -(-b // self.chunks_per_tile) * self.chunks_per_tile
        self.nz = self.n_blocks * LANES
        self.first_norm_tile = gate_blocks // self.chunks_per_tile
        assert self.QC0 % (C_HEADS // C_KV_HEADS // 2) == 0
        norm = [0] * self.n_blocks
        for blk0, nblk, group in ((self.QA0, 2 * hb, bw // A_HEADS), (self.QD0, 2 * hb, bw // D_HEADS),
                                  (self.QB0, 2 * hb, bw // (2 * B_HEADS)),
                                  (self.QC0, hb + 2, bw // C_HEADS)):
            assert group in (LANES, HALF)
            norm[blk0:blk0 + nblk] = [group] * nblk
        self.norm_group = tuple(norm)

    def weight_segments(self):
        o, bw = self.orig, self.bw
        kc0, kc1 = o['kc'], o['kc'] + HALF
        segs = [(o['gates'], N_BRANCH * N_BRANCH * bw),
                (o['qa'], bw), (o['ka'], bw), (o['va'], bw),
                (o['qd'], bw), (o['kd'], bw), (o['vd'], bw),
                (o['qb'], bw), (o['kb'], bw), (o['vb'], bw),
                (o['qc'], bw),
                (kc0, HALF), (kc0, HALF), (kc1, HALF), (kc1, HALF),
                (o['vc'], LANES)]
        pad = self.nz - self.n_real * LANES
        if pad:
            segs.append((None, pad))
        return segs


def _build_wz(w_in, lay):
    parts = []
    for start, width in lay.weight_segments():
        if start is None:
            parts.append(jnp.zeros(w_in.shape[:-1] + (width,), w_in.dtype))
        else:
            parts.append(w_in[..., start:start + width])
    return jnp.concatenate(parts, axis=-1).astype(BF16)


def _build_coef(lay, qk_a, qk_b, qk_c, qk_d):
    hb = lay.bw // LANES
    a_hd = lay.bw // A_HEADS
    b_dk = lay.bw // (2 * B_HEADS)
    c_hd = lay.bw // C_HEADS
    d_hd = lay.bw // D_HEADS
    gain = jnp.ones((lay.nz,), F32)

    def put(gain, blk0, nblk, g, scale):
        lo, hi = blk0 * LANES, (blk0 + nblk) * LANES
        reps = (hi - lo) // g.shape[0]
        return gain.at[lo:hi].set(jnp.tile(g.astype(F32), reps) * scale)

    gain = put(gain, lay.QA0, hb, qk_a[0], a_hd ** -0.5 * _LOG2E)
    gain = put(gain, lay.KA0, hb, qk_a[1], 1.0)
    gain = put(gain, lay.QD0, hb, qk_d[0], d_hd ** -0.5 * _LOG2E)
    gain = put(gain, lay.KD0, hb, qk_d[1], 1.0)
    gain = put(gain, lay.QB0, hb, qk_b[0], b_dk ** -0.5 * _LOG2E)
    gain = put(gain, lay.KB0, hb, qk_b[1], 1.0)
    gain = put(gain, lay.QC0, hb, qk_c[0], c_hd ** -0.5 * _LOG2E)
    gain = put(gain, lay.KC0, 2, qk_c[1], 1.0)
    return gain.reshape(1, lay.nz)


def _ada_kernel(c_ref, w_ref, b_ref, o_ref):
    c = c_ref[...]
    cond = (c * _sigmoid(c)).astype(BF16)
    o_ref[0] = jnp.dot(cond, w_ref[0].astype(BF16), preferred_element_type=F32) + b_ref[0]


def _ada_mod(c, ada_w, ada_b, tn=1024):
    depth, d, n = ada_w.shape
    rows = 8
    cp = jnp.zeros((rows, d), F32).at[:c.shape[0]].set(c)
    tn = min(tn, n)
    return pl.pallas_call(
        _ada_kernel,
        out_shape=jax.ShapeDtypeStruct((depth, rows, n), F32),
        grid=(depth, n // tn),
        in_specs=[pl.BlockSpec((rows, d), lambda l, j: (0, 0)),
                  pl.BlockSpec((1, d, tn), lambda l, j: (l, 0, j)),
                  pl.BlockSpec((1, 1, tn), lambda l, j: (l, 0, j))],
        out_specs=pl.BlockSpec((1, rows, tn), lambda l, j: (l, 0, j)),
        compiler_params=_params("parallel", "parallel"),
        name="ada_mod",
    )(cp, ada_w, ada_b.reshape(depth, 1, n))


def _mod_norm(x, g, shift, scale):
    y = x * lax.rsqrt(jnp.mean(x * x, axis=-1, keepdims=True) + RMS_EPS) * g
    return y * (1.0 + scale) + shift


def _inproj_kernel(x_ref, mod_ref, g_ref, w_ref, gain_ref, z_ref, h_scr, *, first_norm_tile, n_chunks,
                   norm_group):
    j = pl.program_id(1)

    @pl.when(j == 0)
    def _():
        h = _mod_norm(x_ref[...], g_ref[...], mod_ref[0, 0:1, :], mod_ref[0, 1:2, :])
        h_scr[...] = h.astype(BF16)

    acc = jnp.dot(h_scr[...], w_ref[0], preferred_element_type=F32)

    @pl.when(j < first_norm_tile)
    def _():
        z_ref[...] = acc.astype(BF16)

    for tile in range(first_norm_tile, len(norm_group) // n_chunks):
        @pl.when(j == tile)
        def _(tile=tile):
            lo_lane = lax.broadcasted_iota(jnp.int32, (1, LANES), 1) < HALF
            for c in range(n_chunks):
                sl = slice(c * LANES, (c + 1) * LANES)
                group = norm_group[tile * n_chunks + c]
                y = acc[:, sl]
                if group == 0:
                    z_ref[:, sl] = y.astype(BF16)
                    continue
                y2 = y * y
                if group == LANES:
                    ss = jnp.sum(y2, axis=-1, keepdims=True)
                else:
                    lo = jnp.sum(jnp.where(lo_lane, y2, 0.0), axis=-1, keepdims=True)
                    hi = jnp.sum(jnp.where(lo_lane, 0.0, y2), axis=-1, keepdims=True)
                    ss = jnp.where(lo_lane, lo, hi)
                r = lax.rsqrt(ss * (1.0 / group) + RMS_EPS) * gain_ref[0, :, sl]
                z_ref[:, sl] = (y * r).astype(BF16)


def _inproj(x2, mod, g, wz, coef, layer, lay, seq, tm=1024):
    t, d = x2.shape
    tm = min(tm, seq)
    tn = lay.chunks_per_tile * LANES
    per_batch = seq // tm
    kern = functools.partial(_inproj_kernel, first_norm_tile=lay.first_norm_tile,
                             n_chunks=lay.chunks_per_tile, norm_group=lay.norm_group)
    return pl.pallas_call(
        kern,
        out_shape=jax.ShapeDtypeStruct((t, lay.nz), BF16),
        grid=(t // tm, lay.nz // tn),
        in_specs=[pl.BlockSpec((tm, d), lambda i, j: (i, 0)),
                  pl.BlockSpec((1, 6, d), lambda i, j: (i // per_batch, 0, 0)),
                  pl.BlockSpec((1, d), lambda i, j: (0, 0)),
                  pl.BlockSpec((1, d, tn), lambda i, j: (layer, 0, j)),
                  pl.BlockSpec((1, 1, tn), lambda i, j: (layer, 0, j))],
        out_specs=pl.BlockSpec((tm, tn), lambda i, j: (i, j)),
        scratch_shapes=[pltpu.VMEM((tm, d), BF16)],
        compiler_params=_params("parallel", "arbitrary"),
        name="inproj",
    )(x2, mod, g, wz, coef)


def _band_kernel(sink_ref, q_ref, k_ref, v_ref, tab_ref, qmask_ref, o_ref, *, n_side, ck, nkc, hp, q_of,
                 nsub):
    g = pl.program_id(1)
    i = pl.program_id(2)
    for u in range(nsub):
        qi = i * nsub + u
        rows = slice(u * ck, (u + 1) * ck)
        chunks = []
        for d in range(-n_side, n_side + 1):
            kc = qi + d
            valid = jnp.logical_and(kc >= 0, kc < nkc)
            start = pl.multiple_of(jnp.clip(kc, 0, nkc - 1) * ck, ck)
            chunks.append((d + n_side, start, jnp.where(valid, 0.0, NEG_INF)))
        for h in range(hp):
            q = q_ref[rows, q_of(h) * LANES:(q_of(h) + 1) * LANES] * qmask_ref[h]
            sink = sink_ref[g * hp + h]
            m = jnp.full((ck, 1), sink, F32)
            scores = []
            for di, start, penalty in chunks:
                s = lax.dot_general(q, k_ref[pl.ds(start, ck), :], _NT, preferred_element_type=F32)
                s = s + tab_ref[h, di] + penalty
                m = jnp.maximum(m, jnp.max(s, axis=-1, keepdims=True))
                scores.append(s)
            den = jnp.exp2(sink - m)
            acc = jnp.zeros((ck, LANES), F32)
            for s, (di, start, penalty) in zip(scores, chunks):
                p = jnp.exp2(s - m)
                den = den + jnp.sum(p, axis=-1, keepdims=True)
                acc = acc + jnp.dot(p.astype(BF16), v_ref[pl.ds(start, ck), :],
                                    preferred_element_type=F32)
            o_ref[rows, h * LANES:(h + 1) * LANES] = (acc / den).astype(BF16)


def _band_attention(z, table, qmask, sink, *, batch, seq, n_groups, hp, q_width, q_of, qblk, kblk, vblk,
                    n_side, nsub=4):
    t = z.shape[0]
    ck = table.shape[-1]
    nsub = min(nsub, seq // ck)
    tq = nsub * ck
    nq = seq // tq
    kern = functools.partial(_band_kernel, n_side=n_side, ck=ck, nkc=seq // ck, hp=hp, q_of=q_of, nsub=nsub)
    nd = 2 * n_side + 1
    return pl.pallas_call(
        kern,
        out_shape=jax.ShapeDtypeStruct((t, n_groups * hp * LANES), BF16),
        grid=(batch, n_groups, nq),
        in_specs=[pl.BlockSpec(memory_space=pltpu.SMEM),
                  pl.BlockSpec((tq, q_width), lambda b, g, i: (b * nq + i, qblk(g))),
                  pl.BlockSpec((seq, LANES), lambda b, g, i: (b, kblk(g))),
                  pl.BlockSpec((seq, LANES), lambda b, g, i: (b, vblk(g))),
                  pl.BlockSpec((hp, nd, ck, ck), lambda b, g, i: (g, 0, 0, 0)),
                  pl.BlockSpec((hp, 1, LANES), lambda b, g, i: (g, 0, 0))],
        out_specs=pl.BlockSpec((tq, hp * LANES), lambda b, g, i: (b * nq + i, g)),
        compiler_params=_params("parallel", "parallel", "arbitrary"),
        name="band_attention",
    )(sink, z, z, z, table, qmask)


def _alibi_slopes(n):
    return np.array([2.0 ** (-8.0 * (i + 1) / n) for i in range(n)], dtype=np.float32)


def _dilated_table(tq):
    reach = max(w // 2 for w, _ in A_PATTERNS)
    n_side = -(-reach // tq)
    d = np.arange(-n_side, n_side + 1)[:, None, None] * tq
    off = d + np.arange(tq)[None, None, :] - np.arange(tq)[None, :, None]
    mult = np.zeros(off.shape, np.float64)
    for window, dil in A_PATTERNS:
        mult += ((off % dil) == 0) & (np.abs(off) <= window // 2)
    logm = np.where(mult > 0, np.log(np.maximum(mult, 1.0)), 0.0)
    slopes = _alibi_slopes(A_HEADS).astype(np.float64)
    tab = (-slopes[:, None, None, None] * np.abs(off)[None] + logm[None]) * _LOG2E
    tab = np.where(mult[None] > 0, tab, NEG_INF)
    return jnp.asarray(tab.astype(np.float32)), n_side


def _window_table(tq):
    n_side = -(-C_RADIUS // tq)
    d = np.arange(-n_side, n_side + 1)[:, None, None] * tq
    off = d + np.arange(tq)[None, None, :] - np.arange(tq)[None, :, None]
    slopes = _alibi_slopes(C_HEADS).astype(np.float64)
    tab = -slopes[:, None, None, None] * np.abs(off)[None] * _LOG2E
    tab = np.where((np.abs(off) <= C_RADIUS)[None], tab, NEG_INF)
    return jnp.asarray(tab.astype(np.float32)), n_side


_D_QROWS = 8
_D_KROWS = 16


_D_HEADS_PER_STEP = 2


def _nbr_kernel(q_ref, k_ref, v_ref, tab_ref, o_ref, *, rows):
    t = pl.program_id(2)
    krow = jnp.clip(t * _D_QROWS - D_WIN_R // 2, 0, rows - _D_KROWS)
    start = pl.multiple_of(krow * GRID_W, GRID_W * (D_WIN_R // 2))
    nk = _D_KROWS * GRID_W
    for h in range(_D_HEADS_PER_STEP):
        lanes = slice(h * LANES, (h + 1) * LANES)
        s = lax.dot_general(q_ref[:, lanes], k_ref[pl.ds(start, nk), lanes], _NT,
                            preferred_element_type=F32)
        s = s + tab_ref[h, 0]
        m = jnp.max(s, axis=-1, keepdims=True)
        p = jnp.exp2(s - m)
        den = jnp.sum(p, axis=-1, keepdims=True)
        acc = jnp.dot(p.astype(BF16), v_ref[pl.ds(start, nk), lanes], preferred_element_type=F32)
        o_ref[:, lanes] = (acc / den).astype(BF16)


def _nbr_tables(rpb, rows):
    depth, heads, n_dr, n_dc = rpb.shape
    nt = rows // _D_QROWS
    wr = min(D_WIN_R, rows)
    pad_l = GRID_W - 1 - (D_WIN_C - 1)
    vpad = jnp.pad(rpb.astype(F32), ((0, 0), (0, 0), (0, 0), (pad_l, 2 * GRID_W - 1 - n_dc - pad_l)))
    toep = jnp.stack([vpad[..., GRID_W - 1 - c:2 * GRID_W - 1 - c] for c in range(GRID_W)], axis=-2)
    c = np.arange(GRID_W)
    cs = np.clip(c - D_WIN_C // 2, 0, GRID_W - D_WIN_C)
    col_ok = (c[None, :] >= cs[:, None]) & (c[None, :] < cs[:, None] + D_WIN_C)
    colmat = jnp.where(col_ok, toep * _LOG2E, NEG_INF)
    masked = jnp.full((depth, heads, 1, GRID_W, GRID_W), NEG_INF, F32)
    colmat = jnp.concatenate([colmat, masked], axis=2)
    sel = np.full((3, _D_QROWS, _D_KROWS), n_dr, np.int32)
    for kind, t in enumerate((0, 1, nt - 1)):
        krow0 = int(np.clip(t * _D_QROWS - D_WIN_R // 2, 0, rows - _D_KROWS))
        for ri in range(_D_QROWS):
            r = t * _D_QROWS + ri
            rs = int(np.clip(r - wr // 2, 0, rows - wr))
            for kj in range(_D_KROWS):
                kr = krow0 + kj
                if rs <= kr < rs + wr:
                    sel[kind, ri, kj] = kr - r + (D_WIN_R - 1)
    blocks = jnp.take(colmat, jnp.asarray(sel.reshape(-1)), axis=2)
    blocks = blocks.reshape(depth * heads, 3, _D_QROWS, _D_KROWS, GRID_W, GRID_W)
    blocks = blocks.transpose(0, 1, 2, 4, 3, 5)
    return blocks.reshape(depth * heads, 3, _D_QROWS * GRID_W, _D_KROWS * GRID_W)


def _nbr_attention(z, table, layer, *, batch, seq, lay):
    t = z.shape[0]
    rows = seq // GRID_W
    nt = rows // _D_QROWS
    tq = _D_QROWS * GRID_W
    nk = _D_KROWS * GRID_W
    kern = functools.partial(_nbr_kernel, rows=rows)
    hp = _D_HEADS_PER_STEP
    groups = D_HEADS // hp
    width = hp * LANES
    assert lay.QD0 % hp == 0 and lay.KD0 % hp == 0 and lay.VD0 % hp == 0
    kind = lambda i: jnp.where(i == 0, 0, jnp.where(i == nt - 1, 2, 1))
    return pl.pallas_call(
        kern,
        out_shape=jax.ShapeDtypeStruct((t, D_HEADS * LANES), BF16),
        grid=(batch, groups, nt),
        in_specs=[pl.BlockSpec((tq, width), lambda b, g, i: (b * nt + i, lay.QD0 // hp + g)),
                  pl.BlockSpec((seq, width), lambda b, g, i: (b, lay.KD0 // hp + g)),
                  pl.BlockSpec((seq, width), lambda b, g, i: (b, lay.VD0 // hp + g)),
                  pl.BlockSpec((hp, 1, tq, nk), lambda b, g, i: (layer * groups + g, kind(i), 0, 0))],
        out_specs=pl.BlockSpec((tq, width), lambda b, g, i: (b * nt + i, g)),
        compiler_params=_params("parallel", "parallel", "arbitrary"),
        name="nbr_attention",
    )(z, z, z, table)


_N_BIAS_PIECES = 3
_KPOS_RADIX = 64


def _bf16_round(v):
    u = np.asarray(v, np.float32).view(np.uint32)
    u = (u + np.uint32(0x7FFF) + ((u >> np.uint32(16)) & np.uint32(1))) & np.uint32(0xFFFF0000)
    return u.view(np.float32)


def _diff_feature_tables(tq, tk):
    c32 = (_alibi_slopes(B_HEADS).astype(np.float64) * np.log2(np.e)).astype(np.float32)
    qconst = np.zeros((B_HEADS, 2, LANES), np.float32)
    for h in range(B_HEADS):
        rem = np.float32(c32[h])
        pieces = []
        for _ in range(_N_BIAS_PIECES):
            p = np.float32(_bf16_round(rem))
            pieces.append(p)
            rem = np.float32(rem - p)
        for br, base in ((0, HALF), (1, 0)):
            for n, p in enumerate(pieces):
                qconst[h, br, base + n] = -_KPOS_RADIX * p
                qconst[h, br, base + _N_BIAS_PIECES + n] = -p
    j = np.arange(tk)
    kfeat = np.zeros((2, tk, LANES), np.float32)
    for br, base in ((0, HALF), (1, 0)):
        for n in range(_N_BIAS_PIECES):
            kfeat[br, :, base + n] = j // _KPOS_RADIX
            kfeat[br, :, base + _N_BIAS_PIECES + n] = j % _KPOS_RADIX
            kfeat[br, :, base + 2 * _N_BIAS_PIECES + n] = 1.0
    dist = np.abs(np.arange(tq)[None, :] - np.arange(tq)[:, None]).astype(np.float64)
    ttab = (-c32.astype(np.float64)[:, None, None] * dist[None]).astype(np.float32)
    return jnp.asarray(qconst), jnp.asarray(kfeat, BF16), jnp.asarray(c32), jnp.asarray(ttab)


_NARROW_CHUNKS = 3
_UNDERFLOW_MARGIN = 200.0


def _diff_kernel(c_ref, misc_ref, q_ref, k_ref, v_ref, qconst_ref, kfeat_ref, ttab_ref, dl_ref, subg_ref, o_ref,
                 kaug_scr, kmax_scr, *, tq, tk, nk):
    h = pl.program_id(1)
    i = pl.program_id(2)
    c = c_ref[h]
    lam_init = misc_ref[0]
    lane = lax.broadcasted_iota(jnp.int32, (1, LANES), 1)
    lo_lane = lane < HALF

    def half_sums(v):
        lo = jnp.sum(jnp.where(lo_lane, v, 0.0), axis=-1, keepdims=True)
        hi = jnp.sum(jnp.where(lo_lane, 0.0, v), axis=-1, keepdims=True)
        return lo, hi

    @pl.when(i == 0)
    def _():
        def aug(cc, kmax2):
            sl = pl.ds(pl.multiple_of(cc * tk, tk), tk)
            kb = k_ref[sl, :]
            kaug_scr[0, sl, :] = jnp.where(lo_lane, kb, kfeat_ref[0])
            kaug_scr[1, sl, :] = jnp.where(lo_lane, kfeat_ref[1], kb)
            kf = kb.astype(F32)
            lo, hi = half_sums(kf * kf)
            return jnp.maximum(kmax2, jnp.max(jnp.maximum(lo, hi), axis=0, keepdims=True))
        kmax2 = lax.fori_loop(0, nk, aug, jnp.zeros((1, 1), F32))
        kmax_scr[...] = jnp.broadcast_to(kmax2, kmax_scr.shape)

    q = q_ref[...].astype(F32)
    ci = c * lax.broadcasted_iota(jnp.int32, (tq, 1), 0).astype(F32)
    r1 = ci.astype(BF16).astype(F32)
    r2 = (ci - r1).astype(BF16).astype(F32)
    r3 = (ci - r1 - r2).astype(BF16).astype(F32)
    q_plain, q_right, q_left = [], [], []
    for br, base in ((0, HALF), (1, 0)):
        keep = lo_lane if br == 0 else jnp.logical_not(lo_lane)
        f0 = base + 2 * _N_BIAS_PIECES
        feat = qconst_ref[0, br:br + 1, :] + jnp.where(
            lane == f0, r1, jnp.where(lane == f0 + 1, r2, jnp.where(lane == f0 + 2, r3, 0.0)))
        q_plain.append(jnp.where(keep, q, 0.0).astype(BF16))
        q_right.append(jnp.where(keep, q, feat).astype(BF16))
        q_left.append(jnp.where(keep, q, -feat).astype(BF16))

    def update(st, s, shift, vb):
        m_old, l_old, acc_old = st
        m_new = jnp.maximum(m_old, jnp.max(s, axis=-1, keepdims=True) + shift)
        alpha = jnp.exp2(m_old - m_new)
        p = jnp.exp2(s + (shift - m_new))
        l_new = alpha * l_old + jnp.sum(p, axis=-1, keepdims=True)
        acc_new = alpha * acc_old + jnp.dot(p.astype(BF16), vb, preferred_element_type=F32)
        return m_new, l_new, acc_new

    cdiag = (i * tq) // tk

    def side_block(rows, on_right, base, st):
        ss = [lax.dot_general(jnp.where(on_right, q_right[idx], q_left[idx]), kaug_scr[idx, rows, :], _NT,
                              preferred_element_type=F32) for idx in range(2)]
        shift = jnp.where(on_right, -1.0, 1.0) * c * (base - i * tq).astype(F32)
        vb = v_ref[rows, :]
        return [update(st[idx], ss[idx], shift, vb) for idx in range(2)]

    def chunk_rows(cc):
        return pl.ds(pl.multiple_of(cc * tk, tk), tk)

    def diag_chunk():
        st = [(jnp.full((tq, 1), NEG_INF, F32), jnp.zeros((tq, 1), F32), jnp.zeros((tq, LANES), F32))
              for _ in range(2)]
        n_sub = tk // tq
        usq = i % n_sub
        dbase = cdiag * tk
        sq = pl.ds(pl.multiple_of(dbase + usq * tq, tq), tq)
        ss = [lax.dot_general(q_plain[idx], kaug_scr[idx, sq, :], _NT, preferred_element_type=F32)
              + ttab_ref[0] for idx in range(2)]
        vb = v_ref[sq, :]
        st = [update(st[idx], ss[idx], 0.0, vb) for idx in range(2)]
        for w in range(1, n_sub):
            uo = usq + w
            on_right = uo < n_sub
            uo = jnp.where(on_right, uo, uo - n_sub)
            rows = pl.ds(pl.multiple_of(dbase + uo * tq, tq), tq)
            st = side_block(rows, on_right, dbase, st)
        return st

    def finish(st):
        hs = pl.ds(h, 1)
        lam = (jnp.exp(jnp.sum(dl_ref[0, hs, :] * dl_ref[1, hs, :], axis=-1, keepdims=True))
               - jnp.exp(jnp.sum(dl_ref[2, hs, :] * dl_ref[3, hs, :], axis=-1, keepdims=True)) + lam_init)
        o = st[0][2] / st[0][1] - lam * (st[1][2] / st[1][1])
        o = o * lax.rsqrt(jnp.mean(o * o, axis=-1, keepdims=True) + RMS_EPS) * subg_ref[...]
        o_ref[...] = (o * (1.0 - lam_init)).astype(BF16)

    def all_chunks():
        st = diag_chunk()
        for step in range(1, nk):
            ct = cdiag + step
            on_right = ct < nk
            cc = jnp.where(on_right, ct, ct - nk)
            st = side_block(chunk_rows(cc), on_right, cc * tk, st)
        finish(st)

    if nk <= _NARROW_CHUNKS:
        all_chunks()
        return

    kd = k_ref[pl.ds(pl.multiple_of(i * tq, tq), tq), :].astype(F32)
    qlo, qhi = half_sums(q * q)
    dlo, dhi = half_sums(q * kd)
    kmax = jnp.sqrt(kmax_scr[0:1, 0:1])
    spread = jnp.sqrt(jnp.maximum(qlo, qhi)) * kmax - jnp.minimum(dlo, dhi)
    need = jnp.max(spread, axis=0, keepdims=True)[0, 0] + _UNDERFLOW_MARGIN
    start = jnp.clip(cdiag - 1, 0, nk - _NARROW_CHUNKS)
    far = 1e9
    d_left = jnp.where(start > 0, (i * tq - start * tk).astype(F32), far)
    d_right = jnp.where(start + _NARROW_CHUNKS < nk,
                        ((start + _NARROW_CHUNKS) * tk - (i + 1) * tq).astype(F32), far)
    narrow = c * jnp.minimum(d_left, d_right) >= need

    @pl.when(narrow)
    def _():
        st = diag_chunk()
        for w in range(_NARROW_CHUNKS - 1):
            cw = start + w
            cc = jnp.where(cw >= cdiag, cw + 1, cw)
            st = side_block(chunk_rows(cc), cc > cdiag, cc * tk, st)
        finish(st)

    @pl.when(jnp.logical_not(narrow))
    def _():
        all_chunks()


def _diff_attention(z, dl, subg, tables, misc, *, batch, seq, lay, tq, tk):
    t = z.shape[0]
    nq = seq // tq
    qconst, kfeat, cvals, ttab = tables
    kern = functools.partial(_diff_kernel, tq=tq, tk=tk, nk=seq // tk)
    return pl.pallas_call(
        kern,
        out_shape=jax.ShapeDtypeStruct((t, B_HEADS * LANES), BF16),
        grid=(batch, B_HEADS, nq),
        in_specs=[pl.BlockSpec(memory_space=pltpu.SMEM),
                  pl.BlockSpec(memory_space=pltpu.SMEM),
                  pl.BlockSpec((tq, LANES), lambda b, h, i: (b * nq + i, lay.QB0 + h)),
                  pl.BlockSpec((seq, LANES), lambda b, h, i: (b, lay.KB0 + h)),
                  pl.BlockSpec((seq, LANES), lambda b, h, i: (b, lay.VB0 + h)),
                  pl.BlockSpec((1, 2, LANES), lambda b, h, i: (h, 0, 0)),
                  pl.BlockSpec(kfeat.shape, lambda b, h, i: (0, 0, 0)),
                  pl.BlockSpec((1, tq, tq), lambda b, h, i: (h, 0, 0)),
                  pl.BlockSpec(dl.shape, lambda b, h, i: (0, 0, 0)),
                  pl.BlockSpec((1, LANES), lambda b, h, i: (0, 0))],
        out_specs=pl.BlockSpec((tq, LANES), lambda b, h, i: (b * nq + i, h)),
        scratch_shapes=[pltpu.VMEM((2, seq, LANES), BF16), pltpu.VMEM((8, LANES), F32)],
        compiler_params=_params("parallel", "arbitrary", "arbitrary"),
        name="diff_attention",
    )(cvals, misc, z, z, z, qconst, kfeat, ttab, dl, subg)


def _merge_kernel(oa_ref, ob_ref, oc_ref, od_ref, gate_ref, wa_ref, wb_ref, wc_ref, wd_ref, o_ref, *, d):
    acc = None
    for idx, (o, w) in enumerate(((oa_ref, wa_ref), (ob_ref, wb_ref), (oc_ref, wc_ref), (od_ref, wd_ref))):
        y = jnp.dot(o[...], w[...], preferred_element_type=F32)
        g = _sigmoid(gate_ref[:, idx * d:(idx + 1) * d].astype(F32))
        acc = g * y if acc is None else acc + g * y
    o_ref[...] = acc.astype(BF16)


def _merge(oa, ob, oc, od, z, wbs, *, d, tm=256):
    t = oa.shape[0]
    tm = min(tm, t)
    row = lambda a: pl.BlockSpec((tm, a.shape[1]), lambda i: (i, 0))
    full = lambda a: pl.BlockSpec(a.shape, lambda i: (0, 0))
    return pl.pallas_call(
        functools.partial(_merge_kernel, d=d),
        out_shape=jax.ShapeDtypeStruct((t, d), BF16),
        grid=(t // tm,),
        in_specs=[row(oa), row(ob), row(oc), row(od),
                  pl.BlockSpec((tm, N_BRANCH * d), lambda i: (i, 0))] + [full(w) for w in wbs],
        out_specs=pl.BlockSpec((tm, d), lambda i: (i, 0)),
        compiler_params=_params("parallel"),
        name="branch_merge",
    )(oa, ob, oc, od, z, *wbs)


def _outproj_kernel(a_ref, w_ref, x_ref, g_ref, o_ref):
    y = jnp.dot(a_ref[...], w_ref[0], preferred_element_type=F32)
    o_ref[...] = x_ref[...] + g_ref[0] * y


def _outproj(a, w, layer, x2, gate, seq, tm=512):
    t, d = x2.shape
    tm = min(tm, seq)
    per_batch = seq // tm
    return pl.pallas_call(
        _outproj_kernel,
        out_shape=jax.ShapeDtypeStruct((t, d), F32),
        grid=(t // tm,),
        in_specs=[pl.BlockSpec((tm, a.shape[1]), lambda i: (i, 0)),
                  pl.BlockSpec((1,) + w.shape[1:], lambda i: (layer, 0, 0)),
                  pl.BlockSpec((tm, d), lambda i: (i, 0)),
                  pl.BlockSpec((1, 1, d), lambda i: (i // per_batch, 0, 0))],
        out_specs=pl.BlockSpec((tm, d), lambda i: (i, 0)),
        compiler_params=_params("parallel"),
        name="out_proj",
    )(a, w, x2, gate)


def _router_kernel(x_ref, mod_ref, g_ref, rwt_ref, rb_ref, h_ref, e_ref, gw_ref):
    h = _mod_norm(x_ref[...], g_ref[...], mod_ref[0, 3:4, :], mod_ref[0, 4:5, :])
    h_ref[...] = h
    logits = lax.dot_general(rwt_ref[...], h.astype(BF16), _NT, preferred_element_type=F32)
    score = _sigmoid(logits)
    sel = score + rb_ref[...]
    sel_r = [sel[e:e + 1, :] for e in range(N_EXPERTS)]
    sc_r = [score[e:e + 1, :] for e in range(N_EXPERTS)]
    n = EXPERTS_PER_GROUP

    def top2_sum(v):
        best = None
        for a in range(n):
            for b in range(a + 1, n):
                s = v[a] + v[b]
                best = s if best is None else jnp.maximum(best, s)
        return best

    gbest = top2_sum(sel_r[0:n])
    gidx = jnp.zeros(gbest.shape, jnp.int32)
    cv = list(sel_r[0:n])
    cs = list(sc_r[0:n])
    for g in range(1, N_GROUPS):
        gs = top2_sum(sel_r[g * n:(g + 1) * n])
        upd = gs > gbest
        gbest = jnp.where(upd, gs, gbest)
        gidx = jnp.where(upd, g, gidx)
        for j in range(n):
            cv[j] = jnp.where(upd, sel_r[g * n + j], cv[j])
            cs[j] = jnp.where(upd, sc_r[g * n + j], cs[j])
    b1, i1, s1 = cv[0], jnp.zeros(gbest.shape, jnp.int32), cs[0]
    for j in range(1, n):
        upd = cv[j] > b1
        b1 = jnp.where(upd, cv[j], b1)
        i1 = jnp.where(upd, j, i1)
        s1 = jnp.where(upd, cs[j], s1)
    low = -3.0e38
    b2 = jnp.full(gbest.shape, low, F32)
    i2 = jnp.zeros(gbest.shape, jnp.int32)
    s2 = jnp.zeros(gbest.shape, F32)
    for j in range(n):
        cand = jnp.where(i1 == j, low, cv[j])
        upd = cand > b2
        b2 = jnp.where(upd, cand, b2)
        i2 = jnp.where(upd, j, i2)
        s2 = jnp.where(upd, cs[j], s2)
    tot = s1 + s2
    e_ref[0:1, :] = gidx * n + i1
    e_ref[1:2, :] = gidx * n + i2
    gw_ref[0:1, :] = s1 / tot
    gw_ref[1:2, :] = s2 / tot


def _router(x2, mod, g, rwt, rb, seq, tm=512):
    t, d = x2.shape
    tm = min(tm, seq)
    per_batch = seq // tm
    return pl.pallas_call(
        _router_kernel,
        out_shape=(jax.ShapeDtypeStruct((t, d), F32),
                   jax.ShapeDtypeStruct((2, t), jnp.int32),
                   jax.ShapeDtypeStruct((2, t), F32)),
        grid=(t // tm,),
        in_specs=[pl.BlockSpec((tm, d), lambda i: (i, 0)),
                  pl.BlockSpec((1, 6, d), lambda i: (i // per_batch, 0, 0)),
                  pl.BlockSpec((1, d), lambda i: (0, 0)),
                  pl.BlockSpec(rwt.shape, lambda i: (0, 0)),
                  pl.BlockSpec(rb.shape, lambda i: (0, 0))],
        out_specs=(pl.BlockSpec((tm, d), lambda i: (i, 0)),
                   pl.BlockSpec((2, tm), lambda i: (0, i)),
                   pl.BlockSpec((2, tm), lambda i: (0, i))),
        compiler_params=_params("parallel"),
        name="norm_router",
    )(x2, mod, g, rwt, rb)


def _rank_kernel(e_ref, rank_ref, cnt_ref, carry):
    @pl.when(pl.program_id(0) == 0)
    def _():
        carry[...] = jnp.zeros(carry.shape, F32)

    tm = e_ref.shape[1]
    eio = lax.broadcasted_iota(jnp.int32, (N_EXPERTS, tm), 0)
    oh0 = eio == e_ref[0:1, :]
    oh1 = eio == e_ref[1:2, :]
    c = jnp.where(jnp.logical_or(oh0, oh1), 1.0, 0.0)
    before = (lax.broadcasted_iota(jnp.int32, (tm, tm), 0) < lax.broadcasted_iota(jnp.int32, (tm, tm), 1))
    pref = jnp.dot(c.astype(BF16), jnp.where(before, 1.0, 0.0).astype(BF16),
                   preferred_element_type=F32) + carry[...]
    rank_ref[0:1, :] = jnp.sum(jnp.where(oh0, pref, 0.0), axis=0, keepdims=True).astype(jnp.int32)
    rank_ref[1:2, :] = jnp.sum(jnp.where(oh1, pref, 0.0), axis=0, keepdims=True).astype(jnp.int32)
    total = carry[...] + jnp.sum(c, axis=1, keepdims=True)
    carry[...] = total
    cnt_ref[...] = jnp.broadcast_to(total, cnt_ref.shape)


def _rank(eidx, tm=512):
    t = eidx.shape[1]
    tm = min(tm, t)
    return pl.pallas_call(
        _rank_kernel,
        out_shape=(jax.ShapeDtypeStruct((2, t), jnp.int32),
                   jax.ShapeDtypeStruct((N_EXPERTS, LANES), F32)),
        grid=(t // tm,),
        in_specs=[pl.BlockSpec((2, tm), lambda i: (0, i))],
        out_specs=(pl.BlockSpec((2, tm), lambda i: (0, i)),
                   pl.BlockSpec((N_EXPERTS, LANES), lambda i: (0, 0))),
        scratch_shapes=[pltpu.VMEM((N_EXPERTS, 1), F32)],
        compiler_params=_params("arbitrary"),
        name="expert_rank",
    )(eidx)


_DMA_UNROLL = 8


def _row_copy(src_hbm, row, dst, slot, r, sem):
    return pltpu.make_async_copy(src_hbm.at[pl.ds(row, 1), :], dst.at[slot, pl.ds(r, 1), :], sem.at[slot])


def _expert_kernel(te_ref, nu_ref, src_ref, h_hbm, w1_ref, w3_ref, w2_ref, y_ref, xbuf, sem, *, tm):
    j = pl.program_id(0)
    n_used = nu_ref[0]

    def wait_tile(slot):
        pltpu.make_async_copy(h_hbm.at[pl.ds(0, tm), :], xbuf.at[slot], sem.at[slot]).wait()

    @pl.when(jnp.logical_and(j == 0, n_used > 0))
    def _():
        def issue(r, carry):
            _row_copy(h_hbm, src_ref[r], xbuf, 0, r, sem).start()
            return carry
        lax.fori_loop(0, tm, issue, 0, unroll=_DMA_UNROLL)

    @pl.when(j < n_used)
    def _():
        slot = j % 2
        wait_tile(slot)
        nxt = jnp.minimum(j + 1, n_used - 1) * tm
        for r in range(tm):
            _row_copy(h_hbm, src_ref[nxt + r], xbuf, 1 - slot, r, sem).start()
        xs = xbuf[slot].astype(BF16)
        g = jnp.dot(xs, w1_ref[0], preferred_element_type=F32)
        u = jnp.dot(xs, w3_ref[0], preferred_element_type=F32)
        mid = (g * _sigmoid(g) * u).astype(BF16)
        y_ref[...] = jnp.dot(mid, w2_ref[0], preferred_element_type=F32)

    @pl.when(j == n_used - 1)
    def _():
        wait_tile(1 - j % 2)

    @pl.when(j >= n_used)
    def _():
        y_ref[...] = jnp.zeros(y_ref.shape, F32)


def _experts(tile_e, n_used, src, h, w1, w3, w2, layer, tm):
    p = src.shape[0]
    d = h.shape[1]
    f = w1.shape[-1]
    n_exp = w1.shape[1]
    w1r = w1.reshape((-1,) + w1.shape[2:])
    w3r = w3.reshape((-1,) + w3.shape[2:])
    w2r = w2.reshape((-1,) + w2.shape[2:])
    wmap = lambda j, te, nu, src: (layer * n_exp + te[j], 0, 0)
    return pl.pallas_call(
        functools.partial(_expert_kernel, tm=tm),
        out_shape=jax.ShapeDtypeStruct((p, d), F32),
        grid_spec=pltpu.PrefetchScalarGridSpec(
            num_scalar_prefetch=3,
            grid=(p // tm,),
            in_specs=[pl.BlockSpec(memory_space=pl.ANY),
                      pl.BlockSpec((1, d, f), wmap),
                      pl.BlockSpec((1, d, f), wmap),
                      pl.BlockSpec((1, f, d), wmap)],
            out_specs=pl.BlockSpec((tm, d), lambda j, te, nu, src: (j, 0)),
            scratch_shapes=[pltpu.VMEM((2, tm, d), F32), pltpu.SemaphoreType.DMA((2,))]),
        compiler_params=_params("arbitrary"),
        name="experts",
    )(tile_e, n_used, src, h, w1r, w3r, w2r)


def _combine_kernel(pos_ref, x_ref, g_ref, gw_ref, y_hbm, o_ref, buf0, buf1, sem0, sem1, *, tm, t, n_steps):
    i = pl.program_id(0)

    def start_step(step, slot):
        def issue(r, carry):
            _row_copy(y_hbm, pos_ref[step * tm + r], buf0, slot, r, sem0).start()
            _row_copy(y_hbm, pos_ref[t + step * tm + r], buf1, slot, r, sem1).start()
            return carry
        lax.fori_loop(0, tm, issue, 0, unroll=_DMA_UNROLL)

    def wait_step(slot):
        pltpu.make_async_copy(y_hbm.at[pl.ds(0, tm), :], buf0.at[slot], sem0.at[slot]).wait()
        pltpu.make_async_copy(y_hbm.at[pl.ds(0, tm), :], buf1.at[slot], sem1.at[slot]).wait()

    @pl.when(i == 0)
    def _():
        start_step(0, 0)

    @pl.when(i + 1 < n_steps)
    def _():
        start_step(i + 1, (i + 1) % 2)

    slot = i % 2
    wait_step(slot)
    moe = gw_ref[:, 0:1] * buf0[slot] + gw_ref[:, 1:2] * buf1[slot]
    o_ref[...] = x_ref[...] + g_ref[0] * moe


def _combine(pos, x2, gate, gwt, y, seq, tm=256):
    t, d = x2.shape
    tm = min(tm, seq)
    per_batch = seq // tm
    n_steps = t // tm
    return pl.pallas_call(
        functools.partial(_combine_kernel, tm=tm, t=t, n_steps=n_steps),
        out_shape=jax.ShapeDtypeStruct((t, d), F32),
        grid_spec=pltpu.PrefetchScalarGridSpec(
            num_scalar_prefetch=1,
            grid=(n_steps,),
            in_specs=[pl.BlockSpec((tm, d), lambda i, pos: (i, 0)),
                      pl.BlockSpec((1, 1, d), lambda i, pos: (i // per_batch, 0, 0)),
                      pl.BlockSpec((tm, 2), lambda i, pos: (i, 0)),
                      pl.BlockSpec(memory_space=pl.ANY)],
            out_specs=pl.BlockSpec((tm, d), lambda i, pos: (i, 0)),
            scratch_shapes=[pltpu.VMEM((2, tm, d), F32), pltpu.VMEM((2, tm, d), F32),
                            pltpu.SemaphoreType.DMA((2,)), pltpu.SemaphoreType.DMA((2,))]),
        compiler_params=_params("arbitrary"),
        name="moe_combine",
    )(pos, x2, gate, gwt, y)


def _moe(x2, mod, g2n, rwt, rb, w1, w3, w2, layer, *, batch, seq, tm_e=256):
    t, d = x2.shape
    h, eidx, gw = _router(x2, mod, g2n, rwt, rb, seq)
    rank, cnt = _rank(eidx)
    tm_e = min(tm_e, t)
    counts = cnt[:, 0].astype(jnp.int32)
    gsz = ((counts + tm_e - 1) // tm_e) * tm_e
    ends = jnp.cumsum(gsz)
    offs = ends - gsz
    pos = rank
    for e in range(N_EXPERTS):
        pos = pos + jnp.where(eidx == e, offs[e], 0)
    n_tiles = (2 * t) // tm_e + N_EXPERTS
    p = n_tiles * tm_e
    flat = pos.reshape(-1)
    tok = jnp.broadcast_to(jnp.arange(t, dtype=jnp.int32)[None, :], (2, t)).reshape(-1)
    src = jnp.zeros((p,), jnp.int32).at[flat].set(tok)
    tile_start = jnp.arange(n_tiles, dtype=jnp.int32) * tm_e
    tile_e = jnp.minimum(jnp.sum(tile_start[:, None] >= ends[None, :], axis=1),
                         N_EXPERTS - 1).astype(jnp.int32)
    n_used = (ends[-1:] // tm_e).astype(jnp.int32)
    y = _experts(tile_e, n_used, src, h, w1, w3, w2, layer, tm_e)
    return _combine(flat, x2, mod[:, 5:6, :], gw.T, y, seq)


def kernel(x, c, ada_w, ada_b, norm1_g, norm2_g, w_in, qk_norm_a, qk_norm_b, qk_norm_c, qk_norm_d,
           diff_lambda, diff_subln_g, sink_logits, rpb, w_branch, w_out, router_w, router_b, w1, w3, w2):
    batch, seq, d = x.shape
    depth = ada_w.shape[0]
    t = batch * seq
    lay = _Layout(d)
    bw = lay.bw

    mod_all = _ada_mod(c, ada_w, ada_b)
    wz = _build_wz(w_in, lay)
    coef = jnp.stack([_build_coef(lay, qk_norm_a[l], qk_norm_b[l], qk_norm_c[l], qk_norm_d[l])
                      for l in range(depth)])
    wb16 = w_branch.astype(BF16)
    c_hd = bw // C_HEADS
    grp = C_HEADS // C_KV_HEADS
    wc_rows = []
    for hq in range(C_HEADS):
        blk = jnp.zeros((depth, LANES, d), BF16)
        off = (hq // grp) * c_hd
        wc_rows.append(blk.at[:, off:off + c_hd].set(wb16[:, 2, hq * c_hd:(hq + 1) * c_hd]))
    wc_ext = jnp.concatenate(wc_rows, axis=1)
    wout16 = w_out.astype(BF16)
    w1b, w3b, w2b = w1.astype(BF16), w3.astype(BF16), w2.astype(BF16)
    rwt = router_w.T.astype(BF16)
    rb = router_b.reshape(N_EXPERTS, 1).astype(F32)

    tab_a, side_a = _dilated_table(256)
    tab_c, side_c = _window_table(256)
    ones_mask = jnp.ones((A_HEADS, 1, LANES), BF16)
    lane = np.arange(LANES)
    cmask = np.stack([(lane // HALF) == (hq % 2) for hq in range(C_HEADS)]).astype(np.float32)
    cmask = jnp.asarray(cmask.reshape(C_HEADS, 1, LANES), BF16)
    no_sink = jnp.full((A_HEADS,), NEG_INF, F32)
    tq_b, tk_b = min(512, seq), min(1024, seq)
    tables_b = _diff_feature_tables(tq_b, tk_b)
    tab_d = _nbr_tables(rpb, seq // GRID_W)

    x2 = x.reshape(t, d)
    for l in range(depth):
        mod = mod_all[l, :batch].reshape(batch, 6, d)
        z = _inproj(x2, mod, norm1_g[l].reshape(1, d), wz, coef, l, lay, seq)
        o_a = _band_attention(z, tab_a, ones_mask, no_sink, batch=batch, seq=seq, n_groups=A_HEADS, hp=1,
                              q_width=LANES, q_of=lambda h: 0,
                              qblk=lambda g: lay.QA0 + g, kblk=lambda g: lay.KA0 + g,
                              vblk=lambda g: lay.VA0 + g, n_side=side_a)
        lam_init = 0.8 - 0.6 * float(np.exp(-0.3 * l))
        o_b = _diff_attention(z, diff_lambda[l], diff_subln_g[l].reshape(1, LANES), tables_b,
                              jnp.full((1,), lam_init, F32), batch=batch, seq=seq, lay=lay,
                              tq=tq_b, tk=tk_b)
        o_c = _band_attention(z, tab_c, cmask, sink_logits[l].astype(F32) * _LOG2E, batch=batch, seq=seq,
                              n_groups=C_KV_HEADS, hp=grp, q_width=(grp // 2) * LANES,
                              q_of=lambda h: h // 2,
                              qblk=lambda g: lay.QC0 // (grp // 2) + g, kblk=lambda g: lay.KC0 + g,
                              vblk=lambda g: lay.VC0, n_side=side_c)
        o_d = _nbr_attention(z, tab_d, l, batch=batch, seq=seq, lay=lay)
        merged = _merge(o_a, o_b, o_c, o_d, z, (wb16[l, 0], wb16[l, 1], wc_ext[l], wb16[l, 3]), d=d)
        x2 = _outproj(merged, wout16, l, x2, mod[:, 2:3, :], seq)
        x2 = _moe(x2, mod, norm2_g[l].reshape(1, d), rwt, rb, w1b, w3b, w2b, l, batch=batch, seq=seq)
    return x2.reshape(batch, seq, d)
```

```python
import functools

import numpy as np
import jax
import jax.numpy as jnp
from jax import lax
from jax.experimental import pallas as pl
from jax.experimental.pallas import tpu as pltpu

F32 = jnp.float32
BF16 = jnp.bfloat16
LANES = 128
HALF = LANES // 2
RMS_EPS = 1e-6
NEG_INF = -1e30
_LOG2E = float(np.log2(np.e))
VMEM_LIMIT = 56 * 1024 * 1024

GRID_W = 64
N_BRANCH = 4
A_HEADS = 4
A_PATTERNS = ((128, 1), (512, 4), (2048, 16))
B_HEADS = 4
C_HEADS = 8
C_KV_HEADS = 2
C_RADIUS = 128
D_HEADS = 4
D_WIN_R = 8
D_WIN_C = 16
N_EXPERTS = 16
N_GROUPS = 4
EXPERTS_PER_GROUP = N_EXPERTS // N_GROUPS

_NT = (((1,), (1,)), ((), ()))


def _params(*sem):
    return pltpu.CompilerParams(dimension_semantics=sem, vmem_limit_bytes=VMEM_LIMIT)


def _sigmoid(v):
    return 1.0 / (1.0 + jnp.exp(-v))


class _Layout:
    def __init__(self, d_model):
        bw = d_model // N_BRANCH
        self.bw = bw
        hb = bw // LANES
        o = {}
        o['qa'], o['ka'], o['va'] = 0, bw, 2 * bw
        o['qb'], o['kb'], o['vb'] = 3 * bw, 4 * bw, 5 * bw
        o['qc'] = 6 * bw
        o['kc'] = 7 * bw
        o['vc'] = 7 * bw + LANES
        o['qd'], o['kd'], o['vd'] = 7 * bw + 2 * LANES, 8 * bw + 2 * LANES, 9 * bw + 2 * LANES
        o['gates'] = 10 * bw + 2 * LANES
        self.orig = o
        gate_blocks = N_BRANCH * d_model // LANES
        self.GATE0 = 0
        b = gate_blocks
        self.QA0, self.KA0, self.VA0 = b, b + hb, b + 2 * hb
        b += 3 * hb
        self.QD0, self.KD0, self.VD0 = b, b + hb, b + 2 * hb
        b += 3 * hb
        self.QB0, self.KB0, self.VB0 = b, b + hb, b + 2 * hb
        b += 3 * hb
        self.QC0 = b
        self.KC0 = b + hb
        self.VC0 = b + hb + 2
        b += hb + 3
        self.n_real = b
        self.chunks_per_tile = 12
        self.n_blocks = -(-b // self.chunks_per_tile) * self.chunks_per_tile
        self.nz = self.n_blocks * LANES
        self.first_norm_tile = gate_blocks // self.chunks_per_tile
        assert self.QC0 % (C_HEADS // C_KV_HEADS // 2) == 0
        norm = [0] * self.n_blocks
        for blk0, nblk, group in ((self.QA0, 2 * hb, bw // A_HEADS), (self.QD0, 2 * hb, bw // D_HEADS),
                                  (self.QB0, 2 * hb, bw // (2 * B_HEADS)),
                                  (self.QC0, hb + 2, bw // C_HEADS)):
            assert group in (LANES, HALF)
            norm[blk0:blk0 + nblk] = [group] * nblk
        self.norm_group = tuple(norm)

    def weight_segments(self):
        o, bw = self.orig, self.bw
        kc0, kc1 = o['kc'], o['kc'] + HALF
        segs = [(o['gates'], N_BRANCH * N_BRANCH * bw),
                (o['qa'], bw), (o['ka'], bw), (o['va'], bw),
                (o['qd'], bw), (o['kd'], bw), (o['vd'], bw),
                (o['qb'], bw), (o['kb'], bw), (o['vb'], bw),
                (o['qc'], bw),
                (kc0, HALF), (kc0, HALF), (kc1, HALF), (kc1, HALF),
                (o['vc'], LANES)]
        pad = self.nz - self.n_real * LANES
        if pad:
            segs.append((None, pad))
        return segs


def _build_wz(w_in, lay):
    parts = []
    for start, width in lay.weight_segments():
        if start is None:
            parts.append(jnp.zeros(w_in.shape[:-1] + (width,), w_in.dtype))
        else:
            parts.append(w_in[..., start:start + width])
    return jnp.concatenate(parts, axis=-1).astype(BF16)


def _build_coef(lay, qk_a, qk_b, qk_c, qk_d):
    hb = lay.bw // LANES
    a_hd = lay.bw // A_HEADS
    b_dk = lay.bw // (2 * B_HEADS)
    c_hd = lay.bw // C_HEADS
    d_hd = lay.bw // D_HEADS
    gain = jnp.ones((lay.nz,), F32)

    def put(gain, blk0, nblk, g, scale):
        lo, hi = blk0 * LANES, (blk0 + nblk) * LANES
        reps = (hi - lo) // g.shape[0]
        return gain.at[lo:hi].set(jnp.tile(g.astype(F32), reps) * scale)

    gain = put(gain, lay.QA0, hb, qk_a[0], a_hd ** -0.5 * _LOG2E)
    gain = put(gain, lay.KA0, hb, qk_a[1], 1.0)
    gain = put(gain, lay.QD0, hb, qk_d[0], d_hd ** -0.5 * _LOG2E)
    gain = put(gain, lay.KD0, hb, qk_d[1], 1.0)
    gain = put(gain, lay.QB0, hb, qk_b[0], b_dk ** -0.5 * _LOG2E)
    gain = put(gain, lay.KB0, hb, qk_b[1], 1.0)
    gain = put(gain, lay.QC0, hb, qk_c[0], c_hd ** -0.5 * _LOG2E)
    gain = put(gain, lay.KC0, 2, qk_c[1], 1.0)
    return gain.reshape(1, lay.nz)


def _ada_kernel(c_ref, w_ref, b_ref, o_ref):
    c = c_ref[...]
    cond = (c * _sigmoid(c)).astype(BF16)
    o_ref[0] = jnp.dot(cond, w_ref[0].astype(BF16), preferred_element_type=F32) + b_ref[0]


def _ada_mod(c, ada_w, ada_b, tn=1024):
    depth, d, n = ada_w.shape
    rows = 8
    cp = jnp.zeros((rows, d), F32).at[:c.shape[0]].set(c)
    tn = min(tn, n)
    return pl.pallas_call(
        _ada_kernel,
        out_shape=jax.ShapeDtypeStruct((depth, rows, n), F32),
        grid=(depth, n // tn),
        in_specs=[pl.BlockSpec((rows, d), lambda l, j: (0, 0)),
                  pl.BlockSpec((1, d, tn), lambda l, j: (l, 0, j)),
                  pl.BlockSpec((1, 1, tn), lambda l, j: (l, 0, j))],
        out_specs=pl.BlockSpec((1, rows, tn), lambda l, j: (l, 0, j)),
        compiler_params=_params("parallel", "parallel"),
        name="ada_mod",
    )(cp, ada_w, ada_b.reshape(depth, 1, n))


def _mod_norm(x, g, shift, scale):
    y = x * lax.rsqrt(jnp.mean(x * x, axis=-1, keepdims=True) + RMS_EPS) * g
    return y * (1.0 + scale) + shift


def _inproj_kernel(x_ref, mod_ref, g_ref, w_ref, gain_ref, z_ref, h_scr, *, first_norm_tile, n_chunks,
                   norm_group):
    j = pl.program_id(1)

    @pl.when(j == 0)
    def _():
        h = _mod_norm(x_ref[...], g_ref[...], mod_ref[0, 0:1, :], mod_ref[0, 1:2, :])
        h_scr[...] = h.astype(BF16)

    acc = jnp.dot(h_scr[...], w_ref[0], preferred_element_type=F32)

    @pl.when(j < first_norm_tile)
    def _():
        z_ref[...] = acc.astype(BF16)

    for tile in range(first_norm_tile, len(norm_group) // n_chunks):
        @pl.when(j == tile)
        def _(tile=tile):
            lo_lane = lax.broadcasted_iota(jnp.int32, (1, LANES), 1) < HALF
            for c in range(n_chunks):
                sl = slice(c * LANES, (c + 1) * LANES)
                group = norm_group[tile * n_chunks + c]
                y = acc[:, sl]
                if group == 0:
                    z_ref[:, sl] = y.astype(BF16)
                    continue
                y2 = y * y
                if group == LANES:
                    ss = jnp.sum(y2, axis=-1, keepdims=True)
                else:
                    lo = jnp.sum(jnp.where(lo_lane, y2, 0.0), axis=-1, keepdims=True)
                    hi = jnp.sum(jnp.where(lo_lane, 0.0, y2), axis=-1, keepdims=True)
                    ss = jnp.where(lo_lane, lo, hi)
                r = lax.rsqrt(ss * (1.0 / group) + RMS_EPS) * gain_ref[0, :, sl]
                z_ref[:, sl] = (y * r).astype(BF16)


def _inproj(x2, mod, g, wz, coef, layer, lay, seq, tm=1024):
    t, d = x2.shape
    tm = min(tm, seq)
    tn = lay.chunks_per_tile * LANES
    per_batch = seq // tm
    kern = functools.partial(_inproj_kernel, first_norm_tile=lay.first_norm_tile,
                             n_chunks=lay.chunks_per_tile, norm_group=lay.norm_group)
    return pl.pallas_call(
        kern,
        out_shape=jax.ShapeDtypeStruct((t, lay.nz), BF16),
        grid=(t // tm, lay.nz // tn),
        in_specs=[pl.BlockSpec((tm, d), lambda i, j: (i, 0)),
                  pl.BlockSpec((1, 6, d), lambda i, j: (i // per_batch, 0, 0)),
                  pl.BlockSpec((1, d), lambda i, j: (0, 0)),
                  pl.BlockSpec((1, d, tn), lambda i, j: (layer, 0, j)),
                  pl.BlockSpec((1, 1, tn), lambda i, j: (layer, 0, j))],
        out_specs=pl.BlockSpec((tm, tn), lambda i, j: (i, j)),
        scratch_shapes=[pltpu.VMEM((tm, d), BF16)],
        compiler_params=_params("parallel", "arbitrary"),
        name="inproj",
    )(x2, mod, g, wz, coef)


def _band_kernel(sink_ref, q_ref, k_ref, v_ref, tab_ref, qmask_ref, o_ref, *, n_side, ck, nkc, hp, q_of,
                 nsub):
    g = pl.program_id(1)
    i = pl.program_id(2)
    for u in range(nsub):
        qi = i * nsub + u
        rows = slice(u * ck, (u + 1) * ck)
        chunks = []
        for d in range(-n_side, n_side + 1):
            kc = qi + d
            valid = jnp.logical_and(kc >= 0, kc < nkc)
            start = pl.multiple_of(jnp.clip(kc, 0, nkc - 1) * ck, ck)
            chunks.append((d + n_side, start, jnp.where(valid, 0.0, NEG_INF)))
        for h in range(hp):
            q = q_ref[rows, q_of(h) * LANES:(q_of(h) + 1) * LANES] * qmask_ref[h]
            sink = sink_ref[g * hp + h]
            m = jnp.full((ck, 1), sink, F32)
            scores = []
            for di, start, penalty in chunks:
                s = lax.dot_general(q, k_ref[pl.ds(start, ck), :], _NT, preferred_element_type=F32)
                s = s + tab_ref[h, di] + penalty
                m = jnp.maximum(m, jnp.max(s, axis=-1, keepdims=True))
                scores.append(s)
            den = jnp.exp2(sink - m)
            acc = jnp.zeros((ck, LANES), F32)
            for s, (di, start, penalty) in zip(scores, chunks):
                p = jnp.exp2(s - m)
                den = den + jnp.sum(p, axis=-1, keepdims=True)
                acc = acc + jnp.dot(p.astype(BF16), v_ref[pl.ds(start, ck), :],
                                    preferred_element_type=F32)
            o_ref[rows, h * LANES:(h + 1) * LANES] = (acc / den).astype(BF16)


def _band_attention(z, table, qmask, sink, *, batch, seq, n_groups, hp, q_width, q_of, qblk, kblk, vblk,
                    n_side, nsub=4):
    t = z.shape[0]
    ck = table.shape[-1]
    nsub = min(nsub, seq // ck)
    tq = nsub * ck
    nq = seq // tq
    kern = functools.partial(_band_kernel, n_side=n_side, ck=ck, nkc=seq // ck, hp=hp, q_of=q_of, nsub=nsub)
    nd = 2 * n_side + 1
    return pl.pallas_call(
        kern,
        out_shape=jax.ShapeDtypeStruct((t, n_groups * hp * LANES), BF16),
        grid=(batch, n_groups, nq),
        in_specs=[pl.BlockSpec(memory_space=pltpu.SMEM),
                  pl.BlockSpec((tq, q_width), lambda b, g, i: (b * nq + i, qblk(g))),
                  pl.BlockSpec((seq, LANES), lambda b, g, i: (b, kblk(g))),
                  pl.BlockSpec((seq, LANES), lambda b, g, i: (b, vblk(g))),
                  pl.BlockSpec((hp, nd, ck, ck), lambda b, g, i: (g, 0, 0, 0)),
                  pl.BlockSpec((hp, 1, LANES), lambda b, g, i: (g, 0, 0))],
        out_specs=pl.BlockSpec((tq, hp * LANES), lambda b, g, i: (b * nq + i, g)),
        compiler_params=_params("parallel", "parallel", "arbitrary"),
        name="band_attention",
    )(sink, z, z, z, table, qmask)


def _alibi_slopes(n):
    return np.array([2.0 ** (-8.0 * (i + 1) / n) for i in range(n)], dtype=np.float32)


def _dilated_table(tq):
    reach = max(w // 2 for w, _ in A_PATTERNS)
    n_side = -(-reach // tq)
    d = np.arange(-n_side, n_side + 1)[:, None, None] * tq
    off = d + np.arange(tq)[None, None, :] - np.arange(tq)[None, :, None]
    mult = np.zeros(off.shape, np.float64)
    for window, dil in A_PATTERNS:
        mult += ((off % dil) == 0) & (np.abs(off) <= window // 2)
    logm = np.where(mult > 0, np.log(np.maximum(mult, 1.0)), 0.0)
    slopes = _alibi_slopes(A_HEADS).astype(np.float64)
    tab = (-slopes[:, None, None, None] * np.abs(off)[None] + logm[None]) * _LOG2E
    tab = np.where(mult[None] > 0, tab, NEG_INF)
    return jnp.asarray(tab.astype(np.float32)), n_side


def _window_table(tq):
    n_side = -(-C_RADIUS // tq)
    d = np.arange(-n_side, n_side + 1)[:, None, None] * tq
    off = d + np.arange(tq)[None, None, :] - np.arange(tq)[None, :, None]
    slopes = _alibi_slopes(C_HEADS).astype(np.float64)
    tab = -slopes[:, None, None, None] * np.abs(off)[None] * _LOG2E
    tab = np.where((np.abs(off) <= C_RADIUS)[None], tab, NEG_INF)
    return jnp.asarray(tab.astype(np.float32)), n_side


_D_QROWS = 8
_D_KROWS = 16


_D_HEADS_PER_STEP = 2


def _nbr_kernel(q_ref, k_ref, v_ref, tab_ref, o_ref, *, rows):
    t = pl.program_id(2)
    krow = jnp.clip(t * _D_QROWS - D_WIN_R // 2, 0, rows - _D_KROWS)
    start = pl.multiple_of(krow * GRID_W, GRID_W * (D_WIN_R // 2))
    nk = _D_KROWS * GRID_W
    for h in range(_D_HEADS_PER_STEP):
        lanes = slice(h * LANES, (h + 1) * LANES)
        s = lax.dot_general(q_ref[:, lanes], k_ref[pl.ds(start, nk), lanes], _NT,
                            preferred_element_type=F32)
        s = s + tab_ref[h, 0]
        m = jnp.max(s, axis=-1, keepdims=True)
        p = jnp.exp2(s - m)
        den = jnp.sum(p, axis=-1, keepdims=True)
        acc = jnp.dot(p.astype(BF16), v_ref[pl.ds(start, nk), lanes], preferred_element_type=F32)
        o_ref[:, lanes] = (acc / den).astype(BF16)


def _nbr_tables(rpb, rows):
    depth, heads, n_dr, n_dc = rpb.shape
    nt = rows // _D_QROWS
    wr = min(D_WIN_R, rows)
    pad_l = GRID_W - 1 - (D_WIN_C - 1)
    vpad = jnp.pad(rpb.astype(F32), ((0, 0), (0, 0), (0, 0), (pad_l, 2 * GRID_W - 1 - n_dc - pad_l)))
    toep = jnp.stack([vpad[..., GRID_W - 1 - c:2 * GRID_W - 1 - c] for c in range(GRID_W)], axis=-2)
    c = np.arange(GRID_W)
    cs = np.clip(c - D_WIN_C // 2, 0, GRID_W - D_WIN_C)
    col_ok = (c[None, :] >= cs[:, None]) & (c[None, :] < cs[:, None] + D_WIN_C)
    colmat = jnp.where(col_ok, toep * _LOG2E, NEG_INF)
    masked = jnp.full((depth, heads, 1, GRID_W, GRID_W), NEG_INF, F32)
    colmat = jnp.concatenate([colmat, masked], axis=2)
    sel = np.full((3, _D_QROWS, _D_KROWS), n_dr, np.int32)
    for kind, t in enumerate((0, 1, nt - 1)):
        krow0 = int(np.clip(t * _D_QROWS - D_WIN_R // 2, 0, rows - _D_KROWS))
        for ri in range(_D_QROWS):
            r = t * _D_QROWS + ri
            rs = int(np.clip(r - wr // 2, 0, rows - wr))
            for kj in range(_D_KROWS):
                kr = krow0 + kj
                if rs <= kr < rs + wr:
                    sel[kind, ri, kj] = kr - r + (D_WIN_R - 1)
    blocks = jnp.take(colmat, jnp.asarray(sel.reshape(-1)), axis=2)
    blocks = blocks.reshape(depth * heads, 3, _D_QROWS, _D_KROWS, GRID_W, GRID_W)
    blocks = blocks.transpose(0, 1, 2, 4, 3, 5)
    return blocks.reshape(depth * heads, 3, _D_QROWS * GRID_W, _D_KROWS * GRID_W)


def _nbr_attention(z, table, layer, *, batch, seq, lay):
    t = z.shape[0]
    rows = seq // GRID_W
    nt = rows // _D_QROWS
    tq = _D_QROWS * GRID_W
    nk = _D_KROWS * GRID_W
    kern = functools.partial(_nbr_kernel, rows=rows)
    hp = _D_HEADS_PER_STEP
    groups = D_HEADS // hp
    width = hp * LANES
    assert lay.QD0 % hp == 0 and lay.KD0 % hp == 0 and lay.VD0 % hp == 0
    kind = lambda i: jnp.where(i == 0, 0, jnp.where(i == nt - 1, 2, 1))
    return pl.pallas_call(
        kern,
        out_shape=jax.ShapeDtypeStruct((t, D_HEADS * LANES), BF16),
        grid=(batch, groups, nt),
        in_specs=[pl.BlockSpec((tq, width), lambda b, g, i: (b * nt + i, lay.QD0 // hp + g)),
                  pl.BlockSpec((seq, width), lambda b, g, i: (b, lay.KD0 // hp + g)),
                  pl.BlockSpec((seq, width), lambda b, g, i: (b, lay.VD0 // hp + g)),
                  pl.BlockSpec((hp, 1, tq, nk), lambda b, g, i: (layer * groups + g, kind(i), 0, 0))],
        out_specs=pl.BlockSpec((tq, width), lambda b, g, i: (b * nt + i, g)),
        compiler_params=_params("parallel", "parallel", "arbitrary"),
        name="nbr_attention",
    )(z, z, z, table)


_N_BIAS_PIECES = 3
_KPOS_RADIX = 64


def _bf16_round(v):
    u = np.asarray(v, np.float32).view(np.uint32)
    u = (u + np.uint32(0x7FFF) + ((u >> np.uint32(16)) & np.uint32(1))) & np.uint32(0xFFFF0000)
    return u.view(np.float32)


def _diff_feature_tables(tq, tk):
    c32 = (_alibi_slopes(B_HEADS).astype(np.float64) * np.log2(np.e)).astype(np.float32)
    qconst = np.zeros((B_HEADS, 2, LANES), np.float32)
    for h in range(B_HEADS):
        rem = np.float32(c32[h])
        pieces = []
        for _ in range(_N_BIAS_PIECES):
            p = np.float32(_bf16_round(rem))
            pieces.append(p)
            rem = np.float32(rem - p)
        for br, base in ((0, HALF), (1, 0)):
            for n, p in enumerate(pieces):
                qconst[h, br, base + n] = -_KPOS_RADIX * p
                qconst[h, br, base + _N_BIAS_PIECES + n] = -p
    j = np.arange(tk)
    kfeat = np.zeros((2, tk, LANES), np.float32)
    for br, base in ((0, HALF), (1, 0)):
        for n in range(_N_BIAS_PIECES):
            kfeat[br, :, base + n] = j // _KPOS_RADIX
            kfeat[br, :, base + _N_BIAS_PIECES + n] = j % _KPOS_RADIX
            kfeat[br, :, base + 2 * _N_BIAS_PIECES + n] = 1.0
    dist = np.abs(np.arange(tq)[None, :] - np.arange(tq)[:, None]).astype(np.float64)
    ttab = (-c32.astype(np.float64)[:, None, None] * dist[None]).astype(np.float32)
    return jnp.asarray(qconst), jnp.asarray(kfeat, BF16), jnp.asarray(c32), jnp.asarray(ttab)


_WINDOW_BLOCKS = (5, 11)
_UNDERFLOW_MARGIN = 160.0


def _diff_kernel(c_ref, misc_ref, q_ref, k_ref, v_ref, qconst_ref, kfeat_ref, ttab_ref, dl_ref, subg_ref, o_ref,
                 kaug_scr, kmax_scr, *, tq, tk, nk):
    h = pl.program_id(1)
    i = pl.program_id(2)
    c = c_ref[h]
    lam_init = misc_ref[0]
    lane = lax.broadcasted_iota(jnp.int32, (1, LANES), 1)
    lo_lane = lane < HALF

    def half_sums(v):
        lo = jnp.sum(jnp.where(lo_lane, v, 0.0), axis=-1, keepdims=True)
        hi = jnp.sum(jnp.where(lo_lane, 0.0, v), axis=-1, keepdims=True)
        return lo, hi

    @pl.when(i == 0)
    def _():
        def aug(cc, kmax2):
            sl = pl.ds(pl.multiple_of(cc * tk, tk), tk)
            kb = k_ref[sl, :]
            kaug_scr[0, sl, :] = jnp.where(lo_lane, kb, kfeat_ref[0])
            kaug_scr[1, sl, :] = jnp.where(lo_lane, kfeat_ref[1], kb)
            kf = kb.astype(F32)
            lo, hi = half_sums(kf * kf)
            return jnp.maximum(kmax2, jnp.max(jnp.maximum(lo, hi), axis=0, keepdims=True))
        kmax2 = lax.fori_loop(0, nk, aug, jnp.zeros((1, 1), F32))
        kmax_scr[...] = jnp.broadcast_to(kmax2, kmax_scr.shape)

    q = q_ref[...].astype(F32)
    ci = c * lax.broadcasted_iota(jnp.int32, (tq, 1), 0).astype(F32)
    r1 = ci.astype(BF16).astype(F32)
    r2 = (ci - r1).astype(BF16).astype(F32)
    r3 = (ci - r1 - r2).astype(BF16).astype(F32)
    q_plain, q_right, q_left = [], [], []
    for br, base in ((0, HALF), (1, 0)):
        keep = lo_lane if br == 0 else jnp.logical_not(lo_lane)
        f0 = base + 2 * _N_BIAS_PIECES
        feat = qconst_ref[0, br:br + 1, :] + jnp.where(
            lane == f0, r1, jnp.where(lane == f0 + 1, r2, jnp.where(lane == f0 + 2, r3, 0.0)))
        q_plain.append(jnp.where(keep, q, 0.0).astype(BF16))
        q_right.append(jnp.where(keep, q, feat).astype(BF16))
        q_left.append(jnp.where(keep, q, -feat).astype(BF16))

    def update(st, s, shift, vb):
        m_old, l_old, acc_old = st
        m_new = jnp.maximum(m_old, jnp.max(s, axis=-1, keepdims=True) + shift)
        alpha = jnp.exp2(m_old - m_new)
        p = jnp.exp2(s + (shift - m_new))
        l_new = alpha * l_old + jnp.sum(p, axis=-1, keepdims=True)
        acc_new = alpha * acc_old + jnp.dot(p.astype(BF16), vb, preferred_element_type=F32)
        return m_new, l_new, acc_new

    cdiag = (i * tq) // tk

    def side_block(rows, on_right, base, st):
        ss = [lax.dot_general(jnp.where(on_right, q_right[idx], q_left[idx]), kaug_scr[idx, rows, :], _NT,
                              preferred_element_type=F32) for idx in range(2)]
        shift = jnp.where(on_right, -1.0, 1.0) * c * (base - i * tq).astype(F32)
        vb = v_ref[rows, :]
        return [update(st[idx], ss[idx], shift, vb) for idx in range(2)]

    def chunk_rows(cc):
        return pl.ds(pl.multiple_of(cc * tk, tk), tk)

    def square_block():
        st = [(jnp.full((tq, 1), NEG_INF, F32), jnp.zeros((tq, 1), F32), jnp.zeros((tq, LANES), F32))
              for _ in range(2)]
        sq = pl.ds(pl.multiple_of(i * tq, tq), tq)
        ss = [lax.dot_general(q_plain[idx], kaug_scr[idx, sq, :], _NT, preferred_element_type=F32)
              + ttab_ref[0] for idx in range(2)]
        vb = v_ref[sq, :]
        return [update(st[idx], ss[idx], 0.0, vb) for idx in range(2)]

    def diag_chunk():
        st = square_block()
        n_sub = tk // tq
        usq = i % n_sub
        dbase = cdiag * tk
        for w in range(1, n_sub):
            uo = usq + w
            on_right = uo < n_sub
            uo = jnp.where(on_right, uo, uo - n_sub)
            rows = pl.ds(pl.multiple_of(dbase + uo * tq, tq), tq)
            st = side_block(rows, on_right, dbase, st)
        return st

    def finish(st):
        hs = pl.ds(h, 1)
        lam = (jnp.exp(jnp.sum(dl_ref[0, hs, :] * dl_ref[1, hs, :], axis=-1, keepdims=True))
               - jnp.exp(jnp.sum(dl_ref[2, hs, :] * dl_ref[3, hs, :], axis=-1, keepdims=True)) + lam_init)
        o = st[0][2] / st[0][1] - lam * (st[1][2] / st[1][1])
        o = o * lax.rsqrt(jnp.mean(o * o, axis=-1, keepdims=True) + RMS_EPS) * subg_ref[...]
        o_ref[...] = (o * (1.0 - lam_init)).astype(BF16)

    def all_chunks():
        st = diag_chunk()
        for step in range(1, nk):
            ct = cdiag + step
            on_right = ct < nk
            cc = jnp.where(on_right, ct, ct - nk)
            st = side_block(chunk_rows(cc), on_right, cc * tk, st)
        finish(st)

    n_blocks = nk * (tk // tq)
    windows = tuple(w for w in _WINDOW_BLOCKS if w < n_blocks)
    if not windows:
        all_chunks()
        return

    kd = k_ref[pl.ds(pl.multiple_of(i * tq, tq), tq), :].astype(F32)
    qlo, qhi = half_sums(q * q)
    dlo, dhi = half_sums(q * kd)
    kmax = jnp.sqrt(kmax_scr[0:1, 0:1])
    spread = jnp.sqrt(jnp.maximum(qlo, qhi)) * kmax - jnp.minimum(dlo, dhi)
    need = jnp.max(spread, axis=0, keepdims=True)[0, 0] + _UNDERFLOW_MARGIN
    far = 1e9
    n_sub = tk // tq
    taken = False
    for wb in windows:
        start = jnp.clip(i - wb // 2, 0, n_blocks - wb)
        d_left = jnp.where(start > 0, ((i - start) * tq).astype(F32), far)
        d_right = jnp.where(start + wb < n_blocks, ((start + wb - i - 1) * tq).astype(F32), far)
        fits = c * jnp.minimum(d_left, d_right) >= need
        use = fits if taken is False else jnp.logical_and(fits, jnp.logical_not(taken))

        @pl.when(use)
        def _(wb=wb, start=start):
            st = square_block()
            for w in range(wb - 1):
                bw = start + w
                blk = jnp.where(bw >= i, bw + 1, bw)
                rows = pl.ds(pl.multiple_of(blk * tq, tq), tq)
                st = side_block(rows, blk > i, (blk // n_sub) * tk, st)
            finish(st)

        taken = fits if taken is False else jnp.logical_or(taken, fits)

    @pl.when(jnp.logical_not(taken))
    def _():
        all_chunks()


def _diff_attention(z, dl, subg, tables, misc, *, batch, seq, lay, tq, tk):
    t = z.shape[0]
    nq = seq // tq
    qconst, kfeat, cvals, ttab = tables
    kern = functools.partial(_diff_kernel, tq=tq, tk=tk, nk=seq // tk)
    return pl.pallas_call(
        kern,
        out_shape=jax.ShapeDtypeStruct((t, B_HEADS * LANES), BF16),
        grid=(batch, B_HEADS, nq),
        in_specs=[pl.BlockSpec(memory_space=pltpu.SMEM),
                  pl.BlockSpec(memory_space=pltpu.SMEM),
                  pl.BlockSpec((tq, LANES), lambda b, h, i: (b * nq + i, lay.QB0 + h)),
                  pl.BlockSpec((seq, LANES), lambda b, h, i: (b, lay.KB0 + h)),
                  pl.BlockSpec((seq, LANES), lambda b, h, i: (b, lay.VB0 + h)),
                  pl.BlockSpec((1, 2, LANES), lambda b, h, i: (h, 0, 0)),
                  pl.BlockSpec(kfeat.shape, lambda b, h, i: (0, 0, 0)),
                  pl.BlockSpec((1, tq, tq), lambda b, h, i: (h, 0, 0)),
                  pl.BlockSpec(dl.shape, lambda b, h, i: (0, 0, 0)),
                  pl.BlockSpec((1, LANES), lambda b, h, i: (0, 0))],
        out_specs=pl.BlockSpec((tq, LANES), lambda b, h, i: (b * nq + i, h)),
        scratch_shapes=[pltpu.VMEM((2, seq, LANES), BF16), pltpu.VMEM((8, LANES), F32)],
        compiler_params=_params("parallel", "arbitrary", "arbitrary"),
        name="diff_attention",
    )(cvals, misc, z, z, z, qconst, kfeat, ttab, dl, subg)


def _merge_kernel(oa_ref, ob_ref, oc_ref, od_ref, gate_ref, wa_ref, wb_ref, wc_ref, wd_ref, o_ref, *, d):
    acc = None
    for idx, (o, w) in enumerate(((oa_ref, wa_ref), (ob_ref, wb_ref), (oc_ref, wc_ref), (od_ref, wd_ref))):
        y = jnp.dot(o[...], w[...], preferred_element_type=F32)
        g = _sigmoid(gate_ref[:, idx * d:(idx + 1) * d].astype(F32))
        acc = g * y if acc is None else acc + g * y
    o_ref[...] = acc.astype(BF16)


def _merge(oa, ob, oc, od, z, wbs, *, d, tm=256):
    t = oa.shape[0]
    tm = min(tm, t)
    row = lambda a: pl.BlockSpec((tm, a.shape[1]), lambda i: (i, 0))
    full = lambda a: pl.BlockSpec(a.shape, lambda i: (0, 0))
    return pl.pallas_call(
        functools.partial(_merge_kernel, d=d),
        out_shape=jax.ShapeDtypeStruct((t, d), BF16),
        grid=(t // tm,),
        in_specs=[row(oa), row(ob), row(oc), row(od),
                  pl.BlockSpec((tm, N_BRANCH * d), lambda i: (i, 0))] + [full(w) for w in wbs],
        out_specs=pl.BlockSpec((tm, d), lambda i: (i, 0)),
        compiler_params=_params("parallel"),
        name="branch_merge",
    )(oa, ob, oc, od, z, *wbs)


def _outproj_kernel(a_ref, w_ref, x_ref, g_ref, o_ref):
    y = jnp.dot(a_ref[...], w_ref[0], preferred_element_type=F32)
    o_ref[...] = x_ref[...] + g_ref[0] * y


def _outproj(a, w, layer, x2, gate, seq, tm=512):
    t, d = x2.shape
    tm = min(tm, seq)
    per_batch = seq // tm
    return pl.pallas_call(
        _outproj_kernel,
        out_shape=jax.ShapeDtypeStruct((t, d), F32),
        grid=(t // tm,),
        in_specs=[pl.BlockSpec((tm, a.shape[1]), lambda i: (i, 0)),
                  pl.BlockSpec((1,) + w.shape[1:], lambda i: (layer, 0, 0)),
                  pl.BlockSpec((tm, d), lambda i: (i, 0)),
                  pl.BlockSpec((1, 1, d), lambda i: (i // per_batch, 0, 0))],
        out_specs=pl.BlockSpec((tm, d), lambda i: (i, 0)),
        compiler_params=_params("parallel"),
        name="out_proj",
    )(a, w, x2, gate)


def _router_kernel(x_ref, mod_ref, g_ref, rwt_ref, rb_ref, h_ref, e_ref, gw_ref):
    h = _mod_norm(x_ref[...], g_ref[...], mod_ref[0, 3:4, :], mod_ref[0, 4:5, :])
    h_ref[...] = h
    logits = lax.dot_general(rwt_ref[...], h.astype(BF16), _NT, preferred_element_type=F32)
    score = _sigmoid(logits)
    sel = score + rb_ref[...]
    sel_r = [sel[e:e + 1, :] for e in range(N_EXPERTS)]
    sc_r = [score[e:e + 1, :] for e in range(N_EXPERTS)]
    n = EXPERTS_PER_GROUP

    def top2_sum(v):
        best = None
        for a in range(n):
            for b in range(a + 1, n):
                s = v[a] + v[b]
                best = s if best is None else jnp.maximum(best, s)
        return best

    gbest = top2_sum(sel_r[0:n])
    gidx = jnp.zeros(gbest.shape, jnp.int32)
    cv = list(sel_r[0:n])
    cs = list(sc_r[0:n])
    for g in range(1, N_GROUPS):
        gs = top2_sum(sel_r[g * n:(g + 1) * n])
        upd = gs > gbest
        gbest = jnp.where(upd, gs, gbest)
        gidx = jnp.where(upd, g, gidx)
        for j in range(n):
            cv[j] = jnp.where(upd, sel_r[g * n + j], cv[j])
            cs[j] = jnp.where(upd, sc_r[g * n + j], cs[j])
    b1, i1, s1 = cv[0], jnp.zeros(gbest.shape, jnp.int32), cs[0]
    for j in range(1, n):
        upd = cv[j] > b1
        b1 = jnp.where(upd, cv[j], b1)
        i1 = jnp.where(upd, j, i1)
        s1 = jnp.where(upd, cs[j], s1)
    low = -3.0e38
    b2 = jnp.full(gbest.shape, low, F32)
    i2 = jnp.zeros(gbest.shape, jnp.int32)
    s2 = jnp.zeros(gbest.shape, F32)
    for j in range(n):
        cand = jnp.where(i1 == j, low, cv[j])
        upd = cand > b2
        b2 = jnp.where(upd, cand, b2)
        i2 = jnp.where(upd, j, i2)
        s2 = jnp.where(upd, cs[j], s2)
    tot = s1 + s2
    e_ref[0:1, :] = gidx * n + i1
    e_ref[1:2, :] = gidx * n + i2
    gw_ref[0:1, :] = s1 / tot
    gw_ref[1:2, :] = s2 / tot


def _router(x2, mod, g, rwt, rb, seq, tm=512):
    t, d = x2.shape
    tm = min(tm, seq)
    per_batch = seq // tm
    return pl.pallas_call(
        _router_kernel,
        out_shape=(jax.ShapeDtypeStruct((t, d), F32),
                   jax.ShapeDtypeStruct((2, t), jnp.int32),
                   jax.ShapeDtypeStruct((2, t), F32)),
        grid=(t // tm,),
        in_specs=[pl.BlockSpec((tm, d), lambda i: (i, 0)),
                  pl.BlockSpec((1, 6, d), lambda i: (i // per_batch, 0, 0)),
                  pl.BlockSpec((1, d), lambda i: (0, 0)),
                  pl.BlockSpec(rwt.shape, lambda i: (0, 0)),
                  pl.BlockSpec(rb.shape, lambda i: (0, 0))],
        out_specs=(pl.BlockSpec((tm, d), lambda i: (i, 0)),
                   pl.BlockSpec((2, tm), lambda i: (0, i)),
                   pl.BlockSpec((2, tm), lambda i: (0, i))),
        compiler_params=_params("parallel"),
        name="norm_router",
    )(x2, mod, g, rwt, rb)


def _rank_kernel(e_ref, rank_ref, cnt_ref, carry):
    @pl.when(pl.program_id(0) == 0)
    def _():
        carry[...] = jnp.zeros(carry.shape, F32)

    tm = e_ref.shape[1]
    eio = lax.broadcasted_iota(jnp.int32, (N_EXPERTS, tm), 0)
    oh0 = eio == e_ref[0:1, :]
    oh1 = eio == e_ref[1:2, :]
    c = jnp.where(jnp.logical_or(oh0, oh1), 1.0, 0.0)
    before = (lax.broadcasted_iota(jnp.int32, (tm, tm), 0) < lax.broadcasted_iota(jnp.int32, (tm, tm), 1))
    pref = jnp.dot(c.astype(BF16), jnp.where(before, 1.0, 0.0).astype(BF16),
                   preferred_element_type=F32) + carry[...]
    rank_ref[0:1, :] = jnp.sum(jnp.where(oh0, pref, 0.0), axis=0, keepdims=True).astype(jnp.int32)
    rank_ref[1:2, :] = jnp.sum(jnp.where(oh1, pref, 0.0), axis=0, keepdims=True).astype(jnp.int32)
    total = carry[...] + jnp.sum(c, axis=1, keepdims=True)
    carry[...] = total
    cnt_ref[...] = jnp.broadcast_to(total, cnt_ref.shape)


def _rank(eidx, tm=512):
    t = eidx.shape[1]
    tm = min(tm, t)
    return pl.pallas_call(
        _rank_kernel,
        out_shape=(jax.ShapeDtypeStruct((2, t), jnp.int32),
                   jax.ShapeDtypeStruct((N_EXPERTS, LANES), F32)),
        grid=(t // tm,),
        in_specs=[pl.BlockSpec((2, tm), lambda i: (0, i))],
        out_specs=(pl.BlockSpec((2, tm), lambda i: (0, i)),
                   pl.BlockSpec((N_EXPERTS, LANES), lambda i: (0, 0))),
        scratch_shapes=[pltpu.VMEM((N_EXPERTS, 1), F32)],
        compiler_params=_params("arbitrary"),
        name="expert_rank",
    )(eidx)


_DMA_UNROLL = 8


def _row_copy(src_hbm, row, dst, slot, r, sem):
    return pltpu.make_async_copy(src_hbm.at[pl.ds(row, 1), :], dst.at[slot, pl.ds(r, 1), :], sem.at[slot])


def _expert_kernel(te_ref, nu_ref, src_ref, h_hbm, w1_ref, w3_ref, w2_ref, y_ref, xbuf, sem, *, tm):
    j = pl.program_id(0)
    n_used = nu_ref[0]

    def wait_tile(slot):
        pltpu.make_async_copy(h_hbm.at[pl.ds(0, tm), :], xbuf.at[slot], sem.at[slot]).wait()

    @pl.when(jnp.logical_and(j == 0, n_used > 0))
    def _():
        def issue(r, carry):
            _row_copy(h_hbm, src_ref[r], xbuf, 0, r, sem).start()
            return carry
        lax.fori_loop(0, tm, issue, 0, unroll=_DMA_UNROLL)

    @pl.when(j < n_used)
    def _():
        slot = j % 2
        wait_tile(slot)
        nxt = jnp.minimum(j + 1, n_used - 1) * tm
        for r in range(tm):
            _row_copy(h_hbm, src_ref[nxt + r], xbuf, 1 - slot, r, sem).start()
        xs = xbuf[slot].astype(BF16)
        g = jnp.dot(xs, w1_ref[0], preferred_element_type=F32)
        u = jnp.dot(xs, w3_ref[0], preferred_element_type=F32)
        mid = (g * _sigmoid(g) * u).astype(BF16)
        y_ref[...] = jnp.dot(mid, w2_ref[0], preferred_element_type=F32)

    @pl.when(j == n_used - 1)
    def _():
        wait_tile(1 - j % 2)

    @pl.when(j >= n_used)
    def _():
        y_ref[...] = jnp.zeros(y_ref.shape, F32)


def _experts(tile_e, n_used, src, h, w1, w3, w2, layer, tm):
    p = src.shape[0]
    d = h.shape[1]
    f = w1.shape[-1]
    n_exp = w1.shape[1]
    w1r = w1.reshape((-1,) + w1.shape[2:])
    w3r = w3.reshape((-1,) + w3.shape[2:])
    w2r = w2.reshape((-1,) + w2.shape[2:])
    wmap = lambda j, te, nu, src: (layer * n_exp + te[j], 0, 0)
    return pl.pallas_call(
        functools.partial(_expert_kernel, tm=tm),
        out_shape=jax.ShapeDtypeStruct((p, d), F32),
        grid_spec=pltpu.PrefetchScalarGridSpec(
            num_scalar_prefetch=3,
            grid=(p // tm,),
            in_specs=[pl.BlockSpec(memory_space=pl.ANY),
                      pl.BlockSpec((1, d, f), wmap),
                      pl.BlockSpec((1, d, f), wmap),
                      pl.BlockSpec((1, f, d), wmap)],
            out_specs=pl.BlockSpec((tm, d), lambda j, te, nu, src: (j, 0)),
            scratch_shapes=[pltpu.VMEM((2, tm, d), F32), pltpu.SemaphoreType.DMA((2,))]),
        compiler_params=_params("arbitrary"),
        name="experts",
    )(tile_e, n_used, src, h, w1r, w3r, w2r)


def _combine_kernel(pos_ref, x_ref, g_ref, gw_ref, y_hbm, o_ref, buf0, buf1, sem0, sem1, *, tm, t, n_steps):
    i = pl.program_id(0)

    def start_step(step, slot):
        def issue(r, carry):
            _row_copy(y_hbm, pos_ref[step * tm + r], buf0, slot, r, sem0).start()
            _row_copy(y_hbm, pos_ref[t + step * tm + r], buf1, slot, r, sem1).start()
            return carry
        lax.fori_loop(0, tm, issue, 0, unroll=_DMA_UNROLL)

    def wait_step(slot):
        pltpu.make_async_copy(y_hbm.at[pl.ds(0, tm), :], buf0.at[slot], sem0.at[slot]).wait()
        pltpu.make_async_copy(y_hbm.at[pl.ds(0, tm), :], buf1.at[slot], sem1.at[slot]).wait()

    @pl.when(i == 0)
    def _():
        start_step(0, 0)

    @pl.when(i + 1 < n_steps)
    def _():
        start_step(i + 1, (i + 1) % 2)

    slot = i % 2
    wait_step(slot)
    moe = gw_ref[:, 0:1] * buf0[slot] + gw_ref[:, 1:2] * buf1[slot]
    o_ref[...] = x_ref[...] + g_ref[0] * moe


def _combine(pos, x2, gate, gwt, y, seq, tm=256):
    t, d = x2.shape
    tm = min(tm, seq)
    per_batch = seq // tm
    n_steps = t // tm
    return pl.pallas_call(
        functools.partial(_combine_kernel, tm=tm, t=t, n_steps=n_steps),
        out_shape=jax.ShapeDtypeStruct((t, d), F32),
        grid_spec=pltpu.PrefetchScalarGridSpec(
            num_scalar_prefetch=1,
            grid=(n_steps,),
            in_specs=[pl.BlockSpec((tm, d), lambda i, pos: (i, 0)),
                      pl.BlockSpec((1, 1, d), lambda i, pos: (i // per_batch, 0, 0)),
                      pl.BlockSpec((tm, 2), lambda i, pos: (i, 0)),
                      pl.BlockSpec(memory_space=pl.ANY)],
            out_specs=pl.BlockSpec((tm, d), lambda i, pos: (i, 0)),
            scratch_shapes=[pltpu.VMEM((2, tm, d), F32), pltpu.VMEM((2, tm, d), F32),
                            pltpu.SemaphoreType.DMA((2,)), pltpu.SemaphoreType.DMA((2,))]),
        compiler_params=_params("arbitrary"),
        name="moe_combine",
    )(pos, x2, gate, gwt, y)


def _moe(x2, mod, g2n, rwt, rb, w1, w3, w2, layer, *, batch, seq, tm_e=256):
    t, d = x2.shape
    h, eidx, gw = _router(x2, mod, g2n, rwt, rb, seq)
    rank, cnt = _rank(eidx)
    tm_e = min(tm_e, t)
    counts = cnt[:, 0].astype(jnp.int32)
    gsz = ((counts + tm_e - 1) // tm_e) * tm_e
    ends = jnp.cumsum(gsz)
    offs = ends - gsz
    pos = rank
    for e in range(N_EXPERTS):
        pos = pos + jnp.where(eidx == e, offs[e], 0)
    n_tiles = (2 * t) // tm_e + N_EXPERTS
    p = n_tiles * tm_e
    flat = pos.reshape(-1)
    tok = jnp.broadcast_to(jnp.arange(t, dtype=jnp.int32)[None, :], (2, t)).reshape(-1)
    src = jnp.zeros((p,), jnp.int32).at[flat].set(tok)
    tile_start = jnp.arange(n_tiles, dtype=jnp.int32) * tm_e
    tile_e = jnp.minimum(jnp.sum(tile_start[:, None] >= ends[None, :], axis=1),
                         N_EXPERTS - 1).astype(jnp.int32)
    n_used = (ends[-1:] // tm_e).astype(jnp.int32)
    y = _experts(tile_e, n_used, src, h, w1, w3, w2, layer, tm_e)
    return _combine(flat, x2, mod[:, 5:6, :], gw.T, y, seq)


def kernel(x, c, ada_w, ada_b, norm1_g, norm2_g, w_in, qk_norm_a, qk_norm_b, qk_norm_c, qk_norm_d,
           diff_lambda, diff_subln_g, sink_logits, rpb, w_branch, w_out, router_w, router_b, w1, w3, w2):
    batch, seq, d = x.shape
    depth = ada_w.shape[0]
    t = batch * seq
    lay = _Layout(d)
    bw = lay.bw

    mod_all = _ada_mod(c, ada_w, ada_b)
    wz = _build_wz(w_in, lay)
    coef = jnp.stack([_build_coef(lay, qk_norm_a[l], qk_norm_b[l], qk_norm_c[l], qk_norm_d[l])
                      for l in range(depth)])
    wb16 = w_branch.astype(BF16)
    c_hd = bw // C_HEADS
    grp = C_HEADS // C_KV_HEADS
    wc_rows = []
    for hq in range(C_HEADS):
        blk = jnp.zeros((depth, LANES, d), BF16)
        off = (hq // grp) * c_hd
        wc_rows.append(blk.at[:, off:off + c_hd].set(wb16[:, 2, hq * c_hd:(hq + 1) * c_hd]))
    wc_ext = jnp.concatenate(wc_rows, axis=1)
    wout16 = w_out.astype(BF16)
    w1b, w3b, w2b = w1.astype(BF16), w3.astype(BF16), w2.astype(BF16)
    rwt = router_w.T.astype(BF16)
    rb = router_b.reshape(N_EXPERTS, 1).astype(F32)

    tab_a, side_a = _dilated_table(256)
    tab_c, side_c = _window_table(256)
    ones_mask = jnp.ones((A_HEADS, 1, LANES), BF16)
    lane = np.arange(LANES)
    cmask = np.stack([(lane // HALF) == (hq % 2) for hq in range(C_HEADS)]).astype(np.float32)
    cmask = jnp.asarray(cmask.reshape(C_HEADS, 1, LANES), BF16)
    no_sink = jnp.full((A_HEADS,), NEG_INF, F32)
    tq_b, tk_b = min(512, seq), min(1024, seq)
    tables_b = _diff_feature_tables(tq_b, tk_b)
    tab_d = _nbr_tables(rpb, seq // GRID_W)

    x2 = x.reshape(t, d)
    for l in range(depth):
        mod = mod_all[l, :batch].reshape(batch, 6, d)
        z = _inproj(x2, mod, norm1_g[l].reshape(1, d), wz, coef, l, lay, seq)
        o_a = _band_attention(z, tab_a, ones_mask, no_sink, batch=batch, seq=seq, n_groups=A_HEADS, hp=1,
                              q_width=LANES, q_of=lambda h: 0,
                              qblk=lambda g: lay.QA0 + g, kblk=lambda g: lay.KA0 + g,
                              vblk=lambda g: lay.VA0 + g, n_side=side_a)
        lam_init = 0.8 - 0.6 * float(np.exp(-0.3 * l))
        o_b = _diff_attention(z, diff_lambda[l], diff_subln_g[l].reshape(1, LANES), tables_b,
                              jnp.full((1,), lam_init, F32), batch=batch, seq=seq, lay=lay,
                              tq=tq_b, tk=tk_b)
        o_c = _band_attention(z, tab_c, cmask, sink_logits[l].astype(F32) * _LOG2E, batch=batch, seq=seq,
                              n_groups=C_KV_HEADS, hp=grp, q_width=(grp // 2) * LANES,
                              q_of=lambda h: h // 2,
                              qblk=lambda g: lay.QC0 // (grp // 2) + g, kblk=lambda g: lay.KC0 + g,
                              vblk=lambda g: lay.VC0, n_side=side_c)
        o_d = _nbr_attention(z, tab_d, l, batch=batch, seq=seq, lay=lay)
        merged = _merge(o_a, o_b, o_c, o_d, z, (wb16[l, 0], wb16[l, 1], wc_ext[l], wb16[l, 3]), d=d)
        x2 = _outproj(merged, wout16, l, x2, mod[:, 2:3, :], seq)
        x2 = _moe(x2, mod, norm2_g[l].reshape(1, d), rwt, rb, w1b, w3b, w2b, l, batch=batch, seq=seq)
    return x2.reshape(batch, seq, d)
```

```python
import functools

import numpy as np
import jax
import jax.numpy as jnp
from jax import lax
from jax.experimental import pallas as pl
from jax.experimental.pallas import tpu as pltpu

F32 = jnp.float32
BF16 = jnp.bfloat16
LANES = 128
HALF = LANES // 2
RMS_EPS = 1e-6
NEG_INF = -1e30
_LOG2E = float(np.log2(np.e))
VMEM_LIMIT = 56 * 1024 * 1024

GRID_W = 64
N_BRANCH = 4
A_HEADS = 4
A_PATTERNS = ((128, 1), (512, 4), (2048, 16))
B_HEADS = 4
C_HEADS = 8
C_KV_HEADS = 2
C_RADIUS = 128
D_HEADS = 4
D_WIN_R = 8
D_WIN_C = 16
N_EXPERTS = 16
N_GROUPS = 4
EXPERTS_PER_GROUP = N_EXPERTS // N_GROUPS

_NT = (((1,), (1,)), ((), ()))


def _params(*sem):
    return pltpu.CompilerParams(dimension_semantics=sem, vmem_limit_bytes=VMEM_LIMIT)


def _sigmoid(v):
    return 1.0 / (1.0 + jnp.exp(-v))


class _Layout:
    def __init__(self, d_model):
        bw = d_model // N_BRANCH
        self.bw = bw
        hb = bw // LANES
        o = {}
        o['qa'], o['ka'], o['va'] = 0, bw, 2 * bw
        o['qb'], o['kb'], o['vb'] = 3 * bw, 4 * bw, 5 * bw
        o['qc'] = 6 * bw
        o['kc'] = 7 * bw
        o['vc'] = 7 * bw + LANES
        o['qd'], o['kd'], o['vd'] = 7 * bw + 2 * LANES, 8 * bw + 2 * LANES, 9 * bw + 2 * LANES
        o['gates'] = 10 * bw + 2 * LANES
        self.orig = o
        gate_blocks = N_BRANCH * d_model // LANES
        self.GATE0 = 0
        b = gate_blocks
        self.QA0, self.KA0, self.VA0 = b, b + hb, b + 2 * hb
        b += 3 * hb
        self.QD0, self.KD0, self.VD0 = b, b + hb, b + 2 * hb
        b += 3 * hb
        self.QB0, self.KB0, self.VB0 = b, b + hb, b + 2 * hb
        b += 3 * hb
        self.QC0 = b
        self.KC0 = b + hb
        self.VC0 = b + hb + 2
        b += hb + 3
        self.n_real = b
        self.chunks_per_tile = 6
        self.n_blocks = -(-b // self.chunks_per_tile) * self.chunks_per_tile
        self.nz = self.n_blocks * LANES
        self.first_norm_tile = gate_blocks // self.chunks_per_tile
        assert self.QC0 % (C_HEADS // C_KV_HEADS // 2) == 0
        norm = [0] * self.n_blocks
        for blk0, nblk, group in ((self.QA0, 2 * hb, bw // A_HEADS), (self.QD0, 2 * hb, bw // D_HEADS),
                                  (self.QB0, 2 * hb, bw // (2 * B_HEADS)),
                                  (self.QC0, hb + 2, bw // C_HEADS)):
            assert group in (LANES, HALF)
            norm[blk0:blk0 + nblk] = [group] * nblk
        self.norm_group = tuple(norm)

    def weight_segments(self):
        o, bw = self.orig, self.bw
        kc0, kc1 = o['kc'], o['kc'] + HALF
        segs = [(o['gates'], N_BRANCH * N_BRANCH * bw),
                (o['qa'], bw), (o['ka'], bw), (o['va'], bw),
                (o['qd'], bw), (o['kd'], bw), (o['vd'], bw),
                (o['qb'], bw), (o['kb'], bw), (o['vb'], bw),
                (o['qc'], bw),
                (kc0, HALF), (kc0, HALF), (kc1, HALF), (kc1, HALF),
                (o['vc'], LANES)]
        pad = self.nz - self.n_real * LANES
        if pad:
            segs.append((None, pad))
        return segs


def _build_wz(w_in, lay):
    parts = []
    for start, width in lay.weight_segments():
        if start is None:
            parts.append(jnp.zeros(w_in.shape[:-1] + (width,), w_in.dtype))
        else:
            parts.append(w_in[..., start:start + width])
    return jnp.concatenate(parts, axis=-1).astype(BF16)


def _build_coef(lay, qk_a, qk_b, qk_c, qk_d):
    hb = lay.bw // LANES
    a_hd = lay.bw // A_HEADS
    b_dk = lay.bw // (2 * B_HEADS)
    c_hd = lay.bw // C_HEADS
    d_hd = lay.bw // D_HEADS
    gain = jnp.ones((lay.nz,), F32)

    def put(gain, blk0, nblk, g, scale):
        lo, hi = blk0 * LANES, (blk0 + nblk) * LANES
        reps = (hi - lo) // g.shape[0]
        return gain.at[lo:hi].set(jnp.tile(g.astype(F32), reps) * scale)

    gain = put(gain, lay.QA0, hb, qk_a[0], a_hd ** -0.5 * _LOG2E)
    gain = put(gain, lay.KA0, hb, qk_a[1], 1.0)
    gain = put(gain, lay.QD0, hb, qk_d[0], d_hd ** -0.5 * _LOG2E)
    gain = put(gain, lay.KD0, hb, qk_d[1], 1.0)
    gain = put(gain, lay.QB0, hb, qk_b[0], b_dk ** -0.5 * _LOG2E)
    gain = put(gain, lay.KB0, hb, qk_b[1], 1.0)
    gain = put(gain, lay.QC0, hb, qk_c[0], c_hd ** -0.5 * _LOG2E)
    gain = put(gain, lay.KC0, 2, qk_c[1], 1.0)
    group = np.repeat(np.asarray(lay.norm_group, np.float32), LANES)
    invd = np.where(group > 0, 1.0 / np.maximum(group, 1.0), 1.0).astype(np.float32)
    nflag = (group > 0).astype(np.float32)
    hflag = (group == HALF).astype(np.float32)
    return jnp.stack([gain, jnp.asarray(invd), jnp.asarray(nflag), jnp.asarray(hflag)])


def _ada_kernel(c_ref, w_ref, b_ref, o_ref):
    c = c_ref[...]
    cond = (c * _sigmoid(c)).astype(BF16)
    o_ref[0] = jnp.dot(cond, w_ref[0].astype(BF16), preferred_element_type=F32) + b_ref[0]


def _ada_mod(c, ada_w, ada_b, tn=1024):
    depth, d, n = ada_w.shape
    rows = 8
    cp = jnp.zeros((rows, d), F32).at[:c.shape[0]].set(c)
    tn = min(tn, n)
    return pl.pallas_call(
        _ada_kernel,
        out_shape=jax.ShapeDtypeStruct((depth, rows, n), F32),
        grid=(depth, n // tn),
        in_specs=[pl.BlockSpec((rows, d), lambda l, j: (0, 0)),
                  pl.BlockSpec((1, d, tn), lambda l, j: (l, 0, j)),
                  pl.BlockSpec((1, 1, tn), lambda l, j: (l, 0, j))],
        out_specs=pl.BlockSpec((1, rows, tn), lambda l, j: (l, 0, j)),
        compiler_params=_params("parallel", "parallel"),
        name="ada_mod",
    )(cp, ada_w, ada_b.reshape(depth, 1, n))


def _mod_norm(x, g, shift, scale):
    y = x * lax.rsqrt(jnp.mean(x * x, axis=-1, keepdims=True) + RMS_EPS) * g
    return y * (1.0 + scale) + shift


def _inproj_kernel(x_ref, mod_ref, g_ref, w_ref, coef_ref, z_ref, h_scr, acc0, acc1, *, n_tiles, n_chunks):
    j = pl.program_id(1)

    @pl.when(j == 0)
    def _():
        h = _mod_norm(x_ref[...], g_ref[...], mod_ref[0, 0:1, :], mod_ref[0, 1:2, :])
        h_scr[...] = h.astype(BF16)

    def matmul(acc):
        acc[...] = jnp.dot(h_scr[...], w_ref[0], preferred_element_type=F32)

    def epilogue(acc):
        lo_lane = lax.broadcasted_iota(jnp.int32, (1, LANES), 1) < HALF
        for c in range(n_chunks):
            sl = slice(c * LANES, (c + 1) * LANES)
            y = acc[:, sl]
            y2 = y * y
            lo = jnp.sum(jnp.where(lo_lane, y2, 0.0), axis=-1, keepdims=True)
            hi = jnp.sum(jnp.where(lo_lane, 0.0, y2), axis=-1, keepdims=True)
            ss = jnp.where(coef_ref[0, 3:4, sl] > 0.0, jnp.where(lo_lane, lo, hi), lo + hi)
            r = lax.rsqrt(ss * coef_ref[0, 1:2, sl] + RMS_EPS) * coef_ref[0, 0:1, sl]
            z_ref[:, sl] = (y * jnp.where(coef_ref[0, 2:3, sl] > 0.0, r, 1.0)).astype(BF16)

    inner = jnp.logical_and(j > 0, j < n_tiles)
    even = j % 2 == 0

    @pl.when(j == 0)
    def _():
        matmul(acc0)

    @pl.when(jnp.logical_and(inner, even))
    def _():
        matmul(acc0)
        epilogue(acc1)

    @pl.when(jnp.logical_and(inner, jnp.logical_not(even)))
    def _():
        matmul(acc1)
        epilogue(acc0)

    @pl.when(j == n_tiles)
    def _():
        epilogue(acc0 if (n_tiles - 1) % 2 == 0 else acc1)


def _inproj(x2, mod, g, wz, coef, layer, lay, seq, tm=1024):
    t, d = x2.shape
    tm = min(tm, seq)
    tn = lay.chunks_per_tile * LANES
    n_tiles = lay.nz // tn
    per_batch = seq // tm
    kern = functools.partial(_inproj_kernel, n_tiles=n_tiles, n_chunks=lay.chunks_per_tile)
    prev = lambda j: jnp.maximum(j - 1, 0)
    return pl.pallas_call(
        kern,
        out_shape=jax.ShapeDtypeStruct((t, lay.nz), BF16),
        grid=(t // tm, n_tiles + 1),
        in_specs=[pl.BlockSpec((tm, d), lambda i, j: (i, 0)),
                  pl.BlockSpec((1, 6, d), lambda i, j: (i // per_batch, 0, 0)),
                  pl.BlockSpec((1, d), lambda i, j: (0, 0)),
                  pl.BlockSpec((1, d, tn), lambda i, j: (layer, 0, jnp.minimum(j, n_tiles - 1))),
                  pl.BlockSpec((1, 4, tn), lambda i, j: (layer, 0, prev(j)))],
        out_specs=pl.BlockSpec((tm, tn), lambda i, j: (i, prev(j))),
        scratch_shapes=[pltpu.VMEM((tm, d), BF16), pltpu.VMEM((tm, tn), F32), pltpu.VMEM((tm, tn), F32)],
        compiler_params=_params("parallel", "arbitrary"),
        name="inproj",
    )(x2, mod, g, wz, coef)


def _band_kernel(sink_ref, q_ref, k_ref, v_ref, tab_ref, qmask_ref, o_ref, *, n_side, ck, nkc, hp, q_of,
                 nsub):
    g = pl.program_id(1)
    i = pl.program_id(2)
    for u in range(nsub):
        qi = i * nsub + u
        rows = slice(u * ck, (u + 1) * ck)
        chunks = []
        for d in range(-n_side, n_side + 1):
            kc = qi + d
            valid = jnp.logical_and(kc >= 0, kc < nkc)
            start = pl.multiple_of(jnp.clip(kc, 0, nkc - 1) * ck, ck)
            chunks.append((d + n_side, start, jnp.where(valid, 0.0, NEG_INF)))
        for h in range(hp):
            q = q_ref[rows, q_of(h) * LANES:(q_of(h) + 1) * LANES] * qmask_ref[h]
            sink = sink_ref[g * hp + h]
            m = jnp.full((ck, 1), sink, F32)
            scores = []
            for di, start, penalty in chunks:
                s = lax.dot_general(q, k_ref[pl.ds(start, ck), :], _NT, preferred_element_type=F32)
                s = s + tab_ref[h, di] + penalty
                m = jnp.maximum(m, jnp.max(s, axis=-1, keepdims=True))
                scores.append(s)
            den = jnp.exp2(sink - m)
            acc = jnp.zeros((ck, LANES), F32)
            for s, (di, start, penalty) in zip(scores, chunks):
                p = jnp.exp2(s - m)
                den = den + jnp.sum(p, axis=-1, keepdims=True)
                acc = acc + jnp.dot(p.astype(BF16), v_ref[pl.ds(start, ck), :],
                                    preferred_element_type=F32)
            o_ref[rows, h * LANES:(h + 1) * LANES] = (acc / den).astype(BF16)


def _band_attention(z, table, qmask, sink, *, batch, seq, n_groups, hp, q_width, q_of, qblk, kblk, vblk,
                    n_side, nsub=4):
    t = z.shape[0]
    ck = table.shape[-1]
    nsub = min(nsub, seq // ck)
    tq = nsub * ck
    nq = seq // tq
    kern = functools.partial(_band_kernel, n_side=n_side, ck=ck, nkc=seq // ck, hp=hp, q_of=q_of, nsub=nsub)
    nd = 2 * n_side + 1
    return pl.pallas_call(
        kern,
        out_shape=jax.ShapeDtypeStruct((t, n_groups * hp * LANES), BF16),
        grid=(batch, n_groups, nq),
        in_specs=[pl.BlockSpec(memory_space=pltpu.SMEM),
                  pl.BlockSpec((tq, q_width), lambda b, g, i: (b * nq + i, qblk(g))),
                  pl.BlockSpec((seq, LANES), lambda b, g, i: (b, kblk(g))),
                  pl.BlockSpec((seq, LANES), lambda b, g, i: (b, vblk(g))),
                  pl.BlockSpec((hp, nd, ck, ck), lambda b, g, i: (g, 0, 0, 0)),
                  pl.BlockSpec((hp, 1, LANES), lambda b, g, i: (g, 0, 0))],
        out_specs=pl.BlockSpec((tq, hp * LANES), lambda b, g, i: (b * nq + i, g)),
        compiler_params=_params("parallel", "parallel", "arbitrary"),
        name="band_attention",
    )(sink, z, z, z, table, qmask)


def _alibi_slopes(n):
    return np.array([2.0 ** (-8.0 * (i + 1) / n) for i in range(n)], dtype=np.float32)


def _dilated_table(tq):
    reach = max(w // 2 for w, _ in A_PATTERNS)
    n_side = -(-reach // tq)
    d = np.arange(-n_side, n_side + 1)[:, None, None] * tq
    off = d + np.arange(tq)[None, None, :] - np.arange(tq)[None, :, None]
    mult = np.zeros(off.shape, np.float64)
    for window, dil in A_PATTERNS:
        mult += ((off % dil) == 0) & (np.abs(off) <= window // 2)
    logm = np.where(mult > 0, np.log(np.maximum(mult, 1.0)), 0.0)
    slopes = _alibi_slopes(A_HEADS).astype(np.float64)
    tab = (-slopes[:, None, None, None] * np.abs(off)[None] + logm[None]) * _LOG2E
    tab = np.where(mult[None] > 0, tab, NEG_INF)
    return jnp.asarray(tab.astype(np.float32)), n_side


def _window_table(tq):
    n_side = -(-C_RADIUS // tq)
    d = np.arange(-n_side, n_side + 1)[:, None, None] * tq
    off = d + np.arange(tq)[None, None, :] - np.arange(tq)[None, :, None]
    slopes = _alibi_slopes(C_HEADS).astype(np.float64)
    tab = -slopes[:, None, None, None] * np.abs(off)[None] * _LOG2E
    tab = np.where((np.abs(off) <= C_RADIUS)[None], tab, NEG_INF)
    return jnp.asarray(tab.astype(np.float32)), n_side


_D_QROWS = 8
_D_KROWS = 16


_D_HEADS_PER_STEP = 2


def _nbr_kernel(q_ref, k_ref, v_ref, tab_ref, o_ref, *, rows):
    t = pl.program_id(2)
    krow = jnp.clip(t * _D_QROWS - D_WIN_R // 2, 0, rows - _D_KROWS)
    start = pl.multiple_of(krow * GRID_W, GRID_W * (D_WIN_R // 2))
    nk = _D_KROWS * GRID_W
    for h in range(_D_HEADS_PER_STEP):
        lanes = slice(h * LANES, (h + 1) * LANES)
        s = lax.dot_general(q_ref[:, lanes], k_ref[pl.ds(start, nk), lanes], _NT,
                            preferred_element_type=F32)
        s = s + tab_ref[h, 0]
        m = jnp.max(s, axis=-1, keepdims=True)
        p = jnp.exp2(s - m)
        den = jnp.sum(p, axis=-1, keepdims=True)
        acc = jnp.dot(p.astype(BF16), v_ref[pl.ds(start, nk), lanes], preferred_element_type=F32)
        o_ref[:, lanes] = (acc / den).astype(BF16)


def _nbr_tables(rpb, rows):
    depth, heads, n_dr, n_dc = rpb.shape
    nt = rows // _D_QROWS
    wr = min(D_WIN_R, rows)
    pad_l = GRID_W - 1 - (D_WIN_C - 1)
    vpad = jnp.pad(rpb.astype(F32), ((0, 0), (0, 0), (0, 0), (pad_l, 2 * GRID_W - 1 - n_dc - pad_l)))
    toep = jnp.stack([vpad[..., GRID_W - 1 - c:2 * GRID_W - 1 - c] for c in range(GRID_W)], axis=-2)
    c = np.arange(GRID_W)
    cs = np.clip(c - D_WIN_C // 2, 0, GRID_W - D_WIN_C)
    col_ok = (c[None, :] >= cs[:, None]) & (c[None, :] < cs[:, None] + D_WIN_C)
    colmat = jnp.where(col_ok, toep * _LOG2E, NEG_INF)
    masked = jnp.full((depth, heads, 1, GRID_W, GRID_W), NEG_INF, F32)
    colmat = jnp.concatenate([colmat, masked], axis=2)
    sel = np.full((3, _D_QROWS, _D_KROWS), n_dr, np.int32)
    for kind, t in enumerate((0, 1, nt - 1)):
        krow0 = int(np.clip(t * _D_QROWS - D_WIN_R // 2, 0, rows - _D_KROWS))
        for ri in range(_D_QROWS):
            r = t * _D_QROWS + ri
            rs = int(np.clip(r - wr // 2, 0, rows - wr))
            for kj in range(_D_KROWS):
                kr = krow0 + kj
                if rs <= kr < rs + wr:
                    sel[kind, ri, kj] = kr - r + (D_WIN_R - 1)
    blocks = jnp.take(colmat, jnp.asarray(sel.reshape(-1)), axis=2)
    blocks = blocks.reshape(depth * heads, 3, _D_QROWS, _D_KROWS, GRID_W, GRID_W)
    blocks = blocks.transpose(0, 1, 2, 4, 3, 5)
    return blocks.reshape(depth * heads, 3, _D_QROWS * GRID_W, _D_KROWS * GRID_W)


def _nbr_attention(z, table, layer, *, batch, seq, lay):
    t = z.shape[0]
    rows = seq // GRID_W
    nt = rows // _D_QROWS
    tq = _D_QROWS * GRID_W
    nk = _D_KROWS * GRID_W
    kern = functools.partial(_nbr_kernel, rows=rows)
    hp = _D_HEADS_PER_STEP
    groups = D_HEADS // hp
    width = hp * LANES
    assert lay.QD0 % hp == 0 and lay.KD0 % hp == 0 and lay.VD0 % hp == 0
    kind = lambda i: jnp.where(i == 0, 0, jnp.where(i == nt - 1, 2, 1))
    return pl.pallas_call(
        kern,
        out_shape=jax.ShapeDtypeStruct((t, D_HEADS * LANES), BF16),
        grid=(batch, groups, nt),
        in_specs=[pl.BlockSpec((tq, width), lambda b, g, i: (b * nt + i, lay.QD0 // hp + g)),
                  pl.BlockSpec((seq, width), lambda b, g, i: (b, lay.KD0 // hp + g)),
                  pl.BlockSpec((seq, width), lambda b, g, i: (b, lay.VD0 // hp + g)),
                  pl.BlockSpec((hp, 1, tq, nk), lambda b, g, i: (layer * groups + g, kind(i), 0, 0))],
        out_specs=pl.BlockSpec((tq, width), lambda b, g, i: (b * nt + i, g)),
        compiler_params=_params("parallel", "parallel", "arbitrary"),
        name="nbr_attention",
    )(z, z, z, table)


_N_BIAS_PIECES = 3
_KPOS_RADIX = 64


def _bf16_round(v):
    u = np.asarray(v, np.float32).view(np.uint32)
    u = (u + np.uint32(0x7FFF) + ((u >> np.uint32(16)) & np.uint32(1))) & np.uint32(0xFFFF0000)
    return u.view(np.float32)


def _diff_feature_tables(tq, tk):
    c32 = (_alibi_slopes(B_HEADS).astype(np.float64) * np.log2(np.e)).astype(np.float32)
    qconst = np.zeros((B_HEADS, 2, LANES), np.float32)
    for h in range(B_HEADS):
        rem = np.float32(c32[h])
        pieces = []
        for _ in range(_N_BIAS_PIECES):
            p = np.float32(_bf16_round(rem))
            pieces.append(p)
            rem = np.float32(rem - p)
        for br, base in ((0, HALF), (1, 0)):
            for n, p in enumerate(pieces):
                qconst[h, br, base + n] = -_KPOS_RADIX * p
                qconst[h, br, base + _N_BIAS_PIECES + n] = -p
    j = np.arange(tk)
    kfeat = np.zeros((2, tk, LANES), np.float32)
    for br, base in ((0, HALF), (1, 0)):
        for n in range(_N_BIAS_PIECES):
            kfeat[br, :, base + n] = j // _KPOS_RADIX
            kfeat[br, :, base + _N_BIAS_PIECES + n] = j % _KPOS_RADIX
            kfeat[br, :, base + 2 * _N_BIAS_PIECES + n] = 1.0
    dist = np.abs(np.arange(tq)[None, :] - np.arange(tq)[:, None]).astype(np.float64)
    ttab = (-c32.astype(np.float64)[:, None, None] * dist[None]).astype(np.float32)
    return jnp.asarray(qconst), jnp.asarray(kfeat, BF16), jnp.asarray(c32), jnp.asarray(ttab)


_WINDOW_BLOCKS = (5, 11)
_UNDERFLOW_MARGIN = 160.0


def _diff_kernel(c_ref, misc_ref, q_ref, k_ref, v_ref, qconst_ref, kfeat_ref, ttab_ref, dl_ref, subg_ref, o_ref,
                 kaug_scr, kmax_scr, *, tq, tk, nk):
    h = pl.program_id(1)
    i = pl.program_id(2)
    c = c_ref[h]
    lam_init = misc_ref[0]
    lane = lax.broadcasted_iota(jnp.int32, (1, LANES), 1)
    lo_lane = lane < HALF

    @pl.when(i == 0)
    def _():
        def aug(cc, kmax2):
            sl = pl.ds(pl.multiple_of(cc * tk, tk), tk)
            kb = k_ref[sl, :]
            kaug_scr[0, sl, :] = jnp.where(lo_lane, kb, kfeat_ref[0])
            kaug_scr[1, sl, :] = jnp.where(lo_lane, kfeat_ref[1], kb)
            kf = kb.astype(F32)
            norm2 = jnp.sum(kf * kf, axis=-1, keepdims=True)
            return jnp.maximum(kmax2, jnp.max(norm2, axis=0, keepdims=True))
        kmax2 = lax.fori_loop(0, nk, aug, jnp.zeros((1, 1), F32))
        kmax_scr[...] = jnp.broadcast_to(kmax2, kmax_scr.shape)

    q = q_ref[...].astype(F32)
    ci = c * lax.broadcasted_iota(jnp.int32, (tq, 1), 0).astype(F32)
    r1 = ci.astype(BF16).astype(F32)
    r2 = (ci - r1).astype(BF16).astype(F32)
    r3 = (ci - r1 - r2).astype(BF16).astype(F32)
    q_plain, q_right, q_left = [], [], []
    for br, base in ((0, HALF), (1, 0)):
        keep = lo_lane if br == 0 else jnp.logical_not(lo_lane)
        f0 = base + 2 * _N_BIAS_PIECES
        feat = qconst_ref[0, br:br + 1, :] + jnp.where(
            lane == f0, r1, jnp.where(lane == f0 + 1, r2, jnp.where(lane == f0 + 2, r3, 0.0)))
        q_plain.append(jnp.where(keep, q, 0.0).astype(BF16))
        q_right.append(jnp.where(keep, q, feat).astype(BF16))
        q_left.append(jnp.where(keep, q, -feat).astype(BF16))

    def update(st, s, shift, vb):
        m_old, l_old, acc_old = st
        m_new = jnp.maximum(m_old, jnp.max(s, axis=-1, keepdims=True) + shift)
        alpha = jnp.exp2(m_old - m_new)
        p = jnp.exp2(s + (shift - m_new))
        l_new = alpha * l_old + jnp.sum(p, axis=-1, keepdims=True)
        acc_new = alpha * acc_old + jnp.dot(p.astype(BF16), vb, preferred_element_type=F32)
        return m_new, l_new, acc_new

    cdiag = (i * tq) // tk

    def side_block(rows, on_right, base, st):
        ss = [lax.dot_general(jnp.where(on_right, q_right[idx], q_left[idx]), kaug_scr[idx, rows, :], _NT,
                              preferred_element_type=F32) for idx in range(2)]
        shift = jnp.where(on_right, -1.0, 1.0) * c * (base - i * tq).astype(F32)
        vb = v_ref[rows, :]
        return [update(st[idx], ss[idx], shift, vb) for idx in range(2)]

    def chunk_rows(cc):
        return pl.ds(pl.multiple_of(cc * tk, tk), tk)

    def square_block():
        st = [(jnp.full((tq, 1), NEG_INF, F32), jnp.zeros((tq, 1), F32), jnp.zeros((tq, LANES), F32))
              for _ in range(2)]
        sq = pl.ds(pl.multiple_of(i * tq, tq), tq)
        ss = [lax.dot_general(q_plain[idx], kaug_scr[idx, sq, :], _NT, preferred_element_type=F32)
              + ttab_ref[0] for idx in range(2)]
        vb = v_ref[sq, :]
        return [update(st[idx], ss[idx], 0.0, vb) for idx in range(2)]

    def diag_chunk():
        st = square_block()
        n_sub = tk // tq
        usq = i % n_sub
        dbase = cdiag * tk
        for w in range(1, n_sub):
            uo = usq + w
            on_right = uo < n_sub
            uo = jnp.where(on_right, uo, uo - n_sub)
            rows = pl.ds(pl.multiple_of(dbase + uo * tq, tq), tq)
            st = side_block(rows, on_right, dbase, st)
        return st

    def finish(st):
        hs = pl.ds(h, 1)
        lam = (jnp.exp(jnp.sum(dl_ref[0, hs, :] * dl_ref[1, hs, :], axis=-1, keepdims=True))
               - jnp.exp(jnp.sum(dl_ref[2, hs, :] * dl_ref[3, hs, :], axis=-1, keepdims=True)) + lam_init)
        o = st[0][2] / st[0][1] - lam * (st[1][2] / st[1][1])
        o = o * lax.rsqrt(jnp.mean(o * o, axis=-1, keepdims=True) + RMS_EPS) * subg_ref[...]
        o_ref[...] = (o * (1.0 - lam_init)).astype(BF16)

    def all_chunks():
        st = diag_chunk()
        for step in range(1, nk):
            ct = cdiag + step
            on_right = ct < nk
            cc = jnp.where(on_right, ct, ct - nk)
            st = side_block(chunk_rows(cc), on_right, cc * tk, st)
        finish(st)

    n_blocks = nk * (tk // tq)
    windows = tuple(w for w in _WINDOW_BLOCKS if w < n_blocks)
    if not windows:
        all_chunks()
        return

    qmax2 = jnp.max(jnp.sum(q * q, axis=-1, keepdims=True), axis=0, keepdims=True)
    need = (2.0 * jnp.sqrt(qmax2 * kmax_scr[0:1, 0:1]))[0, 0] + _UNDERFLOW_MARGIN
    far = 1e9
    n_sub = tk // tq
    taken = False
    for wb in windows:
        start = jnp.clip(i - wb // 2, 0, n_blocks - wb)
        d_left = jnp.where(start > 0, ((i - start) * tq).astype(F32), far)
        d_right = jnp.where(start + wb < n_blocks, ((start + wb - i - 1) * tq).astype(F32), far)
        fits = c * jnp.minimum(d_left, d_right) >= need
        use = fits if taken is False else jnp.logical_and(fits, jnp.logical_not(taken))

        @pl.when(use)
        def _(wb=wb, start=start):
            st = square_block()
            for w in range(wb - 1):
                bw = start + w
                blk = jnp.where(bw >= i, bw + 1, bw)
                rows = pl.ds(pl.multiple_of(blk * tq, tq), tq)
                st = side_block(rows, blk > i, (blk // n_sub) * tk, st)
            finish(st)

        taken = fits if taken is False else jnp.logical_or(taken, fits)

    @pl.when(jnp.logical_not(taken))
    def _():
        all_chunks()


def _diff_attention(z, dl, subg, tables, misc, *, batch, seq, lay, tq, tk):
    t = z.shape[0]
    nq = seq // tq
    qconst, kfeat, cvals, ttab = tables
    kern = functools.partial(_diff_kernel, tq=tq, tk=tk, nk=seq // tk)
    return pl.pallas_call(
        kern,
        out_shape=jax.ShapeDtypeStruct((t, B_HEADS * LANES), BF16),
        grid=(batch, B_HEADS, nq),
        in_specs=[pl.BlockSpec(memory_space=pltpu.SMEM),
                  pl.BlockSpec(memory_space=pltpu.SMEM),
                  pl.BlockSpec((tq, LANES), lambda b, h, i: (b * nq + i, lay.QB0 + h)),
                  pl.BlockSpec((seq, LANES), lambda b, h, i: (b, lay.KB0 + h)),
                  pl.BlockSpec((seq, LANES), lambda b, h, i: (b, lay.VB0 + h)),
                  pl.BlockSpec((1, 2, LANES), lambda b, h, i: (h, 0, 0)),
                  pl.BlockSpec(kfeat.shape, lambda b, h, i: (0, 0, 0)),
                  pl.BlockSpec((1, tq, tq), lambda b, h, i: (h, 0, 0)),
                  pl.BlockSpec(dl.shape, lambda b, h, i: (0, 0, 0)),
                  pl.BlockSpec((1, LANES), lambda b, h, i: (0, 0))],
        out_specs=pl.BlockSpec((tq, LANES), lambda b, h, i: (b * nq + i, h)),
        scratch_shapes=[pltpu.VMEM((2, seq, LANES), BF16), pltpu.VMEM((8, LANES), F32)],
        compiler_params=_params("parallel", "arbitrary", "arbitrary"),
        name="diff_attention",
    )(cvals, misc, z, z, z, qconst, kfeat, ttab, dl, subg)


def _merge_kernel(oa_ref, ob_ref, oc_ref, od_ref, gate_ref, wa_ref, wb_ref, wc_ref, wd_ref, o_ref, *, d):
    acc = None
    for idx, (o, w) in enumerate(((oa_ref, wa_ref), (ob_ref, wb_ref), (oc_ref, wc_ref), (od_ref, wd_ref))):
        y = jnp.dot(o[...], w[...], preferred_element_type=F32)
        g = _sigmoid(gate_ref[:, idx * d:(idx + 1) * d].astype(F32))
        acc = g * y if acc is None else acc + g * y
    o_ref[...] = acc.astype(BF16)


def _merge(oa, ob, oc, od, z, wbs, *, d, tm=256):
    t = oa.shape[0]
    tm = min(tm, t)
    row = lambda a: pl.BlockSpec((tm, a.shape[1]), lambda i: (i, 0))
    full = lambda a: pl.BlockSpec(a.shape, lambda i: (0, 0))
    return pl.pallas_call(
        functools.partial(_merge_kernel, d=d),
        out_shape=jax.ShapeDtypeStruct((t, d), BF16),
        grid=(t // tm,),
        in_specs=[row(oa), row(ob), row(oc), row(od),
                  pl.BlockSpec((tm, N_BRANCH * d), lambda i: (i, 0))] + [full(w) for w in wbs],
        out_specs=pl.BlockSpec((tm, d), lambda i: (i, 0)),
        compiler_params=_params("parallel"),
        name="branch_merge",
    )(oa, ob, oc, od, z, *wbs)


def _outproj_kernel(a_ref, w_ref, x_ref, g_ref, o_ref):
    y = jnp.dot(a_ref[...], w_ref[0], preferred_element_type=F32)
    o_ref[...] = x_ref[...] + g_ref[0] * y


def _outproj(a, w, layer, x2, gate, seq, tm=512):
    t, d = x2.shape
    tm = min(tm, seq)
    per_batch = seq // tm
    return pl.pallas_call(
        _outproj_kernel,
        out_shape=jax.ShapeDtypeStruct((t, d), F32),
        grid=(t // tm,),
        in_specs=[pl.BlockSpec((tm, a.shape[1]), lambda i: (i, 0)),
                  pl.BlockSpec((1,) + w.shape[1:], lambda i: (layer, 0, 0)),
                  pl.BlockSpec((tm, d), lambda i: (i, 0)),
                  pl.BlockSpec((1, 1, d), lambda i: (i // per_batch, 0, 0))],
        out_specs=pl.BlockSpec((tm, d), lambda i: (i, 0)),
        compiler_params=_params("parallel"),
        name="out_proj",
    )(a, w, x2, gate)


def _router_kernel(x_ref, mod_ref, g_ref, rwt_ref, rb_ref, h_ref, e_ref, gw_ref):
    h = _mod_norm(x_ref[...], g_ref[...], mod_ref[0, 3:4, :], mod_ref[0, 4:5, :])
    h_ref[...] = h
    logits = lax.dot_general(rwt_ref[...], h.astype(BF16), _NT, preferred_element_type=F32)
    score = _sigmoid(logits)
    sel = score + rb_ref[...]
    sel_r = [sel[e:e + 1, :] for e in range(N_EXPERTS)]
    sc_r = [score[e:e + 1, :] for e in range(N_EXPERTS)]
    n = EXPERTS_PER_GROUP

    def top2_sum(v):
        best = None
        for a in range(n):
            for b in range(a + 1, n):
                s = v[a] + v[b]
                best = s if best is None else jnp.maximum(best, s)
        return best

    gbest = top2_sum(sel_r[0:n])
    gidx = jnp.zeros(gbest.shape, jnp.int32)
    cv = list(sel_r[0:n])
    cs = list(sc_r[0:n])
    for g in range(1, N_GROUPS):
        gs = top2_sum(sel_r[g * n:(g + 1) * n])
        upd = gs > gbest
        gbest = jnp.where(upd, gs, gbest)
        gidx = jnp.where(upd, g, gidx)
        for j in range(n):
            cv[j] = jnp.where(upd, sel_r[g * n + j], cv[j])
            cs[j] = jnp.where(upd, sc_r[g * n + j], cs[j])
    b1, i1, s1 = cv[0], jnp.zeros(gbest.shape, jnp.int32), cs[0]
    for j in range(1, n):
        upd = cv[j] > b1
        b1 = jnp.where(upd, cv[j], b1)
        i1 = jnp.where(upd, j, i1)
        s1 = jnp.where(upd, cs[j], s1)
    low = -3.0e38
    b2 = jnp.full(gbest.shape, low, F32)
    i2 = jnp.zeros(gbest.shape, jnp.int32)
    s2 = jnp.zeros(gbest.shape, F32)
    for j in range(n):
        cand = jnp.where(i1 == j, low, cv[j])
        upd = cand > b2
        b2 = jnp.where(upd, cand, b2)
        i2 = jnp.where(upd, j, i2)
        s2 = jnp.where(upd, cs[j], s2)
    tot = s1 + s2
    e_ref[0:1, :] = gidx * n + i1
    e_ref[1:2, :] = gidx * n + i2
    gw_ref[0:1, :] = s1 / tot
    gw_ref[1:2, :] = s2 / tot


def _router(x2, mod, g, rwt, rb, seq, tm=512):
    t, d = x2.shape
    tm = min(tm, seq)
    per_batch = seq // tm
    return pl.pallas_call(
        _router_kernel,
        out_shape=(jax.ShapeDtypeStruct((t, d), F32),
                   jax.ShapeDtypeStruct((2, t), jnp.int32),
                   jax.ShapeDtypeStruct((2, t), F32)),
        grid=(t // tm,),
        in_specs=[pl.BlockSpec((tm, d), lambda i: (i, 0)),
                  pl.BlockSpec((1, 6, d), lambda i: (i // per_batch, 0, 0)),
                  pl.BlockSpec((1, d), lambda i: (0, 0)),
                  pl.BlockSpec(rwt.shape, lambda i: (0, 0)),
                  pl.BlockSpec(rb.shape, lambda i: (0, 0))],
        out_specs=(pl.BlockSpec((tm, d), lambda i: (i, 0)),
                   pl.BlockSpec((2, tm), lambda i: (0, i)),
                   pl.BlockSpec((2, tm), lambda i: (0, i))),
        compiler_params=_params("parallel"),
        name="norm_router",
    )(x2, mod, g, rwt, rb)


def _rank_kernel(e_ref, rank_ref, cnt_ref, carry):
    @pl.when(pl.program_id(0) == 0)
    def _():
        carry[...] = jnp.zeros(carry.shape, F32)

    tm = e_ref.shape[1]
    eio = lax.broadcasted_iota(jnp.int32, (N_EXPERTS, tm), 0)
    oh0 = eio == e_ref[0:1, :]
    oh1 = eio == e_ref[1:2, :]
    c = jnp.where(jnp.logical_or(oh0, oh1), 1.0, 0.0)
    before = (lax.broadcasted_iota(jnp.int32, (tm, tm), 0) < lax.broadcasted_iota(jnp.int32, (tm, tm), 1))
    pref = jnp.dot(c.astype(BF16), jnp.where(before, 1.0, 0.0).astype(BF16),
                   preferred_element_type=F32) + carry[...]
    rank_ref[0:1, :] = jnp.sum(jnp.where(oh0, pref, 0.0), axis=0, keepdims=True).astype(jnp.int32)
    rank_ref[1:2, :] = jnp.sum(jnp.where(oh1, pref, 0.0), axis=0, keepdims=True).astype(jnp.int32)
    total = carry[...] + jnp.sum(c, axis=1, keepdims=True)
    carry[...] = total
    cnt_ref[...] = jnp.broadcast_to(total, cnt_ref.shape)


def _rank(eidx, tm=512):
    t = eidx.shape[1]
    tm = min(tm, t)
    return pl.pallas_call(
        _rank_kernel,
        out_shape=(jax.ShapeDtypeStruct((2, t), jnp.int32),
                   jax.ShapeDtypeStruct((N_EXPERTS, LANES), F32)),
        grid=(t // tm,),
        in_specs=[pl.BlockSpec((2, tm), lambda i: (0, i))],
        out_specs=(pl.BlockSpec((2, tm), lambda i: (0, i)),
                   pl.BlockSpec((N_EXPERTS, LANES), lambda i: (0, 0))),
        scratch_shapes=[pltpu.VMEM((N_EXPERTS, 1), F32)],
        compiler_params=_params("arbitrary"),
        name="expert_rank",
    )(eidx)


_DMA_UNROLL = 8


def _row_copy(src_hbm, row, dst, slot, r, sem):
    return pltpu.make_async_copy(src_hbm.at[pl.ds(row, 1), :], dst.at[slot, pl.ds(r, 1), :], sem.at[slot])


def _expert_kernel(te_ref, nu_ref, src_ref, h_hbm, w1_ref, w3_ref, w2_ref, y_ref, xbuf, sem, *, tm):
    j = pl.program_id(0)
    n_used = nu_ref[0]

    def wait_tile(slot):
        pltpu.make_async_copy(h_hbm.at[pl.ds(0, tm), :], xbuf.at[slot], sem.at[slot]).wait()

    @pl.when(jnp.logical_and(j == 0, n_used > 0))
    def _():
        def issue(r, carry):
            _row_copy(h_hbm, src_ref[r], xbuf, 0, r, sem).start()
            return carry
        lax.fori_loop(0, tm, issue, 0, unroll=_DMA_UNROLL)

    @pl.when(j < n_used)
    def _():
        slot = j % 2
        wait_tile(slot)
        nxt = jnp.minimum(j + 1, n_used - 1) * tm
        for r in range(tm):
            _row_copy(h_hbm, src_ref[nxt + r], xbuf, 1 - slot, r, sem).start()
        xs = xbuf[slot].astype(BF16)
        g = jnp.dot(xs, w1_ref[0], preferred_element_type=F32)
        u = jnp.dot(xs, w3_ref[0], preferred_element_type=F32)
        mid = (g * _sigmoid(g) * u).astype(BF16)
        y_ref[...] = jnp.dot(mid, w2_ref[0], preferred_element_type=F32)

    @pl.when(j == n_used - 1)
    def _():
        wait_tile(1 - j % 2)

    @pl.when(j >= n_used)
    def _():
        y_ref[...] = jnp.zeros(y_ref.shape, F32)


def _experts(tile_e, n_used, src, h, w1, w3, w2, layer, tm):
    p = src.shape[0]
    d = h.shape[1]
    f = w1.shape[-1]
    n_exp = w1.shape[1]
    w1r = w1.reshape((-1,) + w1.shape[2:])
    w3r = w3.reshape((-1,) + w3.shape[2:])
    w2r = w2.reshape((-1,) + w2.shape[2:])
    wmap = lambda j, te, nu, src: (layer * n_exp + te[j], 0, 0)
    return pl.pallas_call(
        functools.partial(_expert_kernel, tm=tm),
        out_shape=jax.ShapeDtypeStruct((p, d), F32),
        grid_spec=pltpu.PrefetchScalarGridSpec(
            num_scalar_prefetch=3,
            grid=(p // tm,),
            in_specs=[pl.BlockSpec(memory_space=pl.ANY),
                      pl.BlockSpec((1, d, f), wmap),
                      pl.BlockSpec((1, d, f), wmap),
                      pl.BlockSpec((1, f, d), wmap)],
            out_specs=pl.BlockSpec((tm, d), lambda j, te, nu, src: (j, 0)),
            scratch_shapes=[pltpu.VMEM((2, tm, d), F32), pltpu.SemaphoreType.DMA((2,))]),
        compiler_params=_params("arbitrary"),
        name="experts",
    )(tile_e, n_used, src, h, w1r, w3r, w2r)


def _combine_kernel(pos_ref, x_ref, g_ref, gw_ref, y_hbm, o_ref, buf0, buf1, sem0, sem1, *, tm, t, n_steps):
    i = pl.program_id(0)

    def start_step(step, slot):
        def issue(r, carry):
            _row_copy(y_hbm, pos_ref[step * tm + r], buf0, slot, r, sem0).start()
            _row_copy(y_hbm, pos_ref[t + step * tm + r], buf1, slot, r, sem1).start()
            return carry
        lax.fori_loop(0, tm, issue, 0, unroll=_DMA_UNROLL)

    def wait_step(slot):
        pltpu.make_async_copy(y_hbm.at[pl.ds(0, tm), :], buf0.at[slot], sem0.at[slot]).wait()
        pltpu.make_async_copy(y_hbm.at[pl.ds(0, tm), :], buf1.at[slot], sem1.at[slot]).wait()

    @pl.when(i == 0)
    def _():
        start_step(0, 0)

    @pl.when(i + 1 < n_steps)
    def _():
        start_step(i + 1, (i + 1) % 2)

    slot = i % 2
    wait_step(slot)
    moe = gw_ref[:, 0:1] * buf0[slot] + gw_ref[:, 1:2] * buf1[slot]
    o_ref[...] = x_ref[...] + g_ref[0] * moe


def _combine(pos, x2, gate, gwt, y, seq, tm=256):
    t, d = x2.shape
    tm = min(tm, seq)
    per_batch = seq // tm
    n_steps = t // tm
    return pl.pallas_call(
        functools.partial(_combine_kernel, tm=tm, t=t, n_steps=n_steps),
        out_shape=jax.ShapeDtypeStruct((t, d), F32),
        grid_spec=pltpu.PrefetchScalarGridSpec(
            num_scalar_prefetch=1,
            grid=(n_steps,),
            in_specs=[pl.BlockSpec((tm, d), lambda i, pos: (i, 0)),
                      pl.BlockSpec((1, 1, d), lambda i, pos: (i // per_batch, 0, 0)),
                      pl.BlockSpec((tm, 2), lambda i, pos: (i, 0)),
                      pl.BlockSpec(memory_space=pl.ANY)],
            out_specs=pl.BlockSpec((tm, d), lambda i, pos: (i, 0)),
            scratch_shapes=[pltpu.VMEM((2, tm, d), F32), pltpu.VMEM((2, tm, d), F32),
                            pltpu.SemaphoreType.DMA((2,)), pltpu.SemaphoreType.DMA((2,))]),
        compiler_params=_params("arbitrary"),
        name="moe_combine",
    )(pos, x2, gate, gwt, y)


def _moe(x2, mod, g2n, rwt, rb, w1, w3, w2, layer, *, batch, seq, tm_e=256):
    t, d = x2.shape
    h, eidx, gw = _router(x2, mod, g2n, rwt, rb, seq)
    rank, cnt = _rank(eidx)
    tm_e = min(tm_e, t)
    counts = cnt[:, 0].astype(jnp.int32)
    gsz = ((counts + tm_e - 1) // tm_e) * tm_e
    ends = jnp.cumsum(gsz)
    offs = ends - gsz
    pos = rank
    for e in range(N_EXPERTS):
        pos = pos + jnp.where(eidx == e, offs[e], 0)
    n_tiles = (2 * t) // tm_e + N_EXPERTS
    p = n_tiles * tm_e
    flat = pos.reshape(-1)
    tok = jnp.broadcast_to(jnp.arange(t, dtype=jnp.int32)[None, :], (2, t)).reshape(-1)
    src = jnp.zeros((p,), jnp.int32).at[flat].set(tok)
    tile_start = jnp.arange(n_tiles, dtype=jnp.int32) * tm_e
    tile_e = jnp.minimum(jnp.sum(tile_start[:, None] >= ends[None, :], axis=1),
                         N_EXPERTS - 1).astype(jnp.int32)
    n_used = (ends[-1:] // tm_e).astype(jnp.int32)
    y = _experts(tile_e, n_used, src, h, w1, w3, w2, layer, tm_e)
    return _combine(flat, x2, mod[:, 5:6, :], gw.T, y, seq)


def kernel(x, c, ada_w, ada_b, norm1_g, norm2_g, w_in, qk_norm_a, qk_norm_b, qk_norm_c, qk_norm_d,
           diff_lambda, diff_subln_g, sink_logits, rpb, w_branch, w_out, router_w, router_b, w1, w3, w2):
    batch, seq, d = x.shape
    depth = ada_w.shape[0]
    t = batch * seq
    lay = _Layout(d)
    bw = lay.bw

    mod_all = _ada_mod(c, ada_w, ada_b)
    wz = _build_wz(w_in, lay)
    coef = jnp.stack([_build_coef(lay, qk_norm_a[l], qk_norm_b[l], qk_norm_c[l], qk_norm_d[l])
                      for l in range(depth)])
    wb16 = w_branch.astype(BF16)
    c_hd = bw // C_HEADS
    grp = C_HEADS // C_KV_HEADS
    wc_rows = []
    for hq in range(C_HEADS):
        blk = jnp.zeros((depth, LANES, d), BF16)
        off = (hq // grp) * c_hd
        wc_rows.append(blk.at[:, off:off + c_hd].set(wb16[:, 2, hq * c_hd:(hq + 1) * c_hd]))
    wc_ext = jnp.concatenate(wc_rows, axis=1)
    wout16 = w_out.astype(BF16)
    w1b, w3b, w2b = w1.astype(BF16), w3.astype(BF16), w2.astype(BF16)
    rwt = router_w.T.astype(BF16)
    rb = router_b.reshape(N_EXPERTS, 1).astype(F32)

    tab_a, side_a = _dilated_table(256)
    tab_c, side_c = _window_table(256)
    ones_mask = jnp.ones((A_HEADS, 1, LANES), BF16)
    lane = np.arange(LANES)
    cmask = np.stack([(lane // HALF) == (hq % 2) for hq in range(C_HEADS)]).astype(np.float32)
    cmask = jnp.asarray(cmask.reshape(C_HEADS, 1, LANES), BF16)
    no_sink = jnp.full((A_HEADS,), NEG_INF, F32)
    tq_b, tk_b = min(512, seq), min(1024, seq)
    tables_b = _diff_feature_tables(tq_b, tk_b)
    tab_d = _nbr_tables(rpb, seq // GRID_W)

    x2 = x.reshape(t, d)
    for l in range(depth):
        mod = mod_all[l, :batch].reshape(batch, 6, d)
        z = _inproj(x2, mod, norm1_g[l].reshape(1, d), wz, coef, l, lay, seq)
        o_a = _band_attention(z, tab_a, ones_mask, no_sink, batch=batch, seq=seq, n_groups=A_HEADS, hp=1,
                              q_width=LANES, q_of=lambda h: 0,
                              qblk=lambda g: lay.QA0 + g, kblk=lambda g: lay.KA0 + g,
                              vblk=lambda g: lay.VA0 + g, n_side=side_a)
        lam_init = 0.8 - 0.6 * float(np.exp(-0.3 * l))
        o_b = _diff_attention(z, diff_lambda[l], diff_subln_g[l].reshape(1, LANES), tables_b,
                              jnp.full((1,), lam_init, F32), batch=batch, seq=seq, lay=lay,
                              tq=tq_b, tk=tk_b)
        o_c = _band_attention(z, tab_c, cmask, sink_logits[l].astype(F32) * _LOG2E, batch=batch, seq=seq,
                              n_groups=C_KV_HEADS, hp=grp, q_width=(grp // 2) * LANES,
                              q_of=lambda h: h // 2,
                              qblk=lambda g: lay.QC0 // (grp // 2) + g, kblk=lambda g: lay.KC0 + g,
                              vblk=lambda g: lay.VC0, n_side=side_c)
        o_d = _nbr_attention(z, tab_d, l, batch=batch, seq=seq, lay=lay)
        merged = _merge(o_a, o_b, o_c, o_d, z, (wb16[l, 0], wb16[l, 1], wc_ext[l], wb16[l, 3]), d=d)
        x2 = _outproj(merged, wout16, l, x2, mod[:, 2:3, :], seq)
        x2 = _moe(x2, mod, norm2_g[l].reshape(1, d), rwt, rb, w1b, w3b, w2b, l, batch=batch, seq=seq)
    return x2.reshape(batch, seq, d)
```

```python
import functools

import numpy as np
import jax
import jax.numpy as jnp
from jax import lax
from jax.experimental import pallas as pl
from jax.experimental.pallas import tpu as pltpu

F32 = jnp.float32
BF16 = jnp.bfloat16
LANES = 128
HALF = LANES // 2
RMS_EPS = 1e-6
NEG_INF = -1e30
_LOG2E = float(np.log2(np.e))
VMEM_LIMIT = 56 * 1024 * 1024

GRID_W = 64
N_BRANCH = 4
A_HEADS = 4
A_PATTERNS = ((128, 1), (512, 4), (2048, 16))
B_HEADS = 4
C_HEADS = 8
C_KV_HEADS = 2
C_RADIUS = 128
D_HEADS = 4
D_WIN_R = 8
D_WIN_C = 16
N_EXPERTS = 16
N_GROUPS = 4
EXPERTS_PER_GROUP = N_EXPERTS // N_GROUPS

_NT = (((1,), (1,)), ((), ()))


def _params(*sem):
    return pltpu.CompilerParams(dimension_semantics=sem, vmem_limit_bytes=VMEM_LIMIT)


def _sigmoid(v):
    return 1.0 / (1.0 + jnp.exp(-v))


class _Layout:
    def __init__(self, d_model):
        bw = d_model // N_BRANCH
        self.bw = bw
        hb = bw // LANES
        o = {}
        o['qa'], o['ka'], o['va'] = 0, bw, 2 * bw
        o['qb'], o['kb'], o['vb'] = 3 * bw, 4 * bw, 5 * bw
        o['qc'] = 6 * bw
        o['kc'] = 7 * bw
        o['vc'] = 7 * bw + LANES
        o['qd'], o['kd'], o['vd'] = 7 * bw + 2 * LANES, 8 * bw + 2 * LANES, 9 * bw + 2 * LANES
        o['gates'] = 10 * bw + 2 * LANES
        self.orig = o
        gate_blocks = N_BRANCH * d_model // LANES
        self.GATE0 = 0
        b = gate_blocks
        self.QA0, self.KA0, self.VA0 = b, b + hb, b + 2 * hb
        b += 3 * hb
        self.QD0, self.KD0, self.VD0 = b, b + hb, b + 2 * hb
        b += 3 * hb
        self.QB0, self.KB0, self.VB0 = b, b + hb, b + 2 * hb
        b += 3 * hb
        self.QC0 = b
        self.KC0 = b + hb
        self.VC0 = b + hb + 2
        b += hb + 3
        self.n_real = b
        self.chunks_per_tile = 12
        self.n_blocks = -(-b // self.chunks_per_tile) * self.chunks_per_tile
        self.nz = self.n_blocks * LANES
        self.first_norm_tile = gate_blocks // self.chunks_per_tile
        assert self.QC0 % (C_HEADS // C_KV_HEADS // 2) == 0
        norm = [0] * self.n_blocks
        for blk0, nblk, group in ((self.QA0, 2 * hb, bw // A_HEADS), (self.QD0, 2 * hb, bw // D_HEADS),
                                  (self.QB0, 2 * hb, bw // (2 * B_HEADS)),
                                  (self.QC0, hb + 2, bw // C_HEADS)):
            assert group in (LANES, HALF)
            norm[blk0:blk0 + nblk] = [group] * nblk
        self.norm_group = tuple(norm)

    def weight_segments(self):
        o, bw = self.orig, self.bw
        kc0, kc1 = o['kc'], o['kc'] + HALF
        segs = [(o['gates'], N_BRANCH * N_BRANCH * bw),
                (o['qa'], bw), (o['ka'], bw), (o['va'], bw),
                (o['qd'], bw), (o['kd'], bw), (o['vd'], bw),
                (o['qb'], bw), (o['kb'], bw), (o['vb'], bw),
                (o['qc'], bw),
                (kc0, HALF), (kc0, HALF), (kc1, HALF), (kc1, HALF),
                (o['vc'], LANES)]
        pad = self.nz - self.n_real * LANES
        if pad:
            segs.append((None, pad))
        return segs


def _build_wz(w_in, lay):
    parts = []
    for start, width in lay.weight_segments():
        if start is None:
            parts.append(jnp.zeros(w_in.shape[:-1] + (width,), w_in.dtype))
        else:
            parts.append(w_in[..., start:start + width])
    return jnp.concatenate(parts, axis=-1).astype(BF16)


def _build_coef(lay, qk_a, qk_b, qk_c, qk_d):
    hb = lay.bw // LANES
    a_hd = lay.bw // A_HEADS
    b_dk = lay.bw // (2 * B_HEADS)
    c_hd = lay.bw // C_HEADS
    d_hd = lay.bw // D_HEADS
    gain = jnp.ones((lay.nz,), F32)

    def put(gain, blk0, nblk, g, scale):
        lo, hi = blk0 * LANES, (blk0 + nblk) * LANES
        reps = (hi - lo) // g.shape[0]
        return gain.at[lo:hi].set(jnp.tile(g.astype(F32), reps) * scale)

    gain = put(gain, lay.QA0, hb, qk_a[0], a_hd ** -0.5 * _LOG2E)
    gain = put(gain, lay.KA0, hb, qk_a[1], 1.0)
    gain = put(gain, lay.QD0, hb, qk_d[0], d_hd ** -0.5 * _LOG2E)
    gain = put(gain, lay.KD0, hb, qk_d[1], 1.0)
    gain = put(gain, lay.QB0, hb, qk_b[0], b_dk ** -0.5 * _LOG2E)
    gain = put(gain, lay.KB0, hb, qk_b[1], 1.0)
    gain = put(gain, lay.QC0, hb, qk_c[0], c_hd ** -0.5 * _LOG2E)
    gain = put(gain, lay.KC0, 2, qk_c[1], 1.0)
    return gain.reshape(1, lay.nz)


def _ada_kernel(c_ref, w_ref, b_ref, o_ref):
    c = c_ref[...]
    cond = (c * _sigmoid(c)).astype(BF16)
    o_ref[0] = jnp.dot(cond, w_ref[0].astype(BF16), preferred_element_type=F32) + b_ref[0]


def _ada_mod(c, ada_w, ada_b, tn=1024):
    depth, d, n = ada_w.shape
    rows = 8
    cp = jnp.zeros((rows, d), F32).at[:c.shape[0]].set(c)
    tn = min(tn, n)
    return pl.pallas_call(
        _ada_kernel,
        out_shape=jax.ShapeDtypeStruct((depth, rows, n), F32),
        grid=(depth, n // tn),
        in_specs=[pl.BlockSpec((rows, d), lambda l, j: (0, 0)),
                  pl.BlockSpec((1, d, tn), lambda l, j: (l, 0, j)),
                  pl.BlockSpec((1, 1, tn), lambda l, j: (l, 0, j))],
        out_specs=pl.BlockSpec((1, rows, tn), lambda l, j: (l, 0, j)),
        compiler_params=_params("parallel", "parallel"),
        name="ada_mod",
    )(cp, ada_w, ada_b.reshape(depth, 1, n))


def _mod_norm(x, g, shift, scale):
    y = x * lax.rsqrt(jnp.mean(x * x, axis=-1, keepdims=True) + RMS_EPS) * g
    return y * (1.0 + scale) + shift


def _inproj_kernel(x_ref, mod_ref, g_ref, w_ref, gain_ref, z_ref, h_scr, *, first_norm_tile, n_chunks,
                   norm_group):
    j = pl.program_id(1)

    @pl.when(j == 0)
    def _():
        h = _mod_norm(x_ref[...], g_ref[...], mod_ref[0, 0:1, :], mod_ref[0, 1:2, :])
        h_scr[...] = h.astype(BF16)

    acc = jnp.dot(h_scr[...], w_ref[0], preferred_element_type=F32)

    @pl.when(j < first_norm_tile)
    def _():
        z_ref[...] = acc.astype(BF16)

    for tile in range(first_norm_tile, len(norm_group) // n_chunks):
        @pl.when(j == tile)
        def _(tile=tile):
            lo_lane = lax.broadcasted_iota(jnp.int32, (1, LANES), 1) < HALF
            for c in range(n_chunks):
                sl = slice(c * LANES, (c + 1) * LANES)
                group = norm_group[tile * n_chunks + c]
                y = acc[:, sl]
                if group == 0:
                    z_ref[:, sl] = y.astype(BF16)
                    continue
                y2 = y * y
                if group == LANES:
                    ss = jnp.sum(y2, axis=-1, keepdims=True)
                else:
                    lo = jnp.sum(jnp.where(lo_lane, y2, 0.0), axis=-1, keepdims=True)
                    hi = jnp.sum(jnp.where(lo_lane, 0.0, y2), axis=-1, keepdims=True)
                    ss = jnp.where(lo_lane, lo, hi)
                r = lax.rsqrt(ss * (1.0 / group) + RMS_EPS) * gain_ref[0, :, sl]
                z_ref[:, sl] = (y * r).astype(BF16)


def _inproj(x2, mod, g, wz, coef, layer, lay, seq, tm=1024):
    t, d = x2.shape
    tm = min(tm, seq)
    tn = lay.chunks_per_tile * LANES
    per_batch = seq // tm
    kern = functools.partial(_inproj_kernel, first_norm_tile=lay.first_norm_tile,
                             n_chunks=lay.chunks_per_tile, norm_group=lay.norm_group)
    return pl.pallas_call(
        kern,
        out_shape=jax.ShapeDtypeStruct((t, lay.nz), BF16),
        grid=(t // tm, lay.nz // tn),
        in_specs=[pl.BlockSpec((tm, d), lambda i, j: (i, 0)),
                  pl.BlockSpec((1, 6, d), lambda i, j: (i // per_batch, 0, 0)),
                  pl.BlockSpec((1, d), lambda i, j: (0, 0)),
                  pl.BlockSpec((1, d, tn), lambda i, j: (layer, 0, j)),
                  pl.BlockSpec((1, 1, tn), lambda i, j: (layer, 0, j))],
        out_specs=pl.BlockSpec((tm, tn), lambda i, j: (i, j)),
        scratch_shapes=[pltpu.VMEM((tm, d), BF16)],
        compiler_params=_params("parallel", "arbitrary"),
        name="inproj",
    )(x2, mod, g, wz, coef)


def _band_kernel(sink_ref, q_ref, k_ref, v_ref, tab_ref, qmask_ref, o_ref, *, n_side, ck, nkc, hp, q_of,
                 nsub):
    g = pl.program_id(1)
    i = pl.program_id(2)
    for u in range(nsub):
        qi = i * nsub + u
        rows = slice(u * ck, (u + 1) * ck)
        chunks = []
        for d in range(-n_side, n_side + 1):
            kc = qi + d
            valid = jnp.logical_and(kc >= 0, kc < nkc)
            start = pl.multiple_of(jnp.clip(kc, 0, nkc - 1) * ck, ck)
            chunks.append((d + n_side, start, jnp.where(valid, 0.0, NEG_INF)))
        for h in range(hp):
            q = q_ref[rows, q_of(h) * LANES:(q_of(h) + 1) * LANES] * qmask_ref[h]
            sink = sink_ref[g * hp + h]
            m = jnp.full((ck, 1), sink, F32)
            scores = []
            for di, start, penalty in chunks:
                s = lax.dot_general(q, k_ref[pl.ds(start, ck), :], _NT, preferred_element_type=F32)
                s = s + tab_ref[h, di] + penalty
                m = jnp.maximum(m, jnp.max(s, axis=-1, keepdims=True))
                scores.append(s)
            den = jnp.exp2(sink - m)
            acc = jnp.zeros((ck, LANES), F32)
            for s, (di, start, penalty) in zip(scores, chunks):
                p = jnp.exp2(s - m)
                den = den + jnp.sum(p, axis=-1, keepdims=True)
                acc = acc + jnp.dot(p.astype(BF16), v_ref[pl.ds(start, ck), :],
                                    preferred_element_type=F32)
            o_ref[rows, h * LANES:(h + 1) * LANES] = (acc / den).astype(BF16)


def _band_attention(z, table, qmask, sink, *, batch, seq, n_groups, hp, q_width, q_of, qblk, kblk, vblk,
                    n_side, nsub=4):
    t = z.shape[0]
    ck = table.shape[-1]
    nsub = min(nsub, seq // ck)
    tq = nsub * ck
    nq = seq // tq
    kern = functools.partial(_band_kernel, n_side=n_side, ck=ck, nkc=seq // ck, hp=hp, q_of=q_of, nsub=nsub)
    nd = 2 * n_side + 1
    return pl.pallas_call(
        kern,
        out_shape=jax.ShapeDtypeStruct((t, n_groups * hp * LANES), BF16),
        grid=(batch, n_groups, nq),
        in_specs=[pl.BlockSpec(memory_space=pltpu.SMEM),
                  pl.BlockSpec((tq, q_width), lambda b, g, i: (b * nq + i, qblk(g))),
                  pl.BlockSpec((seq, LANES), lambda b, g, i: (b, kblk(g))),
                  pl.BlockSpec((seq, LANES), lambda b, g, i: (b, vblk(g))),
                  pl.BlockSpec((hp, nd, ck, ck), lambda b, g, i: (g, 0, 0, 0)),
                  pl.BlockSpec((hp, 1, LANES), lambda b, g, i: (g, 0, 0))],
        out_specs=pl.BlockSpec((tq, hp * LANES), lambda b, g, i: (b * nq + i, g)),
        compiler_params=_params("parallel", "parallel", "arbitrary"),
        name="band_attention",
    )(sink, z, z, z, table, qmask)


def _alibi_slopes(n):
    return np.array([2.0 ** (-8.0 * (i + 1) / n) for i in range(n)], dtype=np.float32)


def _dilated_table(tq):
    reach = max(w // 2 for w, _ in A_PATTERNS)
    n_side = -(-reach // tq)
    d = np.arange(-n_side, n_side + 1)[:, None, None] * tq
    off = d + np.arange(tq)[None, None, :] - np.arange(tq)[None, :, None]
    mult = np.zeros(off.shape, np.float64)
    for window, dil in A_PATTERNS:
        mult += ((off % dil) == 0) & (np.abs(off) <= window // 2)
    logm = np.where(mult > 0, np.log(np.maximum(mult, 1.0)), 0.0)
    slopes = _alibi_slopes(A_HEADS).astype(np.float64)
    tab = (-slopes[:, None, None, None] * np.abs(off)[None] + logm[None]) * _LOG2E
    tab = np.where(mult[None] > 0, tab, NEG_INF)
    return jnp.asarray(tab.astype(np.float32)), n_side


def _window_table(tq):
    n_side = -(-C_RADIUS // tq)
    d = np.arange(-n_side, n_side + 1)[:, None, None] * tq
    off = d + np.arange(tq)[None, None, :] - np.arange(tq)[None, :, None]
    slopes = _alibi_slopes(C_HEADS).astype(np.float64)
    tab = -slopes[:, None, None, None] * np.abs(off)[None] * _LOG2E
    tab = np.where((np.abs(off) <= C_RADIUS)[None], tab, NEG_INF)
    return jnp.asarray(tab.astype(np.float32)), n_side


_D_QROWS = 8
_D_KROWS = 16


_D_HEADS_PER_STEP = 2


def _nbr_kernel(q_ref, k_ref, v_ref, tab_ref, o_ref, *, rows):
    t = pl.program_id(2)
    krow = jnp.clip(t * _D_QROWS - D_WIN_R // 2, 0, rows - _D_KROWS)
    start = pl.multiple_of(krow * GRID_W, GRID_W * (D_WIN_R // 2))
    nk = _D_KROWS * GRID_W
    for h in range(_D_HEADS_PER_STEP):
        lanes = slice(h * LANES, (h + 1) * LANES)
        s = lax.dot_general(q_ref[:, lanes], k_ref[pl.ds(start, nk), lanes], _NT,
                            preferred_element_type=F32)
        s = s + tab_ref[h, 0]
        m = jnp.max(s, axis=-1, keepdims=True)
        p = jnp.exp2(s - m)
        den = jnp.sum(p, axis=-1, keepdims=True)
        acc = jnp.dot(p.astype(BF16), v_ref[pl.ds(start, nk), lanes], preferred_element_type=F32)
        o_ref[:, lanes] = (acc / den).astype(BF16)


def _nbr_tables(rpb, rows):
    depth, heads, n_dr, n_dc = rpb.shape
    nt = rows // _D_QROWS
    wr = min(D_WIN_R, rows)
    pad_l = GRID_W - 1 - (D_WIN_C - 1)
    vpad = jnp.pad(rpb.astype(F32), ((0, 0), (0, 0), (0, 0), (pad_l, 2 * GRID_W - 1 - n_dc - pad_l)))
    toep = jnp.stack([vpad[..., GRID_W - 1 - c:2 * GRID_W - 1 - c] for c in range(GRID_W)], axis=-2)
    c = np.arange(GRID_W)
    cs = np.clip(c - D_WIN_C // 2, 0, GRID_W - D_WIN_C)
    col_ok = (c[None, :] >= cs[:, None]) & (c[None, :] < cs[:, None] + D_WIN_C)
    colmat = jnp.where(col_ok, toep * _LOG2E, NEG_INF)
    masked = jnp.full((depth, heads, 1, GRID_W, GRID_W), NEG_INF, F32)
    colmat = jnp.concatenate([colmat, masked], axis=2)
    sel = np.full((3, _D_QROWS, _D_KROWS), n_dr, np.int32)
    for kind, t in enumerate((0, 1, nt - 1)):
        krow0 = int(np.clip(t * _D_QROWS - D_WIN_R // 2, 0, rows - _D_KROWS))
        for ri in range(_D_QROWS):
            r = t * _D_QROWS + ri
            rs = int(np.clip(r - wr // 2, 0, rows - wr))
            for kj in range(_D_KROWS):
                kr = krow0 + kj
                if rs <= kr < rs + wr:
                    sel[kind, ri, kj] = kr - r + (D_WIN_R - 1)
    blocks = jnp.take(colmat, jnp.asarray(sel.reshape(-1)), axis=2)
    blocks = blocks.reshape(depth * heads, 3, _D_QROWS, _D_KROWS, GRID_W, GRID_W)
    blocks = blocks.transpose(0, 1, 2, 4, 3, 5)
    return blocks.reshape(depth * heads, 3, _D_QROWS * GRID_W, _D_KROWS * GRID_W)


def _nbr_attention(z, table, layer, *, batch, seq, lay):
    t = z.shape[0]
    rows = seq // GRID_W
    nt = rows // _D_QROWS
    tq = _D_QROWS * GRID_W
    nk = _D_KROWS * GRID_W
    kern = functools.partial(_nbr_kernel, rows=rows)
    hp = _D_HEADS_PER_STEP
    groups = D_HEADS // hp
    width = hp * LANES
    assert lay.QD0 % hp == 0 and lay.KD0 % hp == 0 and lay.VD0 % hp == 0
    kind = lambda i: jnp.where(i == 0, 0, jnp.where(i == nt - 1, 2, 1))
    return pl.pallas_call(
        kern,
        out_shape=jax.ShapeDtypeStruct((t, D_HEADS * LANES), BF16),
        grid=(batch, groups, nt),
        in_specs=[pl.BlockSpec((tq, width), lambda b, g, i: (b * nt + i, lay.QD0 // hp + g)),
                  pl.BlockSpec((seq, width), lambda b, g, i: (b, lay.KD0 // hp + g)),
                  pl.BlockSpec((seq, width), lambda b, g, i: (b, lay.VD0 // hp + g)),
                  pl.BlockSpec((hp, 1, tq, nk), lambda b, g, i: (layer * groups + g, kind(i), 0, 0))],
        out_specs=pl.BlockSpec((tq, width), lambda b, g, i: (b * nt + i, g)),
        compiler_params=_params("parallel", "parallel", "arbitrary"),
        name="nbr_attention",
    )(z, z, z, table)


_N_BIAS_PIECES = 3
_KPOS_RADIX = 64


def _bf16_round(v):
    u = np.asarray(v, np.float32).view(np.uint32)
    u = (u + np.uint32(0x7FFF) + ((u >> np.uint32(16)) & np.uint32(1))) & np.uint32(0xFFFF0000)
    return u.view(np.float32)


def _diff_feature_tables(tq, tk):
    c32 = (_alibi_slopes(B_HEADS).astype(np.float64) * np.log2(np.e)).astype(np.float32)
    qconst = np.zeros((B_HEADS, 2, LANES), np.float32)
    for h in range(B_HEADS):
        rem = np.float32(c32[h])
        pieces = []
        for _ in range(_N_BIAS_PIECES):
            p = np.float32(_bf16_round(rem))
            pieces.append(p)
            rem = np.float32(rem - p)
        for br, base in ((0, HALF), (1, 0)):
            for n, p in enumerate(pieces):
                qconst[h, br, base + n] = -_KPOS_RADIX * p
                qconst[h, br, base + _N_BIAS_PIECES + n] = -p
    j = np.arange(tk)
    kfeat = np.zeros((2, tk, LANES), np.float32)
    for br, base in ((0, HALF), (1, 0)):
        for n in range(_N_BIAS_PIECES):
            kfeat[br, :, base + n] = j // _KPOS_RADIX
            kfeat[br, :, base + _N_BIAS_PIECES + n] = j % _KPOS_RADIX
            kfeat[br, :, base + 2 * _N_BIAS_PIECES + n] = 1.0
    dist = np.abs(np.arange(tq)[None, :] - np.arange(tq)[:, None]).astype(np.float64)
    ttab = (-c32.astype(np.float64)[:, None, None] * dist[None]).astype(np.float32)
    return jnp.asarray(qconst), jnp.asarray(kfeat, BF16), jnp.asarray(c32), jnp.asarray(ttab)


_WINDOW_BLOCKS = (5, 11)
_UNDERFLOW_MARGIN = 160.0


def _diff_kernel(c_ref, misc_ref, q_ref, k_ref, v_ref, qconst_ref, kfeat_ref, ttab_ref, dl_ref, subg_ref, o_ref,
                 kaug_scr, kmax_scr, *, tq, tk, nk):
    h = pl.program_id(1)
    i = pl.program_id(2)
    c = c_ref[h]
    lam_init = misc_ref[0]
    lane = lax.broadcasted_iota(jnp.int32, (1, LANES), 1)
    lo_lane = lane < HALF

    @pl.when(i == 0)
    def _():
        def aug(cc, kmax2):
            sl = pl.ds(pl.multiple_of(cc * tk, tk), tk)
            kb = k_ref[sl, :]
            kaug_scr[0, sl, :] = jnp.where(lo_lane, kb, kfeat_ref[0])
            kaug_scr[1, sl, :] = jnp.where(lo_lane, kfeat_ref[1], kb)
            kf = kb.astype(F32)
            norm2 = jnp.sum(kf * kf, axis=-1, keepdims=True)
            return jnp.maximum(kmax2, jnp.max(norm2, axis=0, keepdims=True))
        kmax2 = lax.fori_loop(0, nk, aug, jnp.zeros((1, 1), F32))
        kmax_scr[...] = jnp.broadcast_to(kmax2, kmax_scr.shape)

    q = q_ref[...].astype(F32)
    ci = c * lax.broadcasted_iota(jnp.int32, (tq, 1), 0).astype(F32)
    r1 = ci.astype(BF16).astype(F32)
    r2 = (ci - r1).astype(BF16).astype(F32)
    r3 = (ci - r1 - r2).astype(BF16).astype(F32)
    q_plain, q_right, q_left = [], [], []
    for br, base in ((0, HALF), (1, 0)):
        keep = lo_lane if br == 0 else jnp.logical_not(lo_lane)
        f0 = base + 2 * _N_BIAS_PIECES
        feat = qconst_ref[0, br:br + 1, :] + jnp.where(
            lane == f0, r1, jnp.where(lane == f0 + 1, r2, jnp.where(lane == f0 + 2, r3, 0.0)))
        q_plain.append(jnp.where(keep, q, 0.0).astype(BF16))
        q_right.append(jnp.where(keep, q, feat).astype(BF16))
        q_left.append(jnp.where(keep, q, -feat).astype(BF16))

    def update(st, s, shift, vb):
        m_old, l_old, acc_old = st
        m_new = jnp.maximum(m_old, jnp.max(s, axis=-1, keepdims=True) + shift)
        alpha = jnp.exp2(m_old - m_new)
        p = jnp.exp2(s + (shift - m_new))
        l_new = alpha * l_old + jnp.sum(p, axis=-1, keepdims=True)
        acc_new = alpha * acc_old + jnp.dot(p.astype(BF16), vb, preferred_element_type=F32)
        return m_new, l_new, acc_new

    cdiag = (i * tq) // tk

    def side_block(rows, on_right, base, st):
        ss = [lax.dot_general(jnp.where(on_right, q_right[idx], q_left[idx]), kaug_scr[idx, rows, :], _NT,
                              preferred_element_type=F32) for idx in range(2)]
        shift = jnp.where(on_right, -1.0, 1.0) * c * (base - i * tq).astype(F32)
        vb = v_ref[rows, :]
        return [update(st[idx], ss[idx], shift, vb) for idx in range(2)]

    def chunk_rows(cc):
        return pl.ds(pl.multiple_of(cc * tk, tk), tk)

    def square_block():
        st = [(jnp.full((tq, 1), NEG_INF, F32), jnp.zeros((tq, 1), F32), jnp.zeros((tq, LANES), F32))
              for _ in range(2)]
        sq = pl.ds(pl.multiple_of(i * tq, tq), tq)
        ss = [lax.dot_general(q_plain[idx], kaug_scr[idx, sq, :], _NT, preferred_element_type=F32)
              + ttab_ref[0] for idx in range(2)]
        vb = v_ref[sq, :]
        return [update(st[idx], ss[idx], 0.0, vb) for idx in range(2)]

    def diag_chunk():
        st = square_block()
        n_sub = tk // tq
        usq = i % n_sub
        dbase = cdiag * tk
        for w in range(1, n_sub):
            uo = usq + w
            on_right = uo < n_sub
            uo = jnp.where(on_right, uo, uo - n_sub)
            rows = pl.ds(pl.multiple_of(dbase + uo * tq, tq), tq)
            st = side_block(rows, on_right, dbase, st)
        return st

    def finish(st):
        hs = pl.ds(h, 1)
        lam = (jnp.exp(jnp.sum(dl_ref[0, hs, :] * dl_ref[1, hs, :], axis=-1, keepdims=True))
               - jnp.exp(jnp.sum(dl_ref[2, hs, :] * dl_ref[3, hs, :], axis=-1, keepdims=True)) + lam_init)
        o = st[0][2] / st[0][1] - lam * (st[1][2] / st[1][1])
        o = o * lax.rsqrt(jnp.mean(o * o, axis=-1, keepdims=True) + RMS_EPS) * subg_ref[...]
        o_ref[...] = (o * (1.0 - lam_init)).astype(BF16)

    def all_chunks():
        st = diag_chunk()
        for step in range(1, nk):
            ct = cdiag + step
            on_right = ct < nk
            cc = jnp.where(on_right, ct, ct - nk)
            st = side_block(chunk_rows(cc), on_right, cc * tk, st)
        finish(st)

    n_blocks = nk * (tk // tq)
    windows = tuple(w for w in _WINDOW_BLOCKS if w < n_blocks)
    if not windows:
        all_chunks()
        return

    qmax2 = jnp.max(jnp.sum(q * q, axis=-1, keepdims=True), axis=0, keepdims=True)
    need = (2.0 * jnp.sqrt(qmax2 * kmax_scr[0:1, 0:1]))[0, 0] + _UNDERFLOW_MARGIN
    far = 1e9
    n_sub = tk // tq
    taken = False
    for wb in windows:
        start = jnp.clip(i - wb // 2, 0, n_blocks - wb)
        d_left = jnp.where(start > 0, ((i - start) * tq).astype(F32), far)
        d_right = jnp.where(start + wb < n_blocks, ((start + wb - i - 1) * tq).astype(F32), far)
        fits = c * jnp.minimum(d_left, d_right) >= need
        use = fits if taken is False else jnp.logical_and(fits, jnp.logical_not(taken))

        @pl.when(use)
        def _(wb=wb, start=start):
            st = square_block()
            for w in range(wb - 1):
                bw = start + w
                blk = jnp.where(bw >= i, bw + 1, bw)
                rows = pl.ds(pl.multiple_of(blk * tq, tq), tq)
                st = side_block(rows, blk > i, (blk // n_sub) * tk, st)
            finish(st)

        taken = fits if taken is False else jnp.logical_or(taken, fits)

    @pl.when(jnp.logical_not(taken))
    def _():
        all_chunks()


def _diff_attention(z, dl, subg, tables, misc, *, batch, seq, lay, tq, tk):
    t = z.shape[0]
    nq = seq // tq
    qconst, kfeat, cvals, ttab = tables
    kern = functools.partial(_diff_kernel, tq=tq, tk=tk, nk=seq // tk)
    return pl.pallas_call(
        kern,
        out_shape=jax.ShapeDtypeStruct((t, B_HEADS * LANES), BF16),
        grid=(batch, B_HEADS, nq),
        in_specs=[pl.BlockSpec(memory_space=pltpu.SMEM),
                  pl.BlockSpec(memory_space=pltpu.SMEM),
                  pl.BlockSpec((tq, LANES), lambda b, h, i: (b * nq + i, lay.QB0 + h)),
                  pl.BlockSpec((seq, LANES), lambda b, h, i: (b, lay.KB0 + h)),
                  pl.BlockSpec((seq, LANES), lambda b, h, i: (b, lay.VB0 + h)),
                  pl.BlockSpec((1, 2, LANES), lambda b, h, i: (h, 0, 0)),
                  pl.BlockSpec(kfeat.shape, lambda b, h, i: (0, 0, 0)),
                  pl.BlockSpec((1, tq, tq), lambda b, h, i: (h, 0, 0)),
                  pl.BlockSpec(dl.shape, lambda b, h, i: (0, 0, 0)),
                  pl.BlockSpec((1, LANES), lambda b, h, i: (0, 0))],
        out_specs=pl.BlockSpec((tq, LANES), lambda b, h, i: (b * nq + i, h)),
        scratch_shapes=[pltpu.VMEM((2, seq, LANES), BF16), pltpu.VMEM((8, LANES), F32)],
        compiler_params=_params("parallel", "arbitrary", "arbitrary"),
        name="diff_attention",
    )(cvals, misc, z, z, z, qconst, kfeat, ttab, dl, subg)


def _merge_kernel(oa_ref, ob_ref, oc_ref, od_ref, gate_ref, wa_ref, wb_ref, wc_ref, wd_ref, o_ref, *, d):
    acc = None
    for idx, (o, w) in enumerate(((oa_ref, wa_ref), (ob_ref, wb_ref), (oc_ref, wc_ref), (od_ref, wd_ref))):
        y = jnp.dot(o[...], w[...], preferred_element_type=F32)
        g = _sigmoid(gate_ref[:, idx * d:(idx + 1) * d].astype(F32))
        acc = g * y if acc is None else acc + g * y
    o_ref[...] = acc.astype(BF16)


def _merge(oa, ob, oc, od, z, wbs, *, d, tm=256):
    t = oa.shape[0]
    tm = min(tm, t)
    row = lambda a: pl.BlockSpec((tm, a.shape[1]), lambda i: (i, 0))
    full = lambda a: pl.BlockSpec(a.shape, lambda i: (0, 0))
    return pl.pallas_call(
        functools.partial(_merge_kernel, d=d),
        out_shape=jax.ShapeDtypeStruct((t, d), BF16),
        grid=(t // tm,),
        in_specs=[row(oa), row(ob), row(oc), row(od),
                  pl.BlockSpec((tm, N_BRANCH * d), lambda i: (i, 0))] + [full(w) for w in wbs],
        out_specs=pl.BlockSpec((tm, d), lambda i: (i, 0)),
        compiler_params=_params("parallel"),
        name="branch_merge",
    )(oa, ob, oc, od, z, *wbs)


def _outproj_kernel(a_ref, w_ref, x_ref, g_ref, o_ref):
    y = jnp.dot(a_ref[...], w_ref[0], preferred_element_type=F32)
    o_ref[...] = x_ref[...] + g_ref[0] * y


def _outproj(a, w, layer, x2, gate, seq, tm=512):
    t, d = x2.shape
    tm = min(tm, seq)
    per_batch = seq // tm
    return pl.pallas_call(
        _outproj_kernel,
        out_shape=jax.ShapeDtypeStruct((t, d), F32),
        grid=(t // tm,),
        in_specs=[pl.BlockSpec((tm, a.shape[1]), lambda i: (i, 0)),
                  pl.BlockSpec((1,) + w.shape[1:], lambda i: (layer, 0, 0)),
                  pl.BlockSpec((tm, d), lambda i: (i, 0)),
                  pl.BlockSpec((1, 1, d), lambda i: (i // per_batch, 0, 0))],
        out_specs=pl.BlockSpec((tm, d), lambda i: (i, 0)),
        compiler_params=_params("parallel"),
        name="out_proj",
    )(a, w, x2, gate)


def _router_kernel(x_ref, mod_ref, g_ref, rwt_ref, rb_ref, h_ref, e_ref, gw_ref):
    h = _mod_norm(x_ref[...], g_ref[...], mod_ref[0, 3:4, :], mod_ref[0, 4:5, :])
    h_ref[...] = h
    logits = lax.dot_general(rwt_ref[...], h.astype(BF16), _NT, preferred_element_type=F32)
    score = _sigmoid(logits)
    sel = score + rb_ref[...]
    sel_r = [sel[e:e + 1, :] for e in range(N_EXPERTS)]
    sc_r = [score[e:e + 1, :] for e in range(N_EXPERTS)]
    n = EXPERTS_PER_GROUP

    def top2_sum(v):
        best = None
        for a in range(n):
            for b in range(a + 1, n):
                s = v[a] + v[b]
                best = s if best is None else jnp.maximum(best, s)
        return best

    gbest = top2_sum(sel_r[0:n])
    gidx = jnp.zeros(gbest.shape, jnp.int32)
    cv = list(sel_r[0:n])
    cs = list(sc_r[0:n])
    for g in range(1, N_GROUPS):
        gs = top2_sum(sel_r[g * n:(g + 1) * n])
        upd = gs > gbest
        gbest = jnp.where(upd, gs, gbest)
        gidx = jnp.where(upd, g, gidx)
        for j in range(n):
            cv[j] = jnp.where(upd, sel_r[g * n + j], cv[j])
            cs[j] = jnp.where(upd, sc_r[g * n + j], cs[j])
    b1, i1, s1 = cv[0], jnp.zeros(gbest.shape, jnp.int32), cs[0]
    for j in range(1, n):
        upd = cv[j] > b1
        b1 = jnp.where(upd, cv[j], b1)
        i1 = jnp.where(upd, j, i1)
        s1 = jnp.where(upd, cs[j], s1)
    low = -3.0e38
    b2 = jnp.full(gbest.shape, low, F32)
    i2 = jnp.zeros(gbest.shape, jnp.int32)
    s2 = jnp.zeros(gbest.shape, F32)
    for j in range(n):
        cand = jnp.where(i1 == j, low, cv[j])
        upd = cand > b2
        b2 = jnp.where(upd, cand, b2)
        i2 = jnp.where(upd, j, i2)
        s2 = jnp.where(upd, cs[j], s2)
    tot = s1 + s2
    e_ref[0:1, :] = gidx * n + i1
    e_ref[1:2, :] = gidx * n + i2
    gw_ref[0:1, :] = s1 / tot
    gw_ref[1:2, :] = s2 / tot


def _router(x2, mod, g, rwt, rb, seq, tm=512):
    t, d = x2.shape
    tm = min(tm, seq)
    per_batch = seq // tm
    return pl.pallas_call(
        _router_kernel,
        out_shape=(jax.ShapeDtypeStruct((t, d), F32),
                   jax.ShapeDtypeStruct((2, t), jnp.int32),
                   jax.ShapeDtypeStruct((2, t), F32)),
        grid=(t // tm,),
        in_specs=[pl.BlockSpec((tm, d), lambda i: (i, 0)),
                  pl.BlockSpec((1, 6, d), lambda i: (i // per_batch, 0, 0)),
                  pl.BlockSpec((1, d), lambda i: (0, 0)),
                  pl.BlockSpec(rwt.shape, lambda i: (0, 0)),
                  pl.BlockSpec(rb.shape, lambda i: (0, 0))],
        out_specs=(pl.BlockSpec((tm, d), lambda i: (i, 0)),
                   pl.BlockSpec((2, tm), lambda i: (0, i)),
                   pl.BlockSpec((2, tm), lambda i: (0, i))),
        compiler_params=_params("parallel"),
        name="norm_router",
    )(x2, mod, g, rwt, rb)


def _rank_kernel(e_ref, rank_ref, cnt_ref, carry):
    @pl.when(pl.program_id(0) == 0)
    def _():
        carry[...] = jnp.zeros(carry.shape, F32)

    tm = e_ref.shape[1]
    eio = lax.broadcasted_iota(jnp.int32, (N_EXPERTS, tm), 0)
    oh0 = eio == e_ref[0:1, :]
    oh1 = eio == e_ref[1:2, :]
    c = jnp.where(jnp.logical_or(oh0, oh1), 1.0, 0.0)
    before = (lax.broadcasted_iota(jnp.int32, (tm, tm), 0) < lax.broadcasted_iota(jnp.int32, (tm, tm), 1))
    pref = jnp.dot(c.astype(BF16), jnp.where(before, 1.0, 0.0).astype(BF16),
                   preferred_element_type=F32) + carry[...]
    rank_ref[0:1, :] = jnp.sum(jnp.where(oh0, pref, 0.0), axis=0, keepdims=True).astype(jnp.int32)
    rank_ref[1:2, :] = jnp.sum(jnp.where(oh1, pref, 0.0), axis=0, keepdims=True).astype(jnp.int32)
    total = carry[...] + jnp.sum(c, axis=1, keepdims=True)
    carry[...] = total
    cnt_ref[...] = jnp.broadcast_to(total, cnt_ref.shape)


def _rank(eidx, tm=512):
    t = eidx.shape[1]
    tm = min(tm, t)
    return pl.pallas_call(
        _rank_kernel,
        out_shape=(jax.ShapeDtypeStruct((2, t), jnp.int32),
                   jax.ShapeDtypeStruct((N_EXPERTS, LANES), F32)),
        grid=(t // tm,),
        in_specs=[pl.BlockSpec((2, tm), lambda i: (0, i))],
        out_specs=(pl.BlockSpec((2, tm), lambda i: (0, i)),
                   pl.BlockSpec((N_EXPERTS, LANES), lambda i: (0, 0))),
        scratch_shapes=[pltpu.VMEM((N_EXPERTS, 1), F32)],
        compiler_params=_params("arbitrary"),
        name="expert_rank",
    )(eidx)


_DMA_UNROLL = 8


def _row_copy(src_hbm, row, dst, slot, r, sem):
    return pltpu.make_async_copy(src_hbm.at[pl.ds(row, 1), :], dst.at[slot, pl.ds(r, 1), :], sem.at[slot])


def _expert_kernel(te_ref, nu_ref, src_ref, h_hbm, w1_ref, w3_ref, w2_ref, y_ref, xbuf, sem, *, tm):
    j = pl.program_id(0)
    n_used = nu_ref[0]

    def wait_tile(slot):
        pltpu.make_async_copy(h_hbm.at[pl.ds(0, tm), :], xbuf.at[slot], sem.at[slot]).wait()

    @pl.when(jnp.logical_and(j == 0, n_used > 0))
    def _():
        def issue(r, carry):
            _row_copy(h_hbm, src_ref[r], xbuf, 0, r, sem).start()
            return carry
        lax.fori_loop(0, tm, issue, 0, unroll=_DMA_UNROLL)

    @pl.when(j < n_used)
    def _():
        slot = j % 2
        wait_tile(slot)
        nxt = jnp.minimum(j + 1, n_used - 1) * tm
        for r in range(tm):
            _row_copy(h_hbm, src_ref[nxt + r], xbuf, 1 - slot, r, sem).start()
        xs = xbuf[slot].astype(BF16)
        g = jnp.dot(xs, w1_ref[0], preferred_element_type=F32)
        u = jnp.dot(xs, w3_ref[0], preferred_element_type=F32)
        mid = (g * _sigmoid(g) * u).astype(BF16)
        y_ref[...] = jnp.dot(mid, w2_ref[0], preferred_element_type=F32)

    @pl.when(j == n_used - 1)
    def _():
        wait_tile(1 - j % 2)

    @pl.when(j >= n_used)
    def _():
        y_ref[...] = jnp.zeros(y_ref.shape, F32)


def _experts(tile_e, n_used, src, h, w1, w3, w2, layer, tm):
    p = src.shape[0]
    d = h.shape[1]
    f = w1.shape[-1]
    n_exp = w1.shape[1]
    w1r = w1.reshape((-1,) + w1.shape[2:])
    w3r = w3.reshape((-1,) + w3.shape[2:])
    w2r = w2.reshape((-1,) + w2.shape[2:])
    wmap = lambda j, te, nu, src: (layer * n_exp + te[j], 0, 0)
    return pl.pallas_call(
        functools.partial(_expert_kernel, tm=tm),
        out_shape=jax.ShapeDtypeStruct((p, d), F32),
        grid_spec=pltpu.PrefetchScalarGridSpec(
            num_scalar_prefetch=3,
            grid=(p // tm,),
            in_specs=[pl.BlockSpec(memory_space=pl.ANY),
                      pl.BlockSpec((1, d, f), wmap),
                      pl.BlockSpec((1, d, f), wmap),
                      pl.BlockSpec((1, f, d), wmap)],
            out_specs=pl.BlockSpec((tm, d), lambda j, te, nu, src: (j, 0)),
            scratch_shapes=[pltpu.VMEM((2, tm, d), F32), pltpu.SemaphoreType.DMA((2,))]),
        compiler_params=_params("arbitrary"),
        name="experts",
    )(tile_e, n_used, src, h, w1r, w3r, w2r)


def _combine_kernel(pos_ref, x_ref, g_ref, gw_ref, y_hbm, o_ref, buf0, buf1, sem0, sem1, *, tm, t, n_steps):
    i = pl.program_id(0)

    def start_step(step, slot):
        def issue(r, carry):
            _row_copy(y_hbm, pos_ref[step * tm + r], buf0, slot, r, sem0).start()
            _row_copy(y_hbm, pos_ref[t + step * tm + r], buf1, slot, r, sem1).start()
            return carry
        lax.fori_loop(0, tm, issue, 0, unroll=_DMA_UNROLL)

    def wait_step(slot):
        pltpu.make_async_copy(y_hbm.at[pl.ds(0, tm), :], buf0.at[slot], sem0.at[slot]).wait()
        pltpu.make_async_copy(y_hbm.at[pl.ds(0, tm), :], buf1.at[slot], sem1.at[slot]).wait()

    @pl.when(i == 0)
    def _():
        start_step(0, 0)

    @pl.when(i + 1 < n_steps)
    def _():
        start_step(i + 1, (i + 1) % 2)

    slot = i % 2
    wait_step(slot)
    moe = gw_ref[:, 0:1] * buf0[slot] + gw_ref[:, 1:2] * buf1[slot]
    o_ref[...] = x_ref[...] + g_ref[0] * moe


def _combine(pos, x2, gate, gwt, y, seq, tm=256):
    t, d = x2.shape
    tm = min(tm, seq)
    per_batch = seq // tm
    n_steps = t // tm
    return pl.pallas_call(
        functools.partial(_combine_kernel, tm=tm, t=t, n_steps=n_steps),
        out_shape=jax.ShapeDtypeStruct((t, d), F32),
        grid_spec=pltpu.PrefetchScalarGridSpec(
            num_scalar_prefetch=1,
            grid=(n_steps,),
            in_specs=[pl.BlockSpec((tm, d), lambda i, pos: (i, 0)),
                      pl.BlockSpec((1, 1, d), lambda i, pos: (i // per_batch, 0, 0)),
                      pl.BlockSpec((tm, 2), lambda i, pos: (i, 0)),
                      pl.BlockSpec(memory_space=pl.ANY)],
            out_specs=pl.BlockSpec((tm, d), lambda i, pos: (i, 0)),
            scratch_shapes=[pltpu.VMEM((2, tm, d), F32), pltpu.VMEM((2, tm, d), F32),
                            pltpu.SemaphoreType.DMA((2,)), pltpu.SemaphoreType.DMA((2,))]),
        compiler_params=_params("arbitrary"),
        name="moe_combine",
    )(pos, x2, gate, gwt, y)


def _moe(x2, mod, g2n, rwt, rb, w1, w3, w2, layer, *, batch, seq, tm_e=256):
    t, d = x2.shape
    h, eidx, gw = _router(x2, mod, g2n, rwt, rb, seq)
    rank, cnt = _rank(eidx)
    tm_e = min(tm_e, t)
    counts = cnt[:, 0].astype(jnp.int32)
    gsz = ((counts + tm_e - 1) // tm_e) * tm_e
    ends = jnp.cumsum(gsz)
    offs = ends - gsz
    pos = rank
    for e in range(N_EXPERTS):
        pos = pos + jnp.where(eidx == e, offs[e], 0)
    n_tiles = (2 * t) // tm_e + N_EXPERTS
    p = n_tiles * tm_e
    flat = pos.reshape(-1)
    tok = jnp.broadcast_to(jnp.arange(t, dtype=jnp.int32)[None, :], (2, t)).reshape(-1)
    src = jnp.zeros((p,), jnp.int32).at[flat].set(tok)
    tile_start = jnp.arange(n_tiles, dtype=jnp.int32) * tm_e
    tile_e = jnp.minimum(jnp.sum(tile_start[:, None] >= ends[None, :], axis=1),
                         N_EXPERTS - 1).astype(jnp.int32)
    n_used = (ends[-1:] // tm_e).astype(jnp.int32)
    y = _experts(tile_e, n_used, src, h, w1, w3, w2, layer, tm_e)
    return _combine(flat, x2, mod[:, 5:6, :], gw.T, y, seq)


def kernel(x, c, ada_w, ada_b, norm1_g, norm2_g, w_in, qk_norm_a, qk_norm_b, qk_norm_c, qk_norm_d,
           diff_lambda, diff_subln_g, sink_logits, rpb, w_branch, w_out, router_w, router_b, w1, w3, w2):
    batch, seq, d = x.shape
    depth = ada_w.shape[0]
    t = batch * seq
    lay = _Layout(d)
    bw = lay.bw

    mod_all = _ada_mod(c, ada_w, ada_b)
    wz = _build_wz(w_in, lay)
    coef = jnp.stack([_build_coef(lay, qk_norm_a[l], qk_norm_b[l], qk_norm_c[l], qk_norm_d[l])
                      for l in range(depth)])
    wb16 = w_branch.astype(BF16)
    c_hd = bw // C_HEADS
    grp = C_HEADS // C_KV_HEADS
    wc_rows = []
    for hq in range(C_HEADS):
        blk = jnp.zeros((depth, LANES, d), BF16)
        off = (hq // grp) * c_hd
        wc_rows.append(blk.at[:, off:off + c_hd].set(wb16[:, 2, hq * c_hd:(hq + 1) * c_hd]))
    wc_ext = jnp.concatenate(wc_rows, axis=1)
    wout16 = w_out.astype(BF16)
    w1b, w3b, w2b = w1.astype(BF16), w3.astype(BF16), w2.astype(BF16)
    rwt = router_w.T.astype(BF16)
    rb = router_b.reshape(N_EXPERTS, 1).astype(F32)

    tab_a, side_a = _dilated_table(256)
    tab_c, side_c = _window_table(256)
    ones_mask = jnp.ones((A_HEADS, 1, LANES), BF16)
    lane = np.arange(LANES)
    cmask = np.stack([(lane // HALF) == (hq % 2) for hq in range(C_HEADS)]).astype(np.float32)
    cmask = jnp.asarray(cmask.reshape(C_HEADS, 1, LANES), BF16)
    no_sink = jnp.full((A_HEADS,), NEG_INF, F32)
    tq_b, tk_b = min(512, seq), min(1024, seq)
    tables_b = _diff_feature_tables(tq_b, tk_b)
    tab_d = _nbr_tables(rpb, seq // GRID_W)

    x2 = x.reshape(t, d)
    for l in range(depth):
        mod = mod_all[l, :batch].reshape(batch, 6, d)
        z = _inproj(x2, mod, norm1_g[l].reshape(1, d), wz, coef, l, lay, seq)
        o_a = _band_attention(z, tab_a, ones_mask, no_sink, batch=batch, seq=seq, n_groups=A_HEADS, hp=1,
                              q_width=LANES, q_of=lambda h: 0,
                              qblk=lambda g: lay.QA0 + g, kblk=lambda g: lay.KA0 + g,
                              vblk=lambda g: lay.VA0 + g, n_side=side_a)
        lam_init = 0.8 - 0.6 * float(np.exp(-0.3 * l))
        o_b = _diff_attention(z, diff_lambda[l], diff_subln_g[l].reshape(1, LANES), tables_b,
                              jnp.full((1,), lam_init, F32), batch=batch, seq=seq, lay=lay,
                              tq=tq_b, tk=tk_b)
        o_c = _band_attention(z, tab_c, cmask, sink_logits[l].astype(F32) * _LOG2E, batch=batch, seq=seq,
                              n_groups=C_KV_HEADS, hp=grp, q_width=(grp // 2) * LANES,
                              q_of=lambda h: h // 2,
                              qblk=lambda g: lay.QC0 // (grp // 2) + g, kblk=lambda g: lay.KC0 + g,
                              vblk=lambda g: lay.VC0, n_side=side_c)
        o_d = _nbr_attention(z, tab_d, l, batch=batch, seq=seq, lay=lay)
        merged = _merge(o_a, o_b, o_c, o_d, z, (wb16[l, 0], wb16[l, 1], wc_ext[l], wb16[l, 3]), d=d)
        x2 = _outproj(merged, wout16, l, x2, mod[:, 2:3, :], seq)
        x2 = _moe(x2, mod, norm2_g[l].reshape(1, d), rwt, rb, w1b, w3b, w2b, l, batch=batch, seq=seq)
    return x2.reshape(batch, seq, d)
```

```python
import functools

import numpy as np
import jax
import jax.numpy as jnp
from jax import lax
from jax.experimental import pallas as pl
from jax.experimental.pallas import tpu as pltpu

F32 = jnp.float32
BF16 = jnp.bfloat16
LANES = 128
HALF = LANES // 2
RMS_EPS = 1e-6
NEG_INF = -1e30
_LOG2E = float(np.log2(np.e))
VMEM_LIMIT = 56 * 1024 * 1024

GRID_W = 64
N_BRANCH = 4
A_HEADS = 4
A_PATTERNS = ((128, 1), (512, 4), (2048, 16))
B_HEADS = 4
C_HEADS = 8
C_KV_HEADS = 2
C_RADIUS = 128
D_HEADS = 4
D_WIN_R = 8
D_WIN_C = 16
N_EXPERTS = 16
N_GROUPS = 4
EXPERTS_PER_GROUP = N_EXPERTS // N_GROUPS

_NT = (((1,), (1,)), ((), ()))


def _params(*sem):
    return pltpu.CompilerParams(dimension_semantics=sem, vmem_limit_bytes=VMEM_LIMIT)


def _sigmoid(v):
    return 1.0 / (1.0 + jnp.exp(-v))


class _Layout:
    def __init__(self, d_model):
        bw = d_model // N_BRANCH
        self.bw = bw
        hb = bw // LANES
        o = {}
        o['qa'], o['ka'], o['va'] = 0, bw, 2 * bw
        o['qb'], o['kb'], o['vb'] = 3 * bw, 4 * bw, 5 * bw
        o['qc'] = 6 * bw
        o['kc'] = 7 * bw
        o['vc'] = 7 * bw + LANES
        o['qd'], o['kd'], o['vd'] = 7 * bw + 2 * LANES, 8 * bw + 2 * LANES, 9 * bw + 2 * LANES
        o['gates'] = 10 * bw + 2 * LANES
        self.orig = o
        gate_blocks = N_BRANCH * d_model // LANES
        self.GATE0 = 0
        b = gate_blocks
        self.QA0, self.KA0, self.VA0 = b, b + hb, b + 2 * hb
        b += 3 * hb
        self.QD0, self.KD0, self.VD0 = b, b + hb, b + 2 * hb
        b += 3 * hb
        self.QB0, self.KB0, self.VB0 = b, b + hb, b + 2 * hb
        b += 3 * hb
        self.QC0 = b
        self.KC0 = b + hb
        self.VC0 = b + hb + 2
        b += hb + 3
        self.n_real = b
        self.chunks_per_tile = 12
        self.n_blocks = -(-b // self.chunks_per_tile) * self.chunks_per_tile
        self.nz = self.n_blocks * LANES
        self.first_norm_tile = gate_blocks // self.chunks_per_tile
        assert self.QC0 % (C_HEADS // C_KV_HEADS // 2) == 0
        norm = [0] * self.n_blocks
        for blk0, nblk, group in ((self.QA0, 2 * hb, bw // A_HEADS), (self.QD0, 2 * hb, bw // D_HEADS),
                                  (self.QB0, 2 * hb, bw // (2 * B_HEADS)),
                                  (self.QC0, hb + 2, bw // C_HEADS)):
            assert group in (LANES, HALF)
            norm[blk0:blk0 + nblk] = [group] * nblk
        self.norm_group = tuple(norm)

    def weight_segments(self):
        o, bw = self.orig, self.bw
        kc0, kc1 = o['kc'], o['kc'] + HALF
        segs = [(o['gates'], N_BRANCH * N_BRANCH * bw),
                (o['qa'], bw), (o['ka'], bw), (o['va'], bw),
                (o['qd'], bw), (o['kd'], bw), (o['vd'], bw),
                (o['qb'], bw), (o['kb'], bw), (o['vb'], bw),
                (o['qc'], bw),
                (kc0, HALF), (kc0, HALF), (kc1, HALF), (kc1, HALF),
                (o['vc'], LANES)]
        pad = self.nz - self.n_real * LANES
        if pad:
            segs.append((None, pad))
        return segs


def _build_wz(w_in, lay):
    parts = []
    for start, width in lay.weight_segments():
        if start is None:
            parts.append(jnp.zeros(w_in.shape[:-1] + (width,), w_in.dtype))
        else:
            parts.append(w_in[..., start:start + width])
    return jnp.concatenate(parts, axis=-1).astype(BF16)


def _build_coef(lay, qk_a, qk_b, qk_c, qk_d):
    hb = lay.bw // LANES
    a_hd = lay.bw // A_HEADS
    b_dk = lay.bw // (2 * B_HEADS)
    c_hd = lay.bw // C_HEADS
    d_hd = lay.bw // D_HEADS
    gain = jnp.ones((lay.nz,), F32)

    def put(gain, blk0, nblk, g, scale):
        lo, hi = blk0 * LANES, (blk0 + nblk) * LANES
        reps = (hi - lo) // g.shape[0]
        return gain.at[lo:hi].set(jnp.tile(g.astype(F32), reps) * scale)

    gain = put(gain, lay.QA0, hb, qk_a[0], a_hd ** -0.5 * _LOG2E)
    gain = put(gain, lay.KA0, hb, qk_a[1], 1.0)
    gain = put(gain, lay.QD0, hb, qk_d[0], d_hd ** -0.5 * _LOG2E)
    gain = put(gain, lay.KD0, hb, qk_d[1], 1.0)
    gain = put(gain, lay.QB0, hb, qk_b[0], b_dk ** -0.5 * _LOG2E)
    gain = put(gain, lay.KB0, hb, qk_b[1], 1.0)
    gain = put(gain, lay.QC0, hb, qk_c[0], c_hd ** -0.5 * _LOG2E)
    gain = put(gain, lay.KC0, 2, qk_c[1], 1.0)
    return gain.reshape(1, lay.nz)


def _ada_kernel(c_ref, w_ref, b_ref, o_ref):
    c = c_ref[...]
    cond = (c * _sigmoid(c)).astype(BF16)
    o_ref[0] = jnp.dot(cond, w_ref[0].astype(BF16), preferred_element_type=F32) + b_ref[0]


def _ada_mod(c, ada_w, ada_b, tn=1024):
    depth, d, n = ada_w.shape
    rows = 8
    cp = jnp.zeros((rows, d), F32).at[:c.shape[0]].set(c)
    tn = min(tn, n)
    return pl.pallas_call(
        _ada_kernel,
        out_shape=jax.ShapeDtypeStruct((depth, rows, n), F32),
        grid=(depth, n // tn),
        in_specs=[pl.BlockSpec((rows, d), lambda l, j: (0, 0)),
                  pl.BlockSpec((1, d, tn), lambda l, j: (l, 0, j)),
                  pl.BlockSpec((1, 1, tn), lambda l, j: (l, 0, j))],
        out_specs=pl.BlockSpec((1, rows, tn), lambda l, j: (l, 0, j)),
        compiler_params=_params("parallel", "parallel"),
        name="ada_mod",
    )(cp, ada_w, ada_b.reshape(depth, 1, n))


def _mod_norm(x, g, shift, scale):
    y = x * lax.rsqrt(jnp.mean(x * x, axis=-1, keepdims=True) + RMS_EPS) * g
    return y * (1.0 + scale) + shift


def _inproj_kernel(x_ref, mod_ref, g_ref, w_ref, gain_ref, z_ref, h_scr, *, first_norm_tile, n_chunks,
                   norm_group):
    j = pl.program_id(1)

    @pl.when(j == 0)
    def _():
        h = _mod_norm(x_ref[...], g_ref[...], mod_ref[0, 0:1, :], mod_ref[0, 1:2, :])
        h_scr[...] = h.astype(BF16)

    acc = jnp.dot(h_scr[...], w_ref[0], preferred_element_type=F32)

    @pl.when(j < first_norm_tile)
    def _():
        z_ref[...] = acc.astype(BF16)

    for tile in range(first_norm_tile, len(norm_group) // n_chunks):
        @pl.when(j == tile)
        def _(tile=tile):
            lo_lane = lax.broadcasted_iota(jnp.int32, (1, LANES), 1) < HALF
            for c in range(n_chunks):
                sl = slice(c * LANES, (c + 1) * LANES)
                group = norm_group[tile * n_chunks + c]
                y = acc[:, sl]
                if group == 0:
                    z_ref[:, sl] = y.astype(BF16)
                    continue
                y2 = y * y
                if group == LANES:
                    ss = jnp.sum(y2, axis=-1, keepdims=True)
                else:
                    lo = jnp.sum(jnp.where(lo_lane, y2, 0.0), axis=-1, keepdims=True)
                    hi = jnp.sum(jnp.where(lo_lane, 0.0, y2), axis=-1, keepdims=True)
                    ss = jnp.where(lo_lane, lo, hi)
                r = lax.rsqrt(ss * (1.0 / group) + RMS_EPS) * gain_ref[0, :, sl]
                z_ref[:, sl] = (y * r).astype(BF16)


def _inproj(x2, mod, g, wz, coef, layer, lay, seq, tm=1024):
    t, d = x2.shape
    tm = min(tm, seq)
    tn = lay.chunks_per_tile * LANES
    per_batch = seq // tm
    kern = functools.partial(_inproj_kernel, first_norm_tile=lay.first_norm_tile,
                             n_chunks=lay.chunks_per_tile, norm_group=lay.norm_group)
    return pl.pallas_call(
        kern,
        out_shape=jax.ShapeDtypeStruct((t, lay.nz), BF16),
        grid=(t // tm, lay.nz // tn),
        in_specs=[pl.BlockSpec((tm, d), lambda i, j: (i, 0)),
                  pl.BlockSpec((1, 6, d), lambda i, j: (i // per_batch, 0, 0)),
                  pl.BlockSpec((1, d), lambda i, j: (0, 0)),
                  pl.BlockSpec((1, d, tn), lambda i, j: (layer, 0, j)),
                  pl.BlockSpec((1, 1, tn), lambda i, j: (layer, 0, j))],
        out_specs=pl.BlockSpec((tm, tn), lambda i, j: (i, j)),
        scratch_shapes=[pltpu.VMEM((tm, d), BF16)],
        compiler_params=_params("parallel", "arbitrary"),
        name="inproj",
    )(x2, mod, g, wz, coef)


def _band_kernel(sink_ref, q_ref, k_ref, v_ref, tab_ref, qmask_ref, o_ref, *, n_side, ck, nkc, hp, q_of,
                 nsub):
    g = pl.program_id(1)
    i = pl.program_id(2)
    for u in range(nsub):
        qi = i * nsub + u
        rows = slice(u * ck, (u + 1) * ck)
        chunks = []
        for d in range(-n_side, n_side + 1):
            kc = qi + d
            valid = jnp.logical_and(kc >= 0, kc < nkc)
            start = pl.multiple_of(jnp.clip(kc, 0, nkc - 1) * ck, ck)
            chunks.append((d + n_side, start, jnp.where(valid, 0.0, NEG_INF)))
        for h in range(hp):
            q = q_ref[rows, q_of(h) * LANES:(q_of(h) + 1) * LANES] * qmask_ref[h]
            sink = sink_ref[g * hp + h]
            m = jnp.full((ck, 1), sink, F32)
            scores = []
            for di, start, penalty in chunks:
                s = lax.dot_general(q, k_ref[pl.ds(start, ck), :], _NT, preferred_element_type=F32)
                s = s + tab_ref[h, di] + penalty
                m = jnp.maximum(m, jnp.max(s, axis=-1, keepdims=True))
                scores.append(s)
            den = jnp.exp2(sink - m)
            acc = jnp.zeros((ck, LANES), F32)
            for s, (di, start, penalty) in zip(scores, chunks):
                p = jnp.exp2(s - m)
                den = den + jnp.sum(p, axis=-1, keepdims=True)
                acc = acc + jnp.dot(p.astype(BF16), v_ref[pl.ds(start, ck), :],
                                    preferred_element_type=F32)
            o_ref[rows, h * LANES:(h + 1) * LANES] = (acc / den).astype(BF16)


def _band_attention(z, table, qmask, sink, *, batch, seq, n_groups, hp, q_width, q_of, qblk, kblk, vblk,
                    n_side, nsub=4):
    t = z.shape[0]
    ck = table.shape[-1]
    nsub = min(nsub, seq // ck)
    tq = nsub * ck
    nq = seq // tq
    kern = functools.partial(_band_kernel, n_side=n_side, ck=ck, nkc=seq // ck, hp=hp, q_of=q_of, nsub=nsub)
    nd = 2 * n_side + 1
    return pl.pallas_call(
        kern,
        out_shape=jax.ShapeDtypeStruct((t, n_groups * hp * LANES), BF16),
        grid=(batch, n_groups, nq),
        in_specs=[pl.BlockSpec(memory_space=pltpu.SMEM),
                  pl.BlockSpec((tq, q_width), lambda b, g, i: (b * nq + i, qblk(g))),
                  pl.BlockSpec((seq, LANES), lambda b, g, i: (b, kblk(g))),
                  pl.BlockSpec((seq, LANES), lambda b, g, i: (b, vblk(g))),
                  pl.BlockSpec((hp, nd, ck, ck), lambda b, g, i: (g, 0, 0, 0)),
                  pl.BlockSpec((hp, 1, LANES), lambda b, g, i: (g, 0, 0))],
        out_specs=pl.BlockSpec((tq, hp * LANES), lambda b, g, i: (b * nq + i, g)),
        compiler_params=_params("parallel", "parallel", "arbitrary"),
        name="band_attention",
    )(sink, z, z, z, table, qmask)


def _alibi_slopes(n):
    return np.array([2.0 ** (-8.0 * (i + 1) / n) for i in range(n)], dtype=np.float32)


def _dilated_table(tq):
    reach = max(w // 2 for w, _ in A_PATTERNS)
    n_side = -(-reach // tq)
    d = np.arange(-n_side, n_side + 1)[:, None, None] * tq
    off = d + np.arange(tq)[None, None, :] - np.arange(tq)[None, :, None]
    mult = np.zeros(off.shape, np.float64)
    for window, dil in A_PATTERNS:
        mult += ((off % dil) == 0) & (np.abs(off) <= window // 2)
    logm = np.where(mult > 0, np.log(np.maximum(mult, 1.0)), 0.0)
    slopes = _alibi_slopes(A_HEADS).astype(np.float64)
    tab = (-slopes[:, None, None, None] * np.abs(off)[None] + logm[None]) * _LOG2E
    tab = np.where(mult[None] > 0, tab, NEG_INF)
    return jnp.asarray(tab.astype(np.float32)), n_side


def _window_table(tq):
    n_side = -(-C_RADIUS // tq)
    d = np.arange(-n_side, n_side + 1)[:, None, None] * tq
    off = d + np.arange(tq)[None, None, :] - np.arange(tq)[None, :, None]
    slopes = _alibi_slopes(C_HEADS).astype(np.float64)
    tab = -slopes[:, None, None, None] * np.abs(off)[None] * _LOG2E
    tab = np.where((np.abs(off) <= C_RADIUS)[None], tab, NEG_INF)
    return jnp.asarray(tab.astype(np.float32)), n_side


_D_QROWS = 8
_D_KROWS = 16


_D_HEADS_PER_STEP = 2


def _nbr_kernel(q_ref, k_ref, v_ref, tab_ref, o_ref, *, rows):
    t = pl.program_id(2)
    krow = jnp.clip(t * _D_QROWS - D_WIN_R // 2, 0, rows - _D_KROWS)
    start = pl.multiple_of(krow * GRID_W, GRID_W * (D_WIN_R // 2))
    nk = _D_KROWS * GRID_W
    for h in range(_D_HEADS_PER_STEP):
        lanes = slice(h * LANES, (h + 1) * LANES)
        s = lax.dot_general(q_ref[:, lanes], k_ref[pl.ds(start, nk), lanes], _NT,
                            preferred_element_type=F32)
        s = s + tab_ref[h, 0]
        m = jnp.max(s, axis=-1, keepdims=True)
        p = jnp.exp2(s - m)
        den = jnp.sum(p, axis=-1, keepdims=True)
        acc = jnp.dot(p.astype(BF16), v_ref[pl.ds(start, nk), lanes], preferred_element_type=F32)
        o_ref[:, lanes] = (acc / den).astype(BF16)


def _nbr_tables(rpb, rows):
    depth, heads, n_dr, n_dc = rpb.shape
    nt = rows // _D_QROWS
    wr = min(D_WIN_R, rows)
    pad_l = GRID_W - 1 - (D_WIN_C - 1)
    vpad = jnp.pad(rpb.astype(F32), ((0, 0), (0, 0), (0, 0), (pad_l, 2 * GRID_W - 1 - n_dc - pad_l)))
    toep = jnp.stack([vpad[..., GRID_W - 1 - c:2 * GRID_W - 1 - c] for c in range(GRID_W)], axis=-2)
    c = np.arange(GRID_W)
    cs = np.clip(c - D_WIN_C // 2, 0, GRID_W - D_WIN_C)
    col_ok = (c[None, :] >= cs[:, None]) & (c[None, :] < cs[:, None] + D_WIN_C)
    colmat = jnp.where(col_ok, toep * _LOG2E, NEG_INF)
    masked = jnp.full((depth, heads, 1, GRID_W, GRID_W), NEG_INF, F32)
    colmat = jnp.concatenate([colmat, masked], axis=2)
    sel = np.full((3, _D_QROWS, _D_KROWS), n_dr, np.int32)
    for kind, t in enumerate((0, 1, nt - 1)):
        krow0 = int(np.clip(t * _D_QROWS - D_WIN_R // 2, 0, rows - _D_KROWS))
        for ri in range(_D_QROWS):
            r = t * _D_QROWS + ri
            rs = int(np.clip(r - wr // 2, 0, rows - wr))
            for kj in range(_D_KROWS):
                kr = krow0 + kj
                if rs <= kr < rs + wr:
                    sel[kind, ri, kj] = kr - r + (D_WIN_R - 1)
    blocks = jnp.take(colmat, jnp.asarray(sel.reshape(-1)), axis=2)
    blocks = blocks.reshape(depth * heads, 3, _D_QROWS, _D_KROWS, GRID_W, GRID_W)
    blocks = blocks.transpose(0, 1, 2, 4, 3, 5)
    return blocks.reshape(depth * heads, 3, _D_QROWS * GRID_W, _D_KROWS * GRID_W)


def _nbr_attention(z, table, layer, *, batch, seq, lay):
    t = z.shape[0]
    rows = seq // GRID_W
    nt = rows // _D_QROWS
    tq = _D_QROWS * GRID_W
    nk = _D_KROWS * GRID_W
    kern = functools.partial(_nbr_kernel, rows=rows)
    hp = _D_HEADS_PER_STEP
    groups = D_HEADS // hp
    width = hp * LANES
    assert lay.QD0 % hp == 0 and lay.KD0 % hp == 0 and lay.VD0 % hp == 0
    kind = lambda i: jnp.where(i == 0, 0, jnp.where(i == nt - 1, 2, 1))
    return pl.pallas_call(
        kern,
        out_shape=jax.ShapeDtypeStruct((t, D_HEADS * LANES), BF16),
        grid=(batch, groups, nt),
        in_specs=[pl.BlockSpec((tq, width), lambda b, g, i: (b * nt + i, lay.QD0 // hp + g)),
                  pl.BlockSpec((seq, width), lambda b, g, i: (b, lay.KD0 // hp + g)),
                  pl.BlockSpec((seq, width), lambda b, g, i: (b, lay.VD0 // hp + g)),
                  pl.BlockSpec((hp, 1, tq, nk), lambda b, g, i: (layer * groups + g, kind(i), 0, 0))],
        out_specs=pl.BlockSpec((tq, width), lambda b, g, i: (b * nt + i, g)),
        compiler_params=_params("parallel", "parallel", "arbitrary"),
        name="nbr_attention",
    )(z, z, z, table)


_N_BIAS_PIECES = 3
_KPOS_RADIX = 64


def _bf16_round(v):
    u = np.asarray(v, np.float32).view(np.uint32)
    u = (u + np.uint32(0x7FFF) + ((u >> np.uint32(16)) & np.uint32(1))) & np.uint32(0xFFFF0000)
    return u.view(np.float32)


def _diff_feature_tables(tq, tk):
    c32 = (_alibi_slopes(B_HEADS).astype(np.float64) * np.log2(np.e)).astype(np.float32)
    qconst = np.zeros((B_HEADS, 2, LANES), np.float32)
    for h in range(B_HEADS):
        rem = np.float32(c32[h])
        pieces = []
        for _ in range(_N_BIAS_PIECES):
            p = np.float32(_bf16_round(rem))
            pieces.append(p)
            rem = np.float32(rem - p)
        for br, base in ((0, HALF), (1, 0)):
            for n, p in enumerate(pieces):
                qconst[h, br, base + n] = -_KPOS_RADIX * p
                qconst[h, br, base + _N_BIAS_PIECES + n] = -p
    j = np.arange(tk)
    kfeat = np.zeros((2, tk, LANES), np.float32)
    for br, base in ((0, HALF), (1, 0)):
        for n in range(_N_BIAS_PIECES):
            kfeat[br, :, base + n] = j // _KPOS_RADIX
            kfeat[br, :, base + _N_BIAS_PIECES + n] = j % _KPOS_RADIX
            kfeat[br, :, base + 2 * _N_BIAS_PIECES + n] = 1.0
    dist = np.abs(np.arange(tq)[None, :] - np.arange(tq)[:, None]).astype(np.float64)
    ttab = (-c32.astype(np.float64)[:, None, None] * dist[None]).astype(np.float32)
    return jnp.asarray(qconst), jnp.asarray(kfeat, BF16), jnp.asarray(c32), jnp.asarray(ttab)


_WINDOW_BLOCKS = (5, 11)
_UNDERFLOW_MARGIN = 160.0


def _diff_kernel(c_ref, misc_ref, q_ref, k_ref, v_ref, qconst_ref, kfeat_ref, ttab_ref, dl_ref, subg_ref, o_ref,
                 kaug_scr, kmax_scr, *, tq, tk, nk):
    h = pl.program_id(1)
    i = pl.program_id(2)
    c = c_ref[h]
    lam_init = misc_ref[0]
    lane = lax.broadcasted_iota(jnp.int32, (1, LANES), 1)
    lo_lane = lane < HALF

    @pl.when(i == 0)
    def _():
        def aug(cc, kmax2):
            sl = pl.ds(pl.multiple_of(cc * tk, tk), tk)
            kb = k_ref[sl, :]
            kaug_scr[0, sl, :] = jnp.where(lo_lane, kb, kfeat_ref[0])
            kaug_scr[1, sl, :] = jnp.where(lo_lane, kfeat_ref[1], kb)
            kf = kb.astype(F32)
            norm2 = jnp.sum(kf * kf, axis=-1, keepdims=True)
            return jnp.maximum(kmax2, jnp.max(norm2, axis=0, keepdims=True))
        kmax2 = lax.fori_loop(0, nk, aug, jnp.zeros((1, 1), F32))
        kmax_scr[...] = jnp.broadcast_to(kmax2, kmax_scr.shape)

    q = q_ref[...].astype(F32)
    ci = c * lax.broadcasted_iota(jnp.int32, (tq, 1), 0).astype(F32)
    r1 = ci.astype(BF16).astype(F32)
    r2 = (ci - r1).astype(BF16).astype(F32)
    r3 = (ci - r1 - r2).astype(BF16).astype(F32)
    q_plain, q_right, q_left = [], [], []
    for br, base in ((0, HALF), (1, 0)):
        keep = lo_lane if br == 0 else jnp.logical_not(lo_lane)
        f0 = base + 2 * _N_BIAS_PIECES
        feat = qconst_ref[0, br:br + 1, :] + jnp.where(
            lane == f0, r1, jnp.where(lane == f0 + 1, r2, jnp.where(lane == f0 + 2, r3, 0.0)))
        q_plain.append(jnp.where(keep, q, 0.0).astype(BF16))
        q_right.append(jnp.where(keep, q, feat).astype(BF16))
        q_left.append(jnp.where(keep, q, -feat).astype(BF16))

    def update(st, s, shift, vb):
        m_old, l_old, acc_old = st
        m_new = jnp.maximum(m_old, jnp.max(s, axis=-1, keepdims=True) + shift)
        alpha = jnp.exp2(m_old - m_new)
        p = jnp.exp2(s + (shift - m_new))
        l_new = alpha * l_old + jnp.sum(p, axis=-1, keepdims=True)
        acc_new = alpha * acc_old + jnp.dot(p.astype(BF16), vb, preferred_element_type=F32)
        return m_new, l_new, acc_new

    cdiag = (i * tq) // tk

    def side_block(rows, on_right, base, st):
        ss = [lax.dot_general(jnp.where(on_right, q_right[idx], q_left[idx]), kaug_scr[idx, rows, :], _NT,
                              preferred_element_type=F32) for idx in range(2)]
        shift = jnp.where(on_right, -1.0, 1.0) * c * (base - i * tq).astype(F32)
        vb = v_ref[rows, :]
        return [update(st[idx], ss[idx], shift, vb) for idx in range(2)]

    def chunk_rows(cc):
        return pl.ds(pl.multiple_of(cc * tk, tk), tk)

    def square_block():
        st = [(jnp.full((tq, 1), NEG_INF, F32), jnp.zeros((tq, 1), F32), jnp.zeros((tq, LANES), F32))
              for _ in range(2)]
        sq = pl.ds(pl.multiple_of(i * tq, tq), tq)
        ss = [lax.dot_general(q_plain[idx], kaug_scr[idx, sq, :], _NT, preferred_element_type=F32)
              + ttab_ref[0] for idx in range(2)]
        vb = v_ref[sq, :]
        return [update(st[idx], ss[idx], 0.0, vb) for idx in range(2)]

    def diag_chunk():
        st = square_block()
        n_sub = tk // tq
        usq = i % n_sub
        dbase = cdiag * tk
        for w in range(1, n_sub):
            uo = usq + w
            on_right = uo < n_sub
            uo = jnp.where(on_right, uo, uo - n_sub)
            rows = pl.ds(pl.multiple_of(dbase + uo * tq, tq), tq)
            st = side_block(rows, on_right, dbase, st)
        return st

    def finish(st):
        hs = pl.ds(h, 1)
        lam = (jnp.exp(jnp.sum(dl_ref[0, hs, :] * dl_ref[1, hs, :], axis=-1, keepdims=True))
               - jnp.exp(jnp.sum(dl_ref[2, hs, :] * dl_ref[3, hs, :], axis=-1, keepdims=True)) + lam_init)
        o = st[0][2] / st[0][1] - lam * (st[1][2] / st[1][1])
        o = o * lax.rsqrt(jnp.mean(o * o, axis=-1, keepdims=True) + RMS_EPS) * subg_ref[...]
        o_ref[...] = (o * (1.0 - lam_init)).astype(BF16)

    def all_chunks():
        st = diag_chunk()
        for step in range(1, nk):
            ct = cdiag + step
            on_right = ct < nk
            cc = jnp.where(on_right, ct, ct - nk)
            st = side_block(chunk_rows(cc), on_right, cc * tk, st)
        finish(st)

    n_blocks = nk * (tk // tq)
    windows = tuple(w for w in _WINDOW_BLOCKS if w < n_blocks)
    if not windows:
        all_chunks()
        return

    qmax2 = jnp.max(jnp.sum(q * q, axis=-1, keepdims=True), axis=0, keepdims=True)
    need = (2.0 * jnp.sqrt(qmax2 * kmax_scr[0:1, 0:1]))[0, 0] + _UNDERFLOW_MARGIN
    far = 1e9
    n_sub = tk // tq
    taken = False
    for wb in windows:
        start = jnp.clip(i - wb // 2, 0, n_blocks - wb)
        d_left = jnp.where(start > 0, ((i - start) * tq).astype(F32), far)
        d_right = jnp.where(start + wb < n_blocks, ((start + wb - i - 1) * tq).astype(F32), far)
        fits = c * jnp.minimum(d_left, d_right) >= need
        use = fits if taken is False else jnp.logical_and(fits, jnp.logical_not(taken))

        @pl.when(use)
        def _(wb=wb, start=start):
            st = square_block()
            for w in range(wb - 1):
                bw = start + w
                blk = jnp.where(bw >= i, bw + 1, bw)
                rows = pl.ds(pl.multiple_of(blk * tq, tq), tq)
                st = side_block(rows, blk > i, (blk // n_sub) * tk, st)
            finish(st)

        taken = fits if taken is False else jnp.logical_or(taken, fits)

    @pl.when(jnp.logical_not(taken))
    def _():
        all_chunks()


def _diff_attention(z, dl, subg, tables, misc, *, batch, seq, lay, tq, tk):
    t = z.shape[0]
    nq = seq // tq
    qconst, kfeat, cvals, ttab = tables
    kern = functools.partial(_diff_kernel, tq=tq, tk=tk, nk=seq // tk)
    return pl.pallas_call(
        kern,
        out_shape=jax.ShapeDtypeStruct((t, B_HEADS * LANES), BF16),
        grid=(batch, B_HEADS, nq),
        in_specs=[pl.BlockSpec(memory_space=pltpu.SMEM),
                  pl.BlockSpec(memory_space=pltpu.SMEM),
                  pl.BlockSpec((tq, LANES), lambda b, h, i: (b * nq + i, lay.QB0 + h)),
                  pl.BlockSpec((seq, LANES), lambda b, h, i: (b, lay.KB0 + h)),
                  pl.BlockSpec((seq, LANES), lambda b, h, i: (b, lay.VB0 + h)),
                  pl.BlockSpec((1, 2, LANES), lambda b, h, i: (h, 0, 0)),
                  pl.BlockSpec(kfeat.shape, lambda b, h, i: (0, 0, 0)),
                  pl.BlockSpec((1, tq, tq), lambda b, h, i: (h, 0, 0)),
                  pl.BlockSpec(dl.shape, lambda b, h, i: (0, 0, 0)),
                  pl.BlockSpec((1, LANES), lambda b, h, i: (0, 0))],
        out_specs=pl.BlockSpec((tq, LANES), lambda b, h, i: (b * nq + i, h)),
        scratch_shapes=[pltpu.VMEM((2, seq, LANES), BF16), pltpu.VMEM((8, LANES), F32)],
        compiler_params=_params("parallel", "arbitrary", "arbitrary"),
        name="diff_attention",
    )(cvals, misc, z, z, z, qconst, kfeat, ttab, dl, subg)


def _merge_kernel(oa_ref, ob_ref, oc_ref, od_ref, gate_ref, wa_ref, wb_ref, wc_ref, wd_ref, o_ref, *, d):
    acc = None
    for idx, (o, w) in enumerate(((oa_ref, wa_ref), (ob_ref, wb_ref), (oc_ref, wc_ref), (od_ref, wd_ref))):
        y = jnp.dot(o[...], w[...], preferred_element_type=F32)
        g = _sigmoid(gate_ref[:, idx * d:(idx + 1) * d].astype(F32))
        acc = g * y if acc is None else acc + g * y
    o_ref[...] = acc.astype(BF16)


def _merge(oa, ob, oc, od, z, wbs, *, d, tm=256):
    t = oa.shape[0]
    tm = min(tm, t)
    row = lambda a: pl.BlockSpec((tm, a.shape[1]), lambda i: (i, 0))
    full = lambda a: pl.BlockSpec(a.shape, lambda i: (0, 0))
    return pl.pallas_call(
        functools.partial(_merge_kernel, d=d),
        out_shape=jax.ShapeDtypeStruct((t, d), BF16),
        grid=(t // tm,),
        in_specs=[row(oa), row(ob), row(oc), row(od),
                  pl.BlockSpec((tm, N_BRANCH * d), lambda i: (i, 0))] + [full(w) for w in wbs],
        out_specs=pl.BlockSpec((tm, d), lambda i: (i, 0)),
        compiler_params=_params("parallel"),
        name="branch_merge",
    )(oa, ob, oc, od, z, *wbs)


def _outproj_kernel(a_ref, w_ref, x_ref, g_ref, o_ref):
    y = jnp.dot(a_ref[...], w_ref[0], preferred_element_type=F32)
    o_ref[...] = x_ref[...] + g_ref[0] * y


def _outproj(a, w, layer, x2, gate, seq, tm=512):
    t, d = x2.shape
    tm = min(tm, seq)
    per_batch = seq // tm
    return pl.pallas_call(
        _outproj_kernel,
        out_shape=jax.ShapeDtypeStruct((t, d), F32),
        grid=(t // tm,),
        in_specs=[pl.BlockSpec((tm, a.shape[1]), lambda i: (i, 0)),
                  pl.BlockSpec((1,) + w.shape[1:], lambda i: (layer, 0, 0)),
                  pl.BlockSpec((tm, d), lambda i: (i, 0)),
                  pl.BlockSpec((1, 1, d), lambda i: (i // per_batch, 0, 0))],
        out_specs=pl.BlockSpec((tm, d), lambda i: (i, 0)),
        compiler_params=_params("parallel"),
        name="out_proj",
    )(a, w, x2, gate)


def _router_kernel(x_ref, mod_ref, g_ref, rwt_ref, rb_ref, h_ref, e_ref, gw_ref):
    h = _mod_norm(x_ref[...], g_ref[...], mod_ref[0, 3:4, :], mod_ref[0, 4:5, :])
    h_ref[...] = h
    logits = lax.dot_general(rwt_ref[...], h.astype(BF16), _NT, preferred_element_type=F32)
    score = _sigmoid(logits)
    sel = score + rb_ref[...]
    sel_r = [sel[e:e + 1, :] for e in range(N_EXPERTS)]
    sc_r = [score[e:e + 1, :] for e in range(N_EXPERTS)]
    n = EXPERTS_PER_GROUP

    def top2_sum(v):
        best = None
        for a in range(n):
            for b in range(a + 1, n):
                s = v[a] + v[b]
                best = s if best is None else jnp.maximum(best, s)
        return best

    gbest = top2_sum(sel_r[0:n])
    gidx = jnp.zeros(gbest.shape, jnp.int32)
    cv = list(sel_r[0:n])
    cs = list(sc_r[0:n])
    for g in range(1, N_GROUPS):
        gs = top2_sum(sel_r[g * n:(g + 1) * n])
        upd = gs > gbest
        gbest = jnp.where(upd, gs, gbest)
        gidx = jnp.where(upd, g, gidx)
        for j in range(n):
            cv[j] = jnp.where(upd, sel_r[g * n + j], cv[j])
            cs[j] = jnp.where(upd, sc_r[g * n + j], cs[j])
    b1, i1, s1 = cv[0], jnp.zeros(gbest.shape, jnp.int32), cs[0]
    for j in range(1, n):
        upd = cv[j] > b1
        b1 = jnp.where(upd, cv[j], b1)
        i1 = jnp.where(upd, j, i1)
        s1 = jnp.where(upd, cs[j], s1)
    low = -3.0e38
    b2 = jnp.full(gbest.shape, low, F32)
    i2 = jnp.zeros(gbest.shape, jnp.int32)
    s2 = jnp.zeros(gbest.shape, F32)
    for j in range(n):
        cand = jnp.where(i1 == j, low, cv[j])
        upd = cand > b2
        b2 = jnp.where(upd, cand, b2)
        i2 = jnp.where(upd, j, i2)
        s2 = jnp.where(upd, cs[j], s2)
    tot = s1 + s2
    e_ref[0:1, :] = gidx * n + i1
    e_ref[1:2, :] = gidx * n + i2
    gw_ref[0:1, :] = s1 / tot
    gw_ref[1:2, :] = s2 / tot


def _router(x2, mod, g, rwt, rb, seq, tm=512):
    t, d = x2.shape
    tm = min(tm, seq)
    per_batch = seq // tm
    return pl.pallas_call(
        _router_kernel,
        out_shape=(jax.ShapeDtypeStruct((t, d), F32),
                   jax.ShapeDtypeStruct((2, t), jnp.int32),
                   jax.ShapeDtypeStruct((2, t), F32)),
        grid=(t // tm,),
        in_specs=[pl.BlockSpec((tm, d), lambda i: (i, 0)),
                  pl.BlockSpec((1, 6, d), lambda i: (i // per_batch, 0, 0)),
                  pl.BlockSpec((1, d), lambda i: (0, 0)),
                  pl.BlockSpec(rwt.shape, lambda i: (0, 0)),
                  pl.BlockSpec(rb.shape, lambda i: (0, 0))],
        out_specs=(pl.BlockSpec((tm, d), lambda i: (i, 0)),
                   pl.BlockSpec((2, tm), lambda i: (0, i)),
                   pl.BlockSpec((2, tm), lambda i: (0, i))),
        compiler_params=_params("parallel"),
        name="norm_router",
    )(x2, mod, g, rwt, rb)


def _rank_kernel(e_ref, rank_ref, cnt_ref, carry):
    @pl.when(pl.program_id(0) == 0)
    def _():
        carry[...] = jnp.zeros(carry.shape, F32)

    tm = e_ref.shape[1]
    eio = lax.broadcasted_iota(jnp.int32, (N_EXPERTS, tm), 0)
    oh0 = eio == e_ref[0:1, :]
    oh1 = eio == e_ref[1:2, :]
    c = jnp.where(jnp.logical_or(oh0, oh1), 1.0, 0.0)
    before = (lax.broadcasted_iota(jnp.int32, (tm, tm), 0) < lax.broadcasted_iota(jnp.int32, (tm, tm), 1))
    pref = jnp.dot(c.astype(BF16), jnp.where(before, 1.0, 0.0).astype(BF16),
                   preferred_element_type=F32) + carry[...]
    rank_ref[0:1, :] = jnp.sum(jnp.where(oh0, pref, 0.0), axis=0, keepdims=True).astype(jnp.int32)
    rank_ref[1:2, :] = jnp.sum(jnp.where(oh1, pref, 0.0), axis=0, keepdims=True).astype(jnp.int32)
    total = carry[...] + jnp.sum(c, axis=1, keepdims=True)
    carry[...] = total
    cnt_ref[...] = jnp.broadcast_to(total, cnt_ref.shape)


def _rank(eidx, tm=512):
    t = eidx.shape[1]
    tm = min(tm, t)
    return pl.pallas_call(
        _rank_kernel,
        out_shape=(jax.ShapeDtypeStruct((2, t), jnp.int32),
                   jax.ShapeDtypeStruct((N_EXPERTS, LANES), F32)),
        grid=(t // tm,),
        in_specs=[pl.BlockSpec((2, tm), lambda i: (0, i))],
        out_specs=(pl.BlockSpec((2, tm), lambda i: (0, i)),
                   pl.BlockSpec((N_EXPERTS, LANES), lambda i: (0, 0))),
        scratch_shapes=[pltpu.VMEM((N_EXPERTS, 1), F32)],
        compiler_params=_params("arbitrary"),
        name="expert_rank",
    )(eidx)


_DMA_UNROLL = 8


def _row_copy(src_hbm, row, dst, slot, r, sem):
    return pltpu.make_async_copy(src_hbm.at[pl.ds(row, 1), :], dst.at[slot, pl.ds(r, 1), :], sem.at[slot])


def _expert_kernel(te_ref, nu_ref, src_ref, h_hbm, w1_ref, w3_ref, w2_ref, y_ref, xbuf, sem, *, tm):
    j = pl.program_id(0)
    n_used = nu_ref[0]

    def wait_tile(slot):
        pltpu.make_async_copy(h_hbm.at[pl.ds(0, tm), :], xbuf.at[slot], sem.at[slot]).wait()

    @pl.when(jnp.logical_and(j == 0, n_used > 0))
    def _():
        def issue(r, carry):
            _row_copy(h_hbm, src_ref[r], xbuf, 0, r, sem).start()
            return carry
        lax.fori_loop(0, tm, issue, 0, unroll=_DMA_UNROLL)

    @pl.when(j < n_used)
    def _():
        slot = j % 2
        wait_tile(slot)
        nxt = jnp.minimum(j + 1, n_used - 1) * tm
        for r in range(tm):
            _row_copy(h_hbm, src_ref[nxt + r], xbuf, 1 - slot, r, sem).start()
        xs = xbuf[slot].astype(BF16)
        g = jnp.dot(xs, w1_ref[0], preferred_element_type=F32)
        u = jnp.dot(xs, w3_ref[0], preferred_element_type=F32)
        mid = (g * _sigmoid(g) * u).astype(BF16)
        y_ref[...] = jnp.dot(mid, w2_ref[0], preferred_element_type=F32)

    @pl.when(j == n_used - 1)
    def _():
        wait_tile(1 - j % 2)

    @pl.when(j >= n_used)
    def _():
        y_ref[...] = jnp.zeros(y_ref.shape, F32)


def _experts(tile_e, n_used, src, h, w1, w3, w2, layer, tm):
    p = src.shape[0]
    d = h.shape[1]
    f = w1.shape[-1]
    n_exp = w1.shape[1]
    w1r = w1.reshape((-1,) + w1.shape[2:])
    w3r = w3.reshape((-1,) + w3.shape[2:])
    w2r = w2.reshape((-1,) + w2.shape[2:])
    wmap = lambda j, te, nu, src: (layer * n_exp + te[j], 0, 0)
    return pl.pallas_call(
        functools.partial(_expert_kernel, tm=tm),
        out_shape=jax.ShapeDtypeStruct((p, d), F32),
        grid_spec=pltpu.PrefetchScalarGridSpec(
            num_scalar_prefetch=3,
            grid=(p // tm,),
            in_specs=[pl.BlockSpec(memory_space=pl.ANY),
                      pl.BlockSpec((1, d, f), wmap),
                      pl.BlockSpec((1, d, f), wmap),
                      pl.BlockSpec((1, f, d), wmap)],
            out_specs=pl.BlockSpec((tm, d), lambda j, te, nu, src: (j, 0)),
            scratch_shapes=[pltpu.VMEM((2, tm, d), F32), pltpu.SemaphoreType.DMA((2,))]),
        compiler_params=_params("arbitrary"),
        name="experts",
    )(tile_e, n_used, src, h, w1r, w3r, w2r)


def _combine_kernel(pos_ref, x_ref, g_ref, gw_ref, y_hbm, o_ref, buf0, buf1, sem0, sem1, *, tm, t, n_steps):
    i = pl.program_id(0)

    def start_step(step, slot):
        def issue(r, carry):
            _row_copy(y_hbm, pos_ref[step * tm + r], buf0, slot, r, sem0).start(priority=0)
            _row_copy(y_hbm, pos_ref[t + step * tm + r], buf1, slot, r, sem1).start(priority=1)
            return carry
        lax.fori_loop(0, tm, issue, 0, unroll=_DMA_UNROLL)

    def wait_step(slot):
        pltpu.make_async_copy(y_hbm.at[pl.ds(0, tm), :], buf0.at[slot], sem0.at[slot]).wait()
        pltpu.make_async_copy(y_hbm.at[pl.ds(0, tm), :], buf1.at[slot], sem1.at[slot]).wait()

    @pl.when(i == 0)
    def _():
        start_step(0, 0)

    @pl.when(i + 1 < n_steps)
    def _():
        start_step(i + 1, (i + 1) % 2)

    slot = i % 2
    wait_step(slot)
    moe = gw_ref[:, 0:1] * buf0[slot] + gw_ref[:, 1:2] * buf1[slot]
    o_ref[...] = x_ref[...] + g_ref[0] * moe


def _combine(pos, x2, gate, gwt, y, seq, tm=256):
    t, d = x2.shape
    tm = min(tm, seq)
    per_batch = seq // tm
    n_steps = t // tm
    return pl.pallas_call(
        functools.partial(_combine_kernel, tm=tm, t=t, n_steps=n_steps),
        out_shape=jax.ShapeDtypeStruct((t, d), F32),
        grid_spec=pltpu.PrefetchScalarGridSpec(
            num_scalar_prefetch=1,
            grid=(n_steps,),
            in_specs=[pl.BlockSpec((tm, d), lambda i, pos: (i, 0)),
                      pl.BlockSpec((1, 1, d), lambda i, pos: (i // per_batch, 0, 0)),
                      pl.BlockSpec((tm, 2), lambda i, pos: (i, 0)),
                      pl.BlockSpec(memory_space=pl.ANY)],
            out_specs=pl.BlockSpec((tm, d), lambda i, pos: (i, 0)),
            scratch_shapes=[pltpu.VMEM((2, tm, d), F32), pltpu.VMEM((2, tm, d), F32),
                            pltpu.SemaphoreType.DMA((2,)), pltpu.SemaphoreType.DMA((2,))]),
        compiler_params=_params("arbitrary"),
        name="moe_combine",
    )(pos, x2, gate, gwt, y)


def _moe(x2, mod, g2n, rwt, rb, w1, w3, w2, layer, *, batch, seq, tm_e=256):
    t, d = x2.shape
    h, eidx, gw = _router(x2, mod, g2n, rwt, rb, seq)
    rank, cnt = _rank(eidx)
    tm_e = min(tm_e, t)
    counts = cnt[:, 0].astype(jnp.int32)
    gsz = ((counts + tm_e - 1) // tm_e) * tm_e
    ends = jnp.cumsum(gsz)
    offs = ends - gsz
    pos = rank
    for e in range(N_EXPERTS):
        pos = pos + jnp.where(eidx == e, offs[e], 0)
    n_tiles = (2 * t) // tm_e + N_EXPERTS
    p = n_tiles * tm_e
    flat = pos.reshape(-1)
    tok = jnp.broadcast_to(jnp.arange(t, dtype=jnp.int32)[None, :], (2, t)).reshape(-1)
    src = jnp.zeros((p,), jnp.int32).at[flat].set(tok)
    tile_start = jnp.arange(n_tiles, dtype=jnp.int32) * tm_e
    tile_e = jnp.minimum(jnp.sum(tile_start[:, None] >= ends[None, :], axis=1),
                         N_EXPERTS - 1).astype(jnp.int32)
    n_used = (ends[-1:] // tm_e).astype(jnp.int32)
    y = _experts(tile_e, n_used, src, h, w1, w3, w2, layer, tm_e)
    return _combine(flat, x2, mod[:, 5:6, :], gw.T, y, seq)


def kernel(x, c, ada_w, ada_b, norm1_g, norm2_g, w_in, qk_norm_a, qk_norm_b, qk_norm_c, qk_norm_d,
           diff_lambda, diff_subln_g, sink_logits, rpb, w_branch, w_out, router_w, router_b, w1, w3, w2):
    batch, seq, d = x.shape
    depth = ada_w.shape[0]
    t = batch * seq
    lay = _Layout(d)
    bw = lay.bw

    mod_all = _ada_mod(c, ada_w, ada_b)
    wz = _build_wz(w_in, lay)
    coef = jnp.stack([_build_coef(lay, qk_norm_a[l], qk_norm_b[l], qk_norm_c[l], qk_norm_d[l])
                      for l in range(depth)])
    wb16 = w_branch.astype(BF16)
    c_hd = bw // C_HEADS
    grp = C_HEADS // C_KV_HEADS
    wc_rows = []
    for hq in range(C_HEADS):
        blk = jnp.zeros((depth, LANES, d), BF16)
        off = (hq // grp) * c_hd
        wc_rows.append(blk.at[:, off:off + c_hd].set(wb16[:, 2, hq * c_hd:(hq + 1) * c_hd]))
    wc_ext = jnp.concatenate(wc_rows, axis=1)
    wout16 = w_out.astype(BF16)
    w1b, w3b, w2b = w1.astype(BF16), w3.astype(BF16), w2.astype(BF16)
    rwt = router_w.T.astype(BF16)
    rb = router_b.reshape(N_EXPERTS, 1).astype(F32)

    tab_a, side_a = _dilated_table(256)
    tab_c, side_c = _window_table(256)
    ones_mask = jnp.ones((A_HEADS, 1, LANES), BF16)
    lane = np.arange(LANES)
    cmask = np.stack([(lane // HALF) == (hq % 2) for hq in range(C_HEADS)]).astype(np.float32)
    cmask = jnp.asarray(cmask.reshape(C_HEADS, 1, LANES), BF16)
    no_sink = jnp.full((A_HEADS,), NEG_INF, F32)
    tq_b, tk_b = min(512, seq), min(1024, seq)
    tables_b = _diff_feature_tables(tq_b, tk_b)
    tab_d = _nbr_tables(rpb, seq // GRID_W)

    x2 = x.reshape(t, d)
    for l in range(depth):
        mod = mod_all[l, :batch].reshape(batch, 6, d)
        z = _inproj(x2, mod, norm1_g[l].reshape(1, d), wz, coef, l, lay, seq)
        o_a = _band_attention(z, tab_a, ones_mask, no_sink, batch=batch, seq=seq, n_groups=A_HEADS, hp=1,
                              q_width=LANES, q_of=lambda h: 0,
                              qblk=lambda g: lay.QA0 + g, kblk=lambda g: lay.KA0 + g,
                              vblk=lambda g: lay.VA0 + g, n_side=side_a)
        lam_init = 0.8 - 0.6 * float(np.exp(-0.3 * l))
        o_b = _diff_attention(z, diff_lambda[l], diff_subln_g[l].reshape(1, LANES), tables_b,
                              jnp.full((1,), lam_init, F32), batch=batch, seq=seq, lay=lay,
                              tq=tq_b, tk=tk_b)
        o_c = _band_attention(z, tab_c, cmask, sink_logits[l].astype(F32) * _LOG2E, batch=batch, seq=seq,
                              n_groups=C_KV_HEADS, hp=grp, q_width=(grp // 2) * LANES,
                              q_of=lambda h: h // 2,
                              qblk=lambda g: lay.QC0 // (grp // 2) + g, kblk=lambda g: lay.KC0 + g,
                              vblk=lambda g: lay.VC0, n_side=side_c)
        o_d = _nbr_attention(z, tab_d, l, batch=batch, seq=seq, lay=lay)
        merged = _merge(o_a, o_b, o_c, o_d, z, (wb16[l, 0], wb16[l, 1], wc_ext[l], wb16[l, 3]), d=d)
        x2 = _outproj(merged, wout16, l, x2, mod[:, 2:3, :], seq)
        x2 = _moe(x2, mod, norm2_g[l].reshape(1, d), rwt, rb, w1b, w3b, w2b, l, batch=batch, seq=seq)
    return x2.reshape(batch, seq, d)
```
